```python
import jax
import jax.numpy as jnp
from jax import lax
import numpy as np

D_MODEL = 1024
BATCH = 2
SEQ = 8192
DEPTH = 2

GRID_W = 64
CTX_LEN = 256
HEAD_DIM = 64
N_GROUPS = 4
MIX_WIDTH = D_MODEL
GROUP_WIDTH = MIX_WIDTH // N_GROUPS
GROUP_HEADS = GROUP_WIDTH // HEAD_DIM
Q_BLOCK = 128
ROPE_THETA = 10000.0
NORM_EPS = 1e-6
NEG_INF = -1e30
GQA_KV_HEADS = 2
MLA_Q_RANK = D_MODEL // 4
MLA_KV_RANK = D_MODEL // 8
MLA_NOPE_DIM = 64
MLA_ROPE_DIM = 32
MLA_V_DIM = GROUP_WIDTH // GROUP_HEADS
RWKV_DECAY_LORA = 64
RWKV_ICLR_LORA = 64
RWKV_GATE_LORA = 160
RWKV_GN_EPS = 64e-5
SWA_KV_HEADS = 2
WINDOW = 128
D_FF = 4 * D_MODEL

A_COLS = GROUP_WIDTH + 2 * GQA_KV_HEADS * HEAD_DIM
B_COLS = MLA_Q_RANK + MLA_KV_RANK + MLA_ROPE_DIM
C_COLS = 3 * GROUP_WIDTH + RWKV_DECAY_LORA + RWKV_ICLR_LORA + RWKV_GATE_LORA
D_COLS = GROUP_WIDTH + 2 * SWA_KV_HEADS * HEAD_DIM
IN_COLS = A_COLS + B_COLS + C_COLS + D_COLS
IN_CUTS = (A_COLS, A_COLS + B_COLS, A_COLS + B_COLS + C_COLS)
RWKV_CUTS = (GROUP_WIDTH, 2 * GROUP_WIDTH, 3 * GROUP_WIDTH,
             3 * GROUP_WIDTH + RWKV_DECAY_LORA,
             3 * GROUP_WIDTH + RWKV_DECAY_LORA + RWKV_ICLR_LORA)

kernel_name = 'hybrid_parallel_group_flow_block'


def rms_norm(x, g):
    xf = x.astype(jnp.float32)
    y = xf * lax.rsqrt(jnp.mean(xf * xf, axis=-1, keepdims=True) + NORM_EPS)
    return (y * g.astype(jnp.float32)).astype(x.dtype)


def rope_tables(row, col, rot_dim):
    n = rot_dim // 4
    inv = ROPE_THETA ** (-jnp.arange(n, dtype=jnp.float32) / n)
    ar = row[:, None] * inv
    ac = col[:, None] * inv
    return (jnp.cos(ar), jnp.sin(ar), jnp.cos(ac), jnp.sin(ac))


def _rotate(x, cos, sin):
    shape = (cos.shape[0],) + (1,) * (x.ndim - 3) + (cos.shape[1],)
    cs, sn = cos.reshape(shape), sin.reshape(shape)
    x1, x2 = jnp.split(x, 2, axis=-1)
    return jnp.concatenate([x1 * cs - x2 * sn, x1 * sn + x2 * cs], axis=-1)


def axial_rope(x, rope):
    cos_r, sin_r, cos_c, sin_c = rope
    xf = x.astype(jnp.float32)
    half = x.shape[-1] // 2
    out = jnp.concatenate([_rotate(xf[..., :half], cos_r, sin_r),
                           _rotate(xf[..., half:], cos_c, sin_c)], axis=-1)
    return out.astype(x.dtype)


def softmax_attend(q, k, v, sink):
    s = jnp.einsum('bqgrd,bkgd->bgrqk', q, k).astype(jnp.float32) * (q.shape[-1] ** -0.5)
    if sink is None:
        p = jax.nn.softmax(s, axis=-1)
    else:
        sk = jnp.broadcast_to(sink.astype(jnp.float32)[None, :, :, None, None], s.shape[:-1] + (1,))
        p = jax.nn.softmax(jnp.concatenate([sk, s], axis=-1), axis=-1)[..., 1:]
    return jnp.einsum('bgrqk,bkgd->bqgrd', p.astype(v.dtype), v)


def blocked_attend(q, k, v):
    B, T, G, R, dk = q.shape
    nb = T // Q_BLOCK
    qb = jnp.moveaxis(q.reshape(B, nb, Q_BLOCK, G, R, dk), 1, 0)
    ob = lax.map(lambda qi: softmax_attend(qi, k, v, None), qb)
    return jnp.moveaxis(ob, 0, 1).reshape(B, T, G, R, v.shape[-1])


def banded_window_attend(q, k, v, k_ctx, v_ctx, sink):
    B, T, G, R, d = q.shape
    W = WINDOW
    nb = T // W

    def band(t):
        tp = jnp.pad(t, ((0, 0), (W, W), (0, 0), (0, 0))).reshape(B, nb + 2, W, G, t.shape[-1])
        return jnp.concatenate([tp[:, :-2], tp[:, 1:-1], tp[:, 2:]], axis=2)

    kb, vb = band(k), band(v)
    qb = q.reshape(B, nb, W, G, R, d)
    scale = d ** -0.5
    s_win = jnp.einsum('bnqgrd,bnkgd->bgrnqk', qb, kb).astype(jnp.float32) * scale
    s_ctx = jnp.einsum('bnqgrd,bcgd->bgrnqc', qb, k_ctx).astype(jnp.float32) * scale
    qpos = jnp.arange(T).reshape(nb, W)[:, :, None]
    kpos = (jnp.arange(nb)[:, None] * W - W + jnp.arange(3 * W)[None, :])[:, None, :]
    valid = (jnp.abs(qpos - kpos) <= W) & (kpos >= 0) & (kpos < T)
    s_win = jnp.where(valid, s_win, NEG_INF)
    sk = jnp.broadcast_to(sink.astype(jnp.float32)[None, :, :, None, None, None], s_win.shape[:-1] + (1,))
    p = jax.nn.softmax(jnp.concatenate([sk, s_win, s_ctx], axis=-1), axis=-1)
    p_win = p[..., 1:1 + 3 * W].astype(v.dtype)
    p_ctx = p[..., 1 + 3 * W:].astype(v.dtype)
    o = (jnp.einsum('bgrnqk,bnkgd->bnqgrd', p_win, vb)
         + jnp.einsum('bgrnqc,bcgd->bnqgrd', p_ctx, v_ctx))
    return o.reshape(B, T, G, R, d)


def split_gqa(z, n_kv):
    B, T, _ = z.shape
    q, k, v = jnp.split(z, (GROUP_WIDTH, GROUP_WIDTH + n_kv * HEAD_DIM), axis=-1)
    return (q.reshape(B, T, n_kv, GROUP_HEADS // n_kv, HEAD_DIM),
            k.reshape(B, T, n_kv, HEAD_DIM),
            v.reshape(B, T, n_kv, HEAD_DIM))


def mla_qkv(z, p, rope):
    B, T, _ = z.shape
    cq, ckv, kr = jnp.split(z, (MLA_Q_RANK, MLA_Q_RANK + MLA_KV_RANK), axis=-1)
    q = (rms_norm(cq, p['mla_q_norm']) @ p['mla_w_uq']).reshape(B, T, GROUP_HEADS, MLA_NOPE_DIM + MLA_ROPE_DIM)
    kv = (rms_norm(ckv, p['mla_kv_norm']) @ p['mla_w_ukv']).reshape(B, T, GROUP_HEADS, MLA_NOPE_DIM + MLA_V_DIM)
    q_nope, q_rope = jnp.split(q, (MLA_NOPE_DIM,), axis=-1)
    k_nope, v = jnp.split(kv, (MLA_NOPE_DIM,), axis=-1)
    kr = kr.reshape(B, T, 1, MLA_ROPE_DIM)
    if rope is not None:
        q_rope = axial_rope(q_rope, rope)
        kr = axial_rope(kr, rope)
    q = jnp.concatenate([q_nope, q_rope], axis=-1)[:, :, :, None, :]
    k = jnp.concatenate([k_nope, jnp.broadcast_to(kr, (B, T, GROUP_HEADS, MLA_ROPE_DIM))], axis=-1)
    return q, k, v


def token_shift_centred(z, mu):
    zp = jnp.pad(z, ((0, 0), (1, 1), (0, 0)))
    return z + mu * (0.5 * (zp[:, :-2] + zp[:, 2:]) - z)


def rwkv_prepare(z, p):
    z = token_shift_centred(z, p['rwkv_mu'])
    r, k, v, wl, al, gl = jnp.split(z, RWKV_CUTS, axis=-1)
    B, T, _ = z.shape

    def heads(t):
        return t.reshape(B, T, GROUP_HEADS, HEAD_DIM).astype(jnp.float32)

    wt = jnp.tanh(wl)
    dirs = []
    for d in range(2):
        w_log = -jax.nn.softplus(-(p['rwkv_w0'][d] + wt @ p['rwkv_w2'][d])) - 0.5
        decay = jnp.exp(-jnp.exp(w_log.astype(jnp.float32)))
        a = jax.nn.sigmoid(p['rwkv_a0'][d] + al @ p['rwkv_a2'][d])
        kk = heads(k * p['rwkv_k_k'][d])
        kk = kk / jnp.maximum(jnp.sqrt(jnp.sum(kk * kk, axis=-1, keepdims=True)), 1e-12)
        kd = heads(k * (1.0 + (a - 1.0) * p['rwkv_k_a'][d]))
        dirs.append((heads(decay), kd, -kk, kk * heads(a)))
    return heads(r), heads(v), gl, dirs


def rwkv_scan(S0, r, w, k, v, a, b, reverse):
    def step(S, inp):
        r_t, w_t, k_t, v_t, a_t, b_t = inp
        sa = jnp.einsum('bhij,bhj->bhi', S, a_t)
        S = S * w_t[:, :, None, :] + sa[..., :, None] * b_t[:, :, None, :] + v_t[..., :, None] * k_t[:, :, None, :]
        return S, jnp.einsum('bhij,bhj->bhi', S, r_t)

    xs = tuple(jnp.moveaxis(t, 1, 0) for t in (r, w, k, v, a, b))
    S, ys = lax.scan(step, S0, xs, reverse=reverse)
    return S, jnp.moveaxis(ys, 0, 1)


def rwkv_output(y, r, v, dirs, gl, p, dtype):
    B, T, H, N = y.shape
    mean = jnp.mean(y, axis=-1, keepdims=True)
    var = jnp.mean(jnp.square(y - mean), axis=-1, keepdims=True)
    yn = ((y - mean) * lax.rsqrt(var + RWKV_GN_EPS)).reshape(B, T, H * N)
    yn = yn * p['rwkv_ln_w'].astype(jnp.float32) + p['rwkv_ln_b'].astype(jnp.float32)
    r_k = p['rwkv_r_k'].astype(jnp.float32)
    bonus = ((jnp.sum(r * dirs[0][1] * r_k, axis=-1, keepdims=True)
              + jnp.sum(r * dirs[1][1] * r_k, axis=-1, keepdims=True)) * v).reshape(B, T, H * N)
    g = (jax.nn.sigmoid(gl) @ p['rwkv_g2']).astype(jnp.float32)
    return ((yn + bonus) * g).astype(dtype)


def rwkv_mix(zr, zrc, p, with_ctx_out):
    r_c, v_c, gl_c, dirs_c = rwkv_prepare(zrc, p)
    r_l, v_l, gl_l, dirs_l = rwkv_prepare(zr, p)
    S0 = jnp.zeros((zr.shape[0], GROUP_HEADS, HEAD_DIM, HEAD_DIM), jnp.float32)
    ys_c, ys_l = [], []
    for d, rev in enumerate((False, True)):
        w_c, k_c, a_c, b_c = dirs_c[d]
        S_ctx, y_c = rwkv_scan(S0, r_c, w_c, k_c, v_c, a_c, b_c, rev)
        w_l, k_l, a_l, b_l = dirs_l[d]
        _, y_l = rwkv_scan(S_ctx, r_l, w_l, k_l, v_l, a_l, b_l, rev)
        ys_c.append(y_c)
        ys_l.append(y_l)
    out = rwkv_output(ys_l[0] + ys_l[1], r_l, v_l, dirs_l, gl_l, p, zr.dtype)
    out_c = rwkv_output(ys_c[0] + ys_c[1], r_c, v_c, dirs_c, gl_c, p, zrc.dtype) if with_ctx_out else None
    return out, out_c


def merge_groups(outs):
    return jnp.concatenate([o.reshape(o.shape[0], o.shape[1], -1) for o in outs], axis=-1)


def token_mixer(h, hc, p, rope_hd, rope_mla, with_ctx_out):
    za, zb, zr, zd = jnp.split(h @ p['w_in'], IN_CUTS, axis=-1)
    zac, zbc, zrc, zdc = jnp.split(hc @ p['w_in'], IN_CUTS, axis=-1)
    outs, outs_c = [], []
    q, k, v = split_gqa(za, GQA_KV_HEADS)
    qc, kc, vc = split_gqa(zac, GQA_KV_HEADS)
    q = axial_rope(rms_norm(q, p['gqa_q_norm']), rope_hd)
    k = axial_rope(rms_norm(k, p['gqa_k_norm']), rope_hd)
    qc = rms_norm(qc, p['gqa_q_norm'])
    kc = rms_norm(kc, p['gqa_k_norm'])
    outs.append(blocked_attend(q, jnp.concatenate([k, kc], axis=1), jnp.concatenate([v, vc], axis=1)))
    if with_ctx_out:
        outs_c.append(softmax_attend(qc, kc, vc, None))
    q, k, v = mla_qkv(zb, p, rope_mla)
    qc, kc, vc = mla_qkv(zbc, p, None)
    outs.append(blocked_attend(q, jnp.concatenate([k, kc], axis=1), jnp.concatenate([v, vc], axis=1)))
    if with_ctx_out:
        outs_c.append(softmax_attend(qc, kc, vc, None))
    o, oc = rwkv_mix(zr, zrc, p, with_ctx_out)
    outs.append(o)
    if with_ctx_out:
        outs_c.append(oc)
    q, k, v = split_gqa(zd, SWA_KV_HEADS)
    qc, kc, vc = split_gqa(zdc, SWA_KV_HEADS)
    q = axial_rope(q, rope_hd)
    k = axial_rope(k, rope_hd)
    sink = p['swa_sink'].reshape(SWA_KV_HEADS, GROUP_HEADS // SWA_KV_HEADS)
    outs.append(banded_window_attend(q, k, v, kc, vc, sink))
    if with_ctx_out:
        outs_c.append(softmax_attend(qc, kc, vc, sink))
    y = merge_groups(outs) @ p['w_out']
    yc = merge_groups(outs_c) @ p['w_out'] if with_ctx_out else None
    return y, yc


def sq_relu_mlp(h, w1, w2):
    return jnp.square(jax.nn.relu(h @ w1)) @ w2


def setup_inputs(seed: int = 0) -> dict:
    key = jax.random.key(seed)
    keys = iter(jax.random.split(key, 40))

    def nrm(shape, scale):
        return jax.random.normal(next(keys), shape, jnp.float32) * scale

    def gain(shape):
        return 1.0 + nrm(shape, 0.02)

    def unif(shape, lo, hi):
        return jax.random.uniform(next(keys), shape, jnp.float32, lo, hi)

    L, D, H, N, GW = DEPTH, D_MODEL, GROUP_HEADS, HEAD_DIM, GROUP_WIDTH
    return {
        'x': nrm((BATCH, SEQ, D), 1.0),
        'c': nrm((BATCH, D), 1.0),
        'ctx': nrm((BATCH, CTX_LEN, D), 1.0),
        'c_ctx': nrm((D,), 1.0),
        'w_mod': nrm((L, D, 6 * D), 0.5 * D ** -0.5),
        'b_mod': nrm((L, 6 * D), 0.02),
        'g_pre_mix': gain((L, D)),
        'g_post_mix': gain((L, D)),
        'g_pre_mlp': gain((L, D)),
        'g_post_mlp': gain((L, D)),
        'w_in': nrm((L, D, IN_COLS), D ** -0.5),
        'gqa_q_norm': gain((L, N)),
        'gqa_k_norm': gain((L, N)),
        'mla_q_norm': gain((L, MLA_Q_RANK)),
        'mla_kv_norm': gain((L, MLA_KV_RANK)),
        'mla_w_uq': nrm((L, MLA_Q_RANK, H * (MLA_NOPE_DIM + MLA_ROPE_DIM)), MLA_Q_RANK ** -0.5),
        'mla_w_ukv': nrm((L, MLA_KV_RANK, H * (MLA_NOPE_DIM + MLA_V_DIM)), MLA_KV_RANK ** -0.5),
        'rwkv_mu': unif((L, C_COLS), 0.0, 1.0),
        'rwkv_w0': unif((L, 2, GW), -6.0, -1.0),
        'rwkv_w2': nrm((L, 2, RWKV_DECAY_LORA, GW), 0.1 * RWKV_DECAY_LORA ** -0.5),
        'rwkv_a0': nrm((L, 2, GW), 0.1),
        'rwkv_a2': nrm((L, 2, RWKV_ICLR_LORA, GW), 0.5 * RWKV_ICLR_LORA ** -0.5),
        'rwkv_k_k': 0.85 + nrm((L, 2, GW), 0.05),
        'rwkv_k_a': 1.0 + nrm((L, 2, GW), 0.05),
        'rwkv_r_k': nrm((L, H, N), 0.1),
        'rwkv_g2': nrm((L, RWKV_GATE_LORA, GW), RWKV_GATE_LORA ** -0.5),
        'rwkv_ln_w': gain((L, GW)),
        'rwkv_ln_b': nrm((L, GW), 0.02),
        'swa_sink': nrm((L, H), 0.5),
        'w_out': nrm((L, MIX_WIDTH, D), MIX_WIDTH ** -0.5),
        'w_mlp1': nrm((L, D, D_FF), D ** -0.5),
        'w_mlp2': nrm((L, D_FF, D), D_FF ** -0.5),
    }


def reference(x, c, ctx, c_ctx, w_mod, b_mod, g_pre_mix, g_post_mix, g_pre_mlp, g_post_mlp, w_in,
              gqa_q_norm, gqa_k_norm, mla_q_norm, mla_kv_norm, mla_w_uq, mla_w_ukv,
              rwkv_mu, rwkv_w0, rwkv_w2, rwkv_a0, rwkv_a2, rwkv_k_k, rwkv_k_a, rwkv_r_k, rwkv_g2,
              rwkv_ln_w, rwkv_ln_b, swa_sink, w_out, w_mlp1, w_mlp2):
    T = x.shape[1]
    ROWS = T // GRID_W
    row = jnp.broadcast_to(jnp.arange(ROWS, dtype=jnp.float32)[:, None], (ROWS, GRID_W)).reshape(-1)
    col = jnp.broadcast_to(jnp.arange(GRID_W, dtype=jnp.float32)[None, :], (ROWS, GRID_W)).reshape(-1)
    rope_hd = rope_tables(row, col, HEAD_DIM)
    rope_mla = rope_tables(row, col, MLA_ROPE_DIM)
    silu_c = jax.nn.silu(c)
    silu_cc = jax.nn.silu(c_ctx)[None, :]
    xc = ctx
    for l in range(DEPTH):
        last = l == DEPTH - 1
        m = (silu_c @ w_mod[l] + b_mod[l])[:, None, :]
        mc = (silu_cc @ w_mod[l] + b_mod[l])[:, None, :]
        sh1, sc1, gt1, sh2, sc2, gt2 = jnp.split(m, 6, axis=-1)
        sh1c, sc1c, gt1c, sh2c, sc2c, gt2c = jnp.split(mc, 6, axis=-1)
        p = {
            'w_in': w_in[l], 'w_out': w_out[l],
            'gqa_q_norm': gqa_q_norm[l], 'gqa_k_norm': gqa_k_norm[l],
            'mla_q_norm': mla_q_norm[l], 'mla_kv_norm': mla_kv_norm[l],
            'mla_w_uq': mla_w_uq[l], 'mla_w_ukv': mla_w_ukv[l],
            'rwkv_mu': rwkv_mu[l], 'rwkv_w0': rwkv_w0[l], 'rwkv_w2': rwkv_w2[l],
            'rwkv_a0': rwkv_a0[l], 'rwkv_a2': rwkv_a2[l], 'rwkv_k_k': rwkv_k_k[l],
            'rwkv_k_a': rwkv_k_a[l], 'rwkv_r_k': rwkv_r_k[l], 'rwkv_g2': rwkv_g2[l],
            'rwkv_ln_w': rwkv_ln_w[l], 'rwkv_ln_b': rwkv_ln_b[l], 'swa_sink': swa_sink[l],
        }
        h = rms_norm(x, g_pre_mix[l]) * (1.0 + sc1) + sh1
        hc = rms_norm(xc, g_pre_mix[l]) * (1.0 + sc1c) + sh1c
        y, yc = token_mixer(h, hc, p, rope_hd, rope_mla, not last)
        x = x + gt1 * rms_norm(y, g_post_mix[l])
        h = rms_norm(x, g_pre_mlp[l]) * (1.0 + sc2) + sh2
        x = x + gt2 * rms_norm(sq_relu_mlp(h, w_mlp1[l], w_mlp2[l]), g_post_mlp[l])
        if not last:
            xc = xc + gt1c * rms_norm(yc, g_post_mix[l])
            hc = rms_norm(xc, g_pre_mlp[l]) * (1.0 + sc2c) + sh2c
            xc = xc + gt2c * rms_norm(sq_relu_mlp(hc, w_mlp1[l], w_mlp2[l]), g_post_mlp[l])
    return x
```

```python
import functools
import math

import jax
import jax.numpy as jnp
from jax import lax
from jax.experimental import pallas as pl
from jax.experimental.pallas import tpu as pltpu

F32 = jnp.float32
BF16 = jnp.bfloat16

D_MODEL = 1024
GRID_W = 64
HEAD_DIM = 64
GROUP_WIDTH = 256
GROUP_HEADS = 4
ROPE_THETA = 10000.0
NORM_EPS = 1e-6
NEG_INF = -1e30
MLA_Q_RANK = 256
MLA_KV_RANK = 128
MLA_NOPE_DIM = 64
MLA_ROPE_DIM = 32
MLA_QK_DIM = MLA_NOPE_DIM + MLA_ROPE_DIM
RWKV_DECAY_LORA = 64
RWKV_ICLR_LORA = 64
RWKV_GATE_LORA = 160
RWKV_GN_EPS = 64e-5
WINDOW = 128
D_FF = 4 * D_MODEL

LANES = 128
TM = 256
CHUNK = 64
SEC_A = 0
SEC_B = 512
SEC_D = 1024
ABD_COLS = 1536
C_COLS_PAD = 1152
VMEM_LIMIT = 56 * 1024 * 1024


def _cparams(sem):
    return pltpu.CompilerParams(dimension_semantics=sem, vmem_limit_bytes=VMEM_LIMIT)


def _dot(a, b):
    return jnp.dot(a, b, preferred_element_type=F32)


def _dot_nt(a, b):
    return lax.dot_general(a, b, (((1,), (1,)), ((), ())), preferred_element_type=F32)


def _dot_tn(a, b):
    return lax.dot_general(a, b, (((0,), (0,)), ((), ())), preferred_element_type=F32)


def _split2(x):
    hi = x.astype(BF16)
    lo = (x - hi.astype(F32)).astype(BF16)
    return hi, lo


def _mm_exact_rhs(a, b_bf16):
    hi, lo = _split2(a)
    return _dot(hi, b_bf16) + _dot(lo, b_bf16)


def _mm_exact_lhs(a_bf16, b):
    hi, lo = _split2(b)
    return _dot(a_bf16, hi) + _dot(a_bf16, lo)


def _mm3(a, b):
    ah, al = _split2(a)
    bh, bl = _split2(b)
    return _dot(ah, bh) + (_dot(ah, bl) + _dot(al, bh))


def _sigmoid(x):
    return 1.0 / (1.0 + jnp.exp(-x))


def _rms(x, eps):
    return x * lax.rsqrt(jnp.mean(x * x, axis=-1, keepdims=True) + eps)


def _mod_kernel(ct_ref, w_ref, b_ref, o_ref, *, n_rows):
    ct = ct_ref[...]
    st = ct * _sigmoid(ct)
    w = w_ref[...]
    rows = [jnp.sum(st[:, r:r + 1] * w, axis=0, keepdims=True) for r in range(n_rows)]
    rows.append(jnp.zeros((8 - n_rows, w.shape[1]), F32))
    o_ref[...] = jnp.concatenate(rows, axis=0) + b_ref[...]


def _modulation(ct, w_mod, b_mod, n_rows):
    depth, d, n6 = w_mod.shape
    tn = 1536
    return pl.pallas_call(
        functools.partial(_mod_kernel, n_rows=n_rows),
        grid=(depth, n6 // tn),
        in_specs=[pl.BlockSpec((d, 8), lambda l, j: (0, 0)),
                  pl.BlockSpec((None, d, tn), lambda l, j: (l, 0, j)),
                  pl.BlockSpec((None, 1, tn), lambda l, j: (l, 0, j))],
        out_specs=pl.BlockSpec((None, 8, tn), lambda l, j: (l, 0, j)),
        out_shape=jax.ShapeDtypeStruct((depth, 8, n6), F32),
        compiler_params=_cparams(("parallel", "parallel")),
        name="modulation",
    )(ct, w_mod, b_mod.reshape(depth, 1, n6))


def _inproj_kernel(x_ref, mod_ref, g_ref, w_ref, zabd_ref, zc_ref):
    d = D_MODEL
    mod = mod_ref[...]
    h = _rms(x_ref[...], NORM_EPS) * g_ref[...] * (1.0 + mod[:, d:2 * d]) + mod[:, 0:d]
    z = _dot(h.astype(BF16), w_ref[...])
    zabd_ref[...] = z[:, :ABD_COLS]
    zc_ref[...] = z[:, ABD_COLS:]


def _inproj(xs, mod3, g, w_in_p, mod_row):
    n = xs.shape[0]
    cols = w_in_p.shape[1]
    return pl.pallas_call(
        _inproj_kernel,
        grid=(n // TM,),
        in_specs=[pl.BlockSpec((TM, D_MODEL), lambda i: (i, 0)),
                  pl.BlockSpec((None, 1, 6 * D_MODEL), lambda i: (mod_row(i), 0, 0)),
                  pl.BlockSpec((1, D_MODEL), lambda i: (0, 0)),
                  pl.BlockSpec((D_MODEL, cols), lambda i: (0, 0))],
        out_specs=[pl.BlockSpec((TM, ABD_COLS), lambda i: (i, 0)),
                   pl.BlockSpec((TM, C_COLS_PAD), lambda i: (i, 0))],
        out_shape=[jax.ShapeDtypeStruct((n, ABD_COLS), F32),
                   jax.ShapeDtypeStruct((n, C_COLS_PAD), F32)],
        compiler_params=_cparams(("parallel",)),
        name="inproj",
    )(xs, mod3, g, w_in_p)


def _rope_slab(x, cos, sin_signed, shift):
    left = pltpu.roll(x, LANES - shift, axis=1)
    right = pltpu.roll(x, shift, axis=1)
    lane = lax.broadcasted_iota(jnp.int32, x.shape, 1)
    first = ((lane // shift) % 2) == 0
    return x * cos + jnp.where(first, left, right) * sin_signed


def _head_norm_slab(x, gain, ones):
    ss = _mm_exact_rhs(x * x, ones)
    return x * lax.rsqrt(ss * (1.0 / HEAD_DIM) + NORM_EPS) * gain


def _attn_prep_kernel(z_ref, cs_ref, sn_ref, csm_ref, snm_ref, qn_ref, kn_ref, mqn_ref, mkvn_ref,
                      wuq_ref, wukv_ref, ones_ref,
                      qa_ref, ka_ref, va_ref, qb_ref, kb_ref, vb_ref, qd_ref, kd_ref, vd_ref):
    cs, sn = cs_ref[...], sn_ref[...]
    csm, snm = csm_ref[...], snm_ref[...]
    ones = ones_ref[...]
    hd = HEAD_DIM
    q_scale = HEAD_DIM ** -0.5

    def put_heads(ref, base, slab):
        ref[base] = slab[:, :hd].astype(BF16)
        ref[base + 1] = slab[:, hd:].astype(BF16)

    for s in range(2):
        x = z_ref[:, SEC_A + LANES * s:SEC_A + LANES * (s + 1)]
        x = _rope_slab(_head_norm_slab(x, qn_ref[...], ones), cs, sn, 16) * q_scale
        put_heads(qa_ref, 2 * s, x)
    x = z_ref[:, SEC_A + 256:SEC_A + 384]
    put_heads(ka_ref, 0, _rope_slab(_head_norm_slab(x, kn_ref[...], ones), cs, sn, 16))
    put_heads(va_ref, 0, z_ref[:, SEC_A + 384:SEC_A + 512])

    for s in range(2):
        x = z_ref[:, SEC_D + LANES * s:SEC_D + LANES * (s + 1)]
        put_heads(qd_ref, 2 * s, _rope_slab(x, cs, sn, 16) * q_scale)
    put_heads(kd_ref, 0, _rope_slab(z_ref[:, SEC_D + 256:SEC_D + 384], cs, sn, 16))
    put_heads(vd_ref, 0, z_ref[:, SEC_D + 384:SEC_D + 512])

    cq = _rms(z_ref[:, SEC_B:SEC_B + MLA_Q_RANK], NORM_EPS) * mqn_ref[...]
    q = _dot(cq.astype(BF16), wuq_ref[...])
    ckv = _rms(z_ref[:, SEC_B + 256:SEC_B + 384], NORM_EPS) * mkvn_ref[...]
    kv = _dot(ckv.astype(BF16), wukv_ref[...])
    kr = _rope_slab(z_ref[:, SEC_B + 384:SEC_B + 512], csm, snm, 8)
    b_scale = MLA_QK_DIM ** -0.5
    for h in range(GROUP_HEADS):
        qh = q[:, LANES * h:LANES * (h + 1)]
        qb_ref[h] = (_rope_slab(qh, csm, snm, 8) * b_scale).astype(BF16)
        kb_ref[h] = (kv[:, LANES * h:LANES * (h + 1)] + kr).astype(BF16)
        vb_ref[h] = kv[:, 4 * LANES + hd * h:4 * LANES + hd * (h + 1)].astype(BF16)


def _attn_prep(zabd, tabs, p, n_tiles):
    n = zabd.shape[0]
    cs, sn, csm, snm = tabs

    def tab_spec():
        return pl.BlockSpec((TM, LANES), lambda i: (i % n_tiles, 0))

    def row_spec(w):
        return pl.BlockSpec((1, w), lambda i: (0, 0))

    def heads_out(nh, w):
        return (pl.BlockSpec((nh, TM, w), lambda i: (0, i, 0)),
                jax.ShapeDtypeStruct((nh, n, w), BF16))

    outs = [heads_out(4, 64), heads_out(2, 64), heads_out(2, 64),
            heads_out(4, 128), heads_out(4, 128), heads_out(4, 64),
            heads_out(4, 64), heads_out(2, 64), heads_out(2, 64)]
    return pl.pallas_call(
        _attn_prep_kernel,
        grid=(n // TM,),
        in_specs=[pl.BlockSpec((TM, ABD_COLS), lambda i: (i, 0)),
                  tab_spec(), tab_spec(), tab_spec(), tab_spec(),
                  row_spec(LANES), row_spec(LANES), row_spec(MLA_Q_RANK), row_spec(MLA_KV_RANK),
                  pl.BlockSpec((MLA_Q_RANK, 512), lambda i: (0, 0)),
                  pl.BlockSpec((MLA_KV_RANK, 768), lambda i: (0, 0)),
                  pl.BlockSpec((LANES, LANES), lambda i: (0, 0))],
        out_specs=[o[0] for o in outs],
        out_shape=[o[1] for o in outs],
        compiler_params=_cparams(("parallel",)),
        name="attn_prep",
    )(zabd, cs, sn, csm, snm, p['qn'], p['kn'], p['mqn'], p['mkvn'], p['wuq'], p['wukv'], p['ones64'])


def _flash_kernel(q_ref, k_ref, v_ref, o_ref, *, shared_kv, tq, tk, ctx_len, s_tot):
    qi = pl.program_id(2)
    n_chunks = jnp.where(qi * tq < ctx_len, ctx_len // tk, s_tot // tk)
    dv = v_ref.shape[-1]

    def attend(q, kv):
        m_rows = q.shape[0]

        def body(c, carry):
            m, l, acc = carry
            start = pl.multiple_of(c * tk, tk)
            k = k_ref[kv, pl.ds(start, tk), :]
            v = v_ref[kv, pl.ds(start, tk), :]
            s = _dot_nt(q, k)
            m_new = jnp.maximum(m, jnp.max(s, axis=-1, keepdims=True))
            alpha = jnp.exp(m - m_new)
            pr = jnp.exp(s - m_new)
            l = alpha * l + jnp.sum(pr, axis=-1, keepdims=True)
            acc = alpha * acc + _dot(pr.astype(BF16), v)
            return m_new, l, acc

        init = (jnp.full((m_rows, 1), NEG_INF, F32), jnp.zeros((m_rows, 1), F32),
                jnp.zeros((m_rows, dv), F32))
        _, l, acc = lax.fori_loop(0, n_chunks, body, init)
        return acc / l

    if shared_kv:
        o = attend(q_ref[...].reshape(2 * tq, q_ref.shape[-1]), 0)
        o0, o1 = o[:tq], o[tq:]
    else:
        o0, o1 = attend(q_ref[0], 0), attend(q_ref[1], 1)
    o_ref[...] = jnp.concatenate([o0, o1], axis=-1).astype(o_ref.dtype)


def _flash(q, k, v, *, shared_kv, batch, s_tot, ctx_len):
    nh, n, dk = q.shape
    dv = v.shape[-1]
    nkv = 1 if shared_kv else 2
    tq = tk = TM
    nq = s_tot // tq
    kern = functools.partial(_flash_kernel, shared_kv=shared_kv, tq=tq, tk=tk,
                             ctx_len=ctx_len, s_tot=s_tot)
    return pl.pallas_call(
        kern,
        grid=(batch, nh // 2, nq),
        in_specs=[pl.BlockSpec((2, tq, dk), lambda b, p, i: (p, b * nq + i, 0)),
                  pl.BlockSpec((nkv, s_tot, dk), lambda b, p, i: (p, b, 0)),
                  pl.BlockSpec((nkv, s_tot, dv), lambda b, p, i: (p, b, 0))],
        out_specs=pl.BlockSpec((tq, 2 * dv), lambda b, p, i: (b * nq + i, p)),
        out_shape=jax.ShapeDtypeStruct((n, nh * dv), BF16),
        compiler_params=_cparams(("parallel", "parallel", "arbitrary")),
        name="flash_shared" if shared_kv else "flash_split",
    )(q, k, v)


def _window_kernel(sink_ref, q_ref, k_ref, v_ref, o_ref, *, ctx_len, s_tot):
    w = WINDOW
    g = pl.program_id(1)
    jb = pl.program_id(2)
    first_lat = ctx_len // w
    last_blk = s_tot // w - 1
    is_lat = jb >= first_lat
    pb = jnp.clip(jb - 1, 0, last_blk)
    nb = jnp.clip(jb + 1, 0, last_blk)

    def rows(ref, blk):
        return ref[pl.ds(pl.multiple_of(blk * w, w), w), :]

    q = q_ref[...].reshape(2 * w, HEAD_DIM)
    kw = jnp.concatenate([rows(k_ref, pb), rows(k_ref, jb), rows(k_ref, nb)], axis=0)
    vw = jnp.concatenate([rows(v_ref, pb), rows(v_ref, jb), rows(v_ref, nb)], axis=0)
    s_w = _dot_nt(q, kw)
    s_c = _dot_nt(q, k_ref[0:ctx_len, :])
    r = lax.broadcasted_iota(jnp.int32, s_w.shape, 0) % w
    c = lax.broadcasted_iota(jnp.int32, s_w.shape, 1)
    lo = jnp.where(is_lat, jnp.where(jb - 1 >= first_lat, 0, w), 3 * w)
    hi = jnp.where(jb + 1 <= last_blk, 3 * w - 1, 2 * w - 1)
    valid = jnp.logical_and(c >= jnp.maximum(r, lo), c <= jnp.minimum(r + 2 * w, hi))
    s_w = jnp.where(valid, s_w, NEG_INF)
    row = lax.broadcasted_iota(jnp.int32, (2 * w, 1), 0)
    sk = jnp.where(row < w, sink_ref[2 * g], sink_ref[2 * g + 1])
    m = jnp.maximum(jnp.maximum(jnp.max(s_w, axis=-1, keepdims=True),
                                jnp.max(s_c, axis=-1, keepdims=True)), sk)
    p_w = jnp.exp(s_w - m)
    p_c = jnp.exp(s_c - m)
    denom = jnp.exp(sk - m) + jnp.sum(p_w, axis=-1, keepdims=True) + jnp.sum(p_c, axis=-1, keepdims=True)
    o = (_dot(p_w.astype(BF16), vw) + _dot(p_c.astype(BF16), v_ref[0:ctx_len, :])) / denom
    o_ref[...] = jnp.concatenate([o[:w], o[w:]], axis=-1).astype(o_ref.dtype)


def _window_attn(sink, q, k, v, *, batch, s_tot, ctx_len):
    nh, n, dk = q.shape
    nblk = s_tot // WINDOW
    kern = functools.partial(_window_kernel, ctx_len=ctx_len, s_tot=s_tot)
    return pl.pallas_call(
        kern,
        grid=(batch, nh // 2, nblk),
        in_specs=[pl.BlockSpec(memory_space=pltpu.SMEM),
                  pl.BlockSpec((2, WINDOW, dk), lambda b, g, j: (g, b * nblk + j, 0)),
                  pl.BlockSpec((None, s_tot, dk), lambda b, g, j: (g, b, 0)),
                  pl.BlockSpec((None, s_tot, dk), lambda b, g, j: (g, b, 0))],
        out_specs=pl.BlockSpec((WINDOW, 2 * dk), lambda b, g, j: (b * nblk + j, g)),
        out_shape=jax.ShapeDtypeStruct((n, nh * dk), BF16),
        compiler_params=_cparams(("parallel", "parallel", "arbitrary")),
        name="window_attn",
    )(sink, q, k, v)


def _rwkv_prep_kernel(z_ref, zp_ref, zn_ref, mu_ref, w0_ref, w2_ref, a0_ref, a2_ref, kk_ref, ka_ref,
                      rk_ref, g2_ref, ones_ref,
                      lw_ref, kd_ref, a_ref, b_ref, r_ref, v_ref, g_ref, bonus_ref, *, n_tiles):
    j = pl.program_id(0) % n_tiles
    z = z_ref[...]
    rows = lax.broadcasted_iota(jnp.int32, z.shape, 0)
    prev_row = jnp.where(j <= 1, 0.0, zp_ref[7:8, :])
    next_row = jnp.where(jnp.logical_or(j == 0, j == n_tiles - 1), 0.0, zn_ref[0:1, :])
    z_prev = jnp.where(rows == 0, prev_row, pltpu.roll(z, 1, axis=0))
    z_next = jnp.where(rows == TM - 1, next_row, pltpu.roll(z, TM - 1, axis=0))
    zs = z + mu_ref[...] * (0.5 * (z_prev + z_next) - z)

    gw = GROUP_WIDTH
    r, k, v = zs[:, 0:gw], zs[:, gw:2 * gw], zs[:, 2 * gw:3 * gw]
    lora = zs[:, 3 * gw:3 * gw + LANES]
    gl = zs[:, 3 * gw + LANES:3 * gw + LANES + gw]
    wt = jnp.tanh(lora)
    ones = ones_ref[...]

    def seg_sum(x):
        return jnp.concatenate([_mm_exact_rhs(x[:, :LANES], ones), _mm_exact_rhs(x[:, LANES:], ones)], axis=-1)

    r_ref[...] = r
    v_ref[...] = v
    g_ref[...] = _mm3(_sigmoid(gl), g2_ref[...])
    kd_sum = None
    for d in range(2):
        u = w0_ref[d] + _mm3(wt, w2_ref[d])
        soft = jnp.maximum(-u, 0.0) + jnp.log(1.0 + jnp.exp(-jnp.abs(u)))
        lw_ref[d] = -jnp.exp(-soft - 0.5)
        gate = _sigmoid(a0_ref[d] + _mm3(lora, a2_ref[d]))
        kk = k * kk_ref[d]
        kk = kk / jnp.maximum(jnp.sqrt(seg_sum(kk * kk)), 1e-12)
        kd = k * (1.0 + (gate - 1.0) * ka_ref[d])
        kd_ref[d] = kd
        a_ref[d] = -kk
        b_ref[d] = kk * gate
        kd_sum = kd if kd_sum is None else kd_sum + kd
    bonus_ref[...] = seg_sum(r * kd_sum * rk_ref[...]) * v


def _rwkv_prep(zc, p, n_tiles):
    n = zc.shape[0]
    gw = GROUP_WIDTH
    nb8 = n // 8

    def full(shape):
        nd = len(shape)
        return pl.BlockSpec(shape, lambda i: (0,) * nd)

    dir_out = (pl.BlockSpec((2, TM, gw), lambda i: (0, i, 0)), jax.ShapeDtypeStruct((2, n, gw), F32))
    one_out = (pl.BlockSpec((TM, gw), lambda i: (i, 0)), jax.ShapeDtypeStruct((n, gw), F32))
    outs = [dir_out] * 4 + [one_out] * 4
    return pl.pallas_call(
        functools.partial(_rwkv_prep_kernel, n_tiles=n_tiles),
        grid=(n // TM,),
        in_specs=[pl.BlockSpec((TM, C_COLS_PAD), lambda i: (i, 0)),
                  pl.BlockSpec((8, C_COLS_PAD), lambda i: (jnp.maximum(i * (TM // 8) - 1, 0), 0)),
                  pl.BlockSpec((8, C_COLS_PAD), lambda i: (jnp.minimum((i + 1) * (TM // 8), nb8 - 1), 0)),
                  full((1, C_COLS_PAD)),
                  full((2, 1, gw)), full((2, LANES, gw)), full((2, 1, gw)), full((2, LANES, gw)),
                  full((2, 1, gw)), full((2, 1, gw)), full((1, gw)), full((gw, gw)),
                  full((LANES, LANES))],
        out_specs=[o[0] for o in outs],
        out_shape=[o[1] for o in outs],
        compiler_params=_cparams(("parallel",)),
        name="rwkv_prep",
    )(zc, zc, zc, p['mu'], p['w0'], p['w2p'], p['a0'], p['a2p'], p['k_k'], p['k_a'], p['r_k'],
      p['g2p'], p['ones64'])


def _rwkv_scan_kernel(lw_ref, k_ref, a_ref, b_ref, r_ref, v_ref, y_ref, s_ref,
                      wa_s, u0_s, yr_s, y0_s, bt_s, kt_s, *, reverse):
    @pl.when(pl.program_id(1) == 0)
    def _():
        s_ref[...] = jnp.zeros_like(s_ref)

    n_chunks = TM // CHUNK
    hd = HEAD_DIM
    row = lax.broadcasted_iota(jnp.int32, (TM, TM), 0)
    col = lax.broadcasted_iota(jnp.int32, (TM, TM), 1)
    same = (row // CHUNK) == (col // CHUNK)
    before = (col > row) if reverse else (col < row)
    m_strict = jnp.logical_and(same, before)
    m_incl = jnp.logical_and(same, jnp.logical_or(before, row == col))
    eye = (row == col).astype(F32)

    lw = lw_ref[...]
    cum = _mm_exact_lhs(m_incl.astype(BF16), lw)
    e_pos = jnp.exp(cum)
    e_neg = jnp.exp(-cum)
    at_all = a_ref[...] * jnp.exp(cum - lw)
    rt_all = r_ref[...] * e_pos
    bt_all = b_ref[...] * e_neg
    kt_all = k_ref[...] * e_neg
    v_all = v_ref[...]
    bt_s[...] = bt_all
    kt_s[...] = kt_all

    for h in range(GROUP_HEADS):
        sl = slice(hd * h, hd * (h + 1))
        at, rt, bt, kt = at_all[:, sl], rt_all[:, sl], bt_all[:, sl], kt_all[:, sl]
        vb = v_all[:, sl].astype(BF16)
        at16, rt16, bt16, kt16 = at.astype(BF16), rt.astype(BF16), bt.astype(BF16), kt.astype(BF16)
        ab = jnp.where(m_strict, _dot_nt(at16, bt16), 0.0)
        ak = jnp.where(m_strict, _dot_nt(at16, kt16), 0.0)
        rb = jnp.where(m_incl, _dot_nt(rt16, bt16), 0.0)
        rk = jnp.where(m_incl, _dot_nt(rt16, kt16), 0.0)
        pw = ab
        tinv = eye + ab
        for _ in range(int(math.log2(CHUNK)) - 1):
            pw = _mm3(pw, pw)
            tinv = tinv + _mm3(tinv, pw)
        x = jnp.concatenate([at, _dot(ak.astype(BF16), vb)], axis=-1)
        wu = _mm_exact_rhs(tinv, x.astype(BF16))
        wa, u0 = wu[:, :hd], wu[:, hd:]
        rb16 = rb.astype(BF16)
        wa_s[h] = wa
        u0_s[h] = u0
        yr_s[h] = rt + _dot(rb16, wa.astype(BF16))
        y0_s[h] = _dot(rb16, u0.astype(BF16)) + _dot(rk.astype(BF16), vb)

    order = range(n_chunks - 1, -1, -1) if reverse else range(n_chunks)
    for c in order:
        rs = slice(CHUNK * c, CHUNK * (c + 1))
        last = CHUNK * c if reverse else CHUNK * (c + 1) - 1
        decay = e_pos[last:last + 1, :]
        y_heads = []
        for h in range(GROUP_HEADS):
            sl = slice(hd * h, hd * (h + 1))
            s0 = s_ref[h]
            y_heads.append(_mm3nt(yr_s[h, rs, :], s0) + y0_s[h, rs, :])
            u = _mm3nt(wa_s[h, rs, :], s0) + u0_s[h, rs, :]
            upd = _mm3tn(u, bt_s[rs, sl]) + _mm3tn(v_all[rs, sl], kt_s[rs, sl])
            s_ref[h] = (s0 + upd) * decay[:, sl]
        y_ref[rs, :] = jnp.concatenate(y_heads, axis=-1)


def _mm3nt(a, b):
    ah, al = _split2(a)
    bh, bl = _split2(b)
    return _dot_nt(ah, bh) + (_dot_nt(ah, bl) + _dot_nt(al, bh))


def _mm3tn(a, b):
    ah, al = _split2(a)
    bh, bl = _split2(b)
    return _dot_tn(ah, bh) + (_dot_tn(ah, bl) + _dot_tn(al, bh))


def _rwkv_scan(lw, kd, a, b, r, v, *, d, batch, n_tiles):
    n = r.shape[0]
    gw = GROUP_WIDTH
    reverse = d == 1

    def tile(b_, s):
        if reverse:
            t = jnp.where(s == 0, 0, n_tiles - s)
        else:
            t = s
        return b_ * n_tiles + t

    dspec = pl.BlockSpec((None, TM, gw), lambda b_, s: (d, tile(b_, s), 0))
    spec = pl.BlockSpec((TM, gw), lambda b_, s: (tile(b_, s), 0))
    hs = pltpu.VMEM((GROUP_HEADS, TM, HEAD_DIM), F32)
    return pl.pallas_call(
        functools.partial(_rwkv_scan_kernel, reverse=reverse),
        grid=(batch, n_tiles),
        in_specs=[dspec, dspec, dspec, dspec, spec, spec],
        out_specs=spec,
        out_shape=jax.ShapeDtypeStruct((n, gw), F32),
        scratch_shapes=[pltpu.VMEM((GROUP_HEADS, HEAD_DIM, HEAD_DIM), F32), hs, hs, hs, hs,
                        pltpu.VMEM((TM, gw), F32), pltpu.VMEM((TM, gw), F32)],
        compiler_params=_cparams(("parallel", "arbitrary")),
        name="rwkv_scan_rev" if reverse else "rwkv_scan_fwd",
    )(lw, kd, a, b, r, v)


def _rwkv_out_kernel(yf_ref, yr_ref, bonus_ref, g_ref, lnw_ref, lnb_ref, ones_ref, o_ref):
    ones = ones_ref[...]

    def seg_mean(x):
        s = jnp.concatenate([_mm_exact_rhs(x[:, :LANES], ones), _mm_exact_rhs(x[:, LANES:], ones)], axis=-1)
        return s * (1.0 / HEAD_DIM)

    y = yf_ref[...] + yr_ref[...]
    dlt = y - seg_mean(y)
    yn = dlt * lax.rsqrt(seg_mean(dlt * dlt) + RWKV_GN_EPS) * lnw_ref[...] + lnb_ref[...]
    o_ref[...] = ((yn + bonus_ref[...]) * g_ref[...]).astype(o_ref.dtype)


def _rwkv_out(yf, yr, bonus, g, p):
    n, gw = yf.shape
    spec = pl.BlockSpec((TM, gw), lambda i: (i, 0))
    row = pl.BlockSpec((1, gw), lambda i: (0, 0))
    return pl.pallas_call(
        _rwkv_out_kernel,
        grid=(n // TM,),
        in_specs=[spec, spec, spec, spec, row, row, pl.BlockSpec((LANES, LANES), lambda i: (0, 0))],
        out_specs=spec,
        out_shape=jax.ShapeDtypeStruct((n, gw), BF16),
        compiler_params=_cparams(("parallel",)),
        name="rwkv_out",
    )(yf, yr, bonus, g, p['ln_w'], p['ln_b'], p['ones64'])


def _out_mlp_kernel(x_ref, oa_ref, ob_ref, oc_ref, od_ref, mod_ref, gpost_ref, gpre_ref, gpm_ref,
                    wo_ref, w1_ref, w2_ref, xo_ref):
    d = D_MODEL
    gw = GROUP_WIDTH
    mod = mod_ref[...]
    y = _dot(oa_ref[...], wo_ref[0:gw, :])
    y += _dot(ob_ref[...], wo_ref[gw:2 * gw, :])
    y += _dot(oc_ref[...], wo_ref[2 * gw:3 * gw, :])
    y += _dot(od_ref[...], wo_ref[3 * gw:4 * gw, :])
    x1 = x_ref[...] + mod[:, 2 * d:3 * d] * (_rms(y, NORM_EPS) * gpost_ref[...])
    h = _rms(x1, NORM_EPS) * gpre_ref[...] * (1.0 + mod[:, 4 * d:5 * d]) + mod[:, 3 * d:4 * d]
    u = jnp.maximum(_dot(h.astype(BF16), w1_ref[...]), 0.0)
    zz = _dot((u * u).astype(BF16), w2_ref[...])
    xo_ref[...] = x1 + mod[:, 5 * d:6 * d] * (_rms(zz, NORM_EPS) * gpm_ref[...])


def _out_mlp(xs, oa, ob, oc, od, mod3, p, mod_row):
    n = xs.shape[0]
    d = D_MODEL
    gw = GROUP_WIDTH
    ospec = pl.BlockSpec((TM, gw), lambda i: (i, 0))
    row = pl.BlockSpec((1, d), lambda i: (0, 0))

    def wspec(shape):
        return pl.BlockSpec(shape, lambda i: (0, 0), pipeline_mode=pl.Buffered(1))

    return pl.pallas_call(
        _out_mlp_kernel,
        grid=(n // TM,),
        in_specs=[pl.BlockSpec((TM, d), lambda i: (i, 0)), ospec, ospec, ospec, ospec,
                  pl.BlockSpec((None, 1, 6 * d), lambda i: (mod_row(i), 0, 0)),
                  row, row, row,
                  wspec((d, d)), wspec((d, D_FF)), wspec((D_FF, d))],
        out_specs=pl.BlockSpec((TM, d), lambda i: (i, 0)),
        out_shape=jax.ShapeDtypeStruct((n, d), F32),
        compiler_params=_cparams(("parallel",)),
        name="out_mlp",
    )(xs, oa, ob, oc, od, mod3, p['g_post_mix'], p['g_pre_mlp'], p['g_post_mlp'],
      p['w_out'], p['w_mlp1'], p['w_mlp2'])


def _rope_tables(seq, ctx_len):
    t = jnp.arange(seq, dtype=jnp.int32)
    row = (t // GRID_W).astype(F32)
    col = (t % GRID_W).astype(F32)

    def tables(rot_dim, lane_dim):
        n = rot_dim // 4
        inv = ROPE_THETA ** (-jnp.arange(n, dtype=F32) / n)
        ar, ac = row[:, None] * inv, col[:, None] * inv
        cos = jnp.concatenate([jnp.cos(ar), jnp.cos(ar), jnp.cos(ac), jnp.cos(ac)], axis=-1)
        sin = jnp.concatenate([-jnp.sin(ar), jnp.sin(ar), -jnp.sin(ac), jnp.sin(ac)], axis=-1)
        return cos, sin

    def with_ctx(tab, fill):
        return jnp.concatenate([jnp.full((ctx_len, tab.shape[1]), fill, F32), tab], axis=0)

    cos, sin = tables(HEAD_DIM, LANES)
    cs = with_ctx(jnp.tile(cos, (1, LANES // HEAD_DIM)), 1.0)
    sn = with_ctx(jnp.tile(sin, (1, LANES // HEAD_DIM)), 0.0)
    cosm, sinm = tables(MLA_ROPE_DIM, LANES)
    pad_l, pad_r = MLA_NOPE_DIM, LANES - MLA_QK_DIM
    csm = with_ctx(jnp.pad(cosm, ((0, 0), (pad_l, pad_r)), constant_values=1.0), 1.0)
    snm = with_ctx(jnp.pad(sinm, ((0, 0), (pad_l, pad_r))), 0.0)
    return cs, sn, csm, snm


def _layer_params(l, w):
    d = D_MODEL
    gw = GROUP_WIDTH
    w_in = w['w_in'][l]
    a_end, b_end, c_end = 512, 512 + 416, 512 + 416 + 1056
    zeros = lambda c: jnp.zeros((d, c), F32)
    w_in_p = jnp.concatenate([
        w_in[:, :a_end],
        w_in[:, a_end:a_end + 384], zeros(MLA_NOPE_DIM), w_in[:, a_end + 384:b_end], zeros(LANES - MLA_QK_DIM),
        w_in[:, c_end:],
        w_in[:, b_end:c_end], zeros(C_COLS_PAD - 1056)], axis=1).astype(BF16)
    wuq = w['mla_w_uq'][l].reshape(MLA_Q_RANK, GROUP_HEADS, MLA_QK_DIM)
    wuq = jnp.pad(wuq, ((0, 0), (0, 0), (0, LANES - MLA_QK_DIM))).reshape(MLA_Q_RANK, GROUP_HEADS * LANES)
    wukv = w['mla_w_ukv'][l].reshape(MLA_KV_RANK, GROUP_HEADS, MLA_NOPE_DIM + HEAD_DIM)
    wk = jnp.pad(wukv[:, :, :MLA_NOPE_DIM], ((0, 0), (0, 0), (0, LANES - MLA_NOPE_DIM)))
    wukv_p = jnp.concatenate([wk.reshape(MLA_KV_RANK, GROUP_HEADS * LANES),
                              wukv[:, :, MLA_NOPE_DIM:].reshape(MLA_KV_RANK, gw)], axis=1)
    lane = jnp.arange(LANES)
    ones64 = ((lane[:, None] // HEAD_DIM) == (lane[None, :] // HEAD_DIM)).astype(BF16)
    zl = jnp.zeros((2, LANES - RWKV_DECAY_LORA, gw), F32)
    return {
        'w_in_p': w_in_p,
        'g_pre_mix': w['g_pre_mix'][l].reshape(1, d),
        'g_post_mix': w['g_post_mix'][l].reshape(1, d),
        'g_pre_mlp': w['g_pre_mlp'][l].reshape(1, d),
        'g_post_mlp': w['g_post_mlp'][l].reshape(1, d),
        'qn': jnp.tile(w['gqa_q_norm'][l], 2).reshape(1, LANES),
        'kn': jnp.tile(w['gqa_k_norm'][l], 2).reshape(1, LANES),
        'mqn': w['mla_q_norm'][l].reshape(1, MLA_Q_RANK),
        'mkvn': w['mla_kv_norm'][l].reshape(1, MLA_KV_RANK),
        'wuq': wuq.astype(BF16),
        'wukv': wukv_p.astype(BF16),
        'ones64': ones64,
        'mu': jnp.pad(w['rwkv_mu'][l], (0, C_COLS_PAD - 1056)).reshape(1, C_COLS_PAD),
        'w0': w['rwkv_w0'][l].reshape(2, 1, gw),
        'w2p': jnp.concatenate([w['rwkv_w2'][l], zl], axis=1),
        'a0': w['rwkv_a0'][l].reshape(2, 1, gw),
        'a2p': jnp.concatenate([zl, w['rwkv_a2'][l]], axis=1),
        'k_k': w['rwkv_k_k'][l].reshape(2, 1, gw),
        'k_a': w['rwkv_k_a'][l].reshape(2, 1, gw),
        'r_k': w['rwkv_r_k'][l].reshape(1, gw),
        'g2p': jnp.pad(w['rwkv_g2'][l], ((0, gw - RWKV_GATE_LORA), (0, 0))),
        'ln_w': w['rwkv_ln_w'][l].reshape(1, gw),
        'ln_b': w['rwkv_ln_b'][l].reshape(1, gw),
        'sink': w['swa_sink'][l],
        'w_out': w['w_out'][l].astype(BF16),
        'w_mlp1': w['w_mlp1'][l].astype(BF16),
        'w_mlp2': w['w_mlp2'][l].astype(BF16),
    }


def kernel(x, c, ctx, c_ctx, w_mod, b_mod, g_pre_mix, g_post_mix, g_pre_mlp, g_post_mlp, w_in, gqa_q_norm, gqa_k_norm, mla_q_norm, mla_kv_norm, mla_w_uq, mla_w_ukv, rwkv_mu, rwkv_w0, rwkv_w2, rwkv_a0, rwkv_a2, rwkv_k_k, rwkv_k_a, rwkv_r_k, rwkv_g2, rwkv_ln_w, rwkv_ln_b, swa_sink, w_out, w_mlp1, w_mlp2):
    batch, seq, d = x.shape
    ctx_len = ctx.shape[1]
    depth = w_mod.shape[0]
    assert d == D_MODEL and ctx_len == TM and seq % TM == 0 and seq % GRID_W == 0
    assert batch + 1 <= 8
    s_tot = ctx_len + seq
    n_tiles = s_tot // TM
    w = dict(w_in=w_in, g_pre_mix=g_pre_mix, g_post_mix=g_post_mix, g_pre_mlp=g_pre_mlp,
             g_post_mlp=g_post_mlp, gqa_q_norm=gqa_q_norm, gqa_k_norm=gqa_k_norm, mla_q_norm=mla_q_norm,
             mla_kv_norm=mla_kv_norm, mla_w_uq=mla_w_uq, mla_w_ukv=mla_w_ukv, rwkv_mu=rwkv_mu,
             rwkv_w0=rwkv_w0, rwkv_w2=rwkv_w2, rwkv_a0=rwkv_a0, rwkv_a2=rwkv_a2, rwkv_k_k=rwkv_k_k,
             rwkv_k_a=rwkv_k_a, rwkv_r_k=rwkv_r_k, rwkv_g2=rwkv_g2, rwkv_ln_w=rwkv_ln_w,
             rwkv_ln_b=rwkv_ln_b, swa_sink=swa_sink, w_out=w_out, w_mlp1=w_mlp1, w_mlp2=w_mlp2)

    ct = jnp.concatenate([c, c_ctx[None, :], jnp.zeros((8 - batch - 1, d), F32)], axis=0).T
    mods = _modulation(ct, w_mod, b_mod, batch + 1)

    def mod_row(i):
        return jnp.where(i % n_tiles == 0, batch, i // n_tiles)

    tabs = _rope_tables(seq, ctx_len)
    xs = jnp.concatenate([ctx, x], axis=1).reshape(batch * s_tot, d)
    geo = dict(batch=batch, s_tot=s_tot, ctx_len=ctx_len)
    for l in range(depth):
        p = _layer_params(l, w)
        mod3 = mods[l].reshape(8, 1, 6 * d)
        zabd, zc = _inproj(xs, mod3, p['g_pre_mix'], p['w_in_p'], mod_row)
        qa, ka, va, qb, kb, vb, qd, kd, vd = _attn_prep(zabd, tabs, p, n_tiles)
        oa = _flash(qa, ka, va, shared_kv=True, **geo)
        ob = _flash(qb, kb, vb, shared_kv=False, **geo)
        od = _window_attn(p['sink'], qd, kd, vd, **geo)
        lw, rkd, ra, rb, rr, rv, rg, bonus = _rwkv_prep(zc, p, n_tiles)
        yf = _rwkv_scan(lw, rkd, ra, rb, rr, rv, d=0, batch=batch, n_tiles=n_tiles)
        yr = _rwkv_scan(lw, rkd, ra, rb, rr, rv, d=1, batch=batch, n_tiles=n_tiles)
        oc = _rwkv_out(yf, yr, bonus, rg, p)
        xs = _out_mlp(xs, oa, ob, oc, od, mod3, p, mod_row)
    return xs.reshape(batch, s_tot, d)[:, ctx_len:, :]
```

```python
import functools
import math

import jax
import jax.numpy as jnp
from jax import lax
from jax.experimental import pallas as pl
from jax.experimental.pallas import tpu as pltpu

F32 = jnp.float32
BF16 = jnp.bfloat16

D_MODEL = 1024
GRID_W = 64
HEAD_DIM = 64
GROUP_WIDTH = 256
GROUP_HEADS = 4
ROPE_THETA = 10000.0
NORM_EPS = 1e-6
NEG_INF = -1e30
MLA_Q_RANK = 256
MLA_KV_RANK = 128
MLA_NOPE_DIM = 64
MLA_ROPE_DIM = 32
MLA_QK_DIM = MLA_NOPE_DIM + MLA_ROPE_DIM
RWKV_DECAY_LORA = 64
RWKV_ICLR_LORA = 64
RWKV_GATE_LORA = 160
RWKV_GN_EPS = 64e-5
WINDOW = 128
D_FF = 4 * D_MODEL

LOG2E = math.log2(math.e)
LANES = 128
TM = 256
CHUNK = 64
SEC_A = 0
SEC_B = 512
SEC_D = 1024
ABD_COLS = 1536
C_COLS_PAD = 1152
VMEM_LIMIT = 56 * 1024 * 1024


def _cparams(sem):
    return pltpu.CompilerParams(dimension_semantics=sem, vmem_limit_bytes=VMEM_LIMIT)


def _dot(a, b):
    return jnp.dot(a, b, preferred_element_type=F32)


def _dot_nt(a, b):
    return lax.dot_general(a, b, (((1,), (1,)), ((), ())), preferred_element_type=F32)


def _dot_tn(a, b):
    return lax.dot_general(a, b, (((0,), (0,)), ((), ())), preferred_element_type=F32)


def _split2(x):
    hi = x.astype(BF16)
    lo = (x - hi.astype(F32)).astype(BF16)
    return hi, lo


def _mm_exact_rhs(a, b_bf16):
    hi, lo = _split2(a)
    return _dot(hi, b_bf16) + _dot(lo, b_bf16)


def _mm_exact_lhs(a_bf16, b):
    hi, lo = _split2(b)
    return _dot(a_bf16, hi) + _dot(a_bf16, lo)


def _mm3(a, b):
    ah, al = _split2(a)
    bh, bl = _split2(b)
    return _dot(ah, bh) + (_dot(ah, bl) + _dot(al, bh))


def _sigmoid(x):
    return 1.0 / (1.0 + jnp.exp(-x))


def _rms(x, eps):
    return x * lax.rsqrt(jnp.mean(x * x, axis=-1, keepdims=True) + eps)


def _mod_kernel(ct_ref, w_ref, b_ref, o_ref, *, n_rows):
    ct = ct_ref[...]
    st = ct * _sigmoid(ct)
    w = w_ref[...]
    rows = [jnp.sum(st[:, r:r + 1] * w, axis=0, keepdims=True) for r in range(n_rows)]
    rows.append(jnp.zeros((8 - n_rows, w.shape[1]), F32))
    o_ref[...] = jnp.concatenate(rows, axis=0) + b_ref[...]


def _modulation(ct, w_mod, b_mod, n_rows):
    depth, d, n6 = w_mod.shape
    tn = 1536
    return pl.pallas_call(
        functools.partial(_mod_kernel, n_rows=n_rows),
        grid=(depth, n6 // tn),
        in_specs=[pl.BlockSpec((d, 8), lambda l, j: (0, 0)),
                  pl.BlockSpec((None, d, tn), lambda l, j: (l, 0, j)),
                  pl.BlockSpec((None, 1, tn), lambda l, j: (l, 0, j))],
        out_specs=pl.BlockSpec((None, 8, tn), lambda l, j: (l, 0, j)),
        out_shape=jax.ShapeDtypeStruct((depth, 8, n6), F32),
        compiler_params=_cparams(("parallel", "parallel")),
        name="modulation",
    )(ct, w_mod, b_mod.reshape(depth, 1, n6))


def _inproj_kernel(x_ref, mod_ref, g_ref, w_ref, zabd_ref, zc_ref):
    d = D_MODEL
    mod = mod_ref[...]
    h = _rms(x_ref[...], NORM_EPS) * g_ref[...] * (1.0 + mod[:, d:2 * d]) + mod[:, 0:d]
    z = _dot(h.astype(BF16), w_ref[...])
    zabd_ref[...] = z[:, :ABD_COLS]
    zc_ref[...] = z[:, ABD_COLS:]


def _inproj(xs, mod3, g, w_in_p, mod_row):
    n = xs.shape[0]
    cols = w_in_p.shape[1]
    return pl.pallas_call(
        _inproj_kernel,
        grid=(n // TM,),
        in_specs=[pl.BlockSpec((TM, D_MODEL), lambda i: (i, 0)),
                  pl.BlockSpec((None, 1, 6 * D_MODEL), lambda i: (mod_row(i), 0, 0)),
                  pl.BlockSpec((1, D_MODEL), lambda i: (0, 0)),
                  pl.BlockSpec((D_MODEL, cols), lambda i: (0, 0))],
        out_specs=[pl.BlockSpec((TM, ABD_COLS), lambda i: (i, 0)),
                   pl.BlockSpec((TM, C_COLS_PAD), lambda i: (i, 0))],
        out_shape=[jax.ShapeDtypeStruct((n, ABD_COLS), F32),
                   jax.ShapeDtypeStruct((n, C_COLS_PAD), F32)],
        compiler_params=_cparams(("parallel",)),
        name="inproj",
    )(xs, mod3, g, w_in_p)


def _rope_slab(x, cos, sin_signed, shift):
    left = pltpu.roll(x, LANES - shift, axis=1)
    right = pltpu.roll(x, shift, axis=1)
    lane = lax.broadcasted_iota(jnp.int32, x.shape, 1)
    first = ((lane // shift) % 2) == 0
    return x * cos + jnp.where(first, left, right) * sin_signed


def _head_norm_slab(x, gain, ones):
    ss = _mm_exact_rhs(x * x, ones)
    return x * lax.rsqrt(ss * (1.0 / HEAD_DIM) + NORM_EPS) * gain


def _attn_prep_kernel(z_ref, cs_ref, sn_ref, csm_ref, snm_ref, qn_ref, kn_ref, mqn_ref, mkvn_ref,
                      wuq_ref, wukv_ref, ones_ref,
                      qa_ref, ka_ref, va_ref, qb_ref, kb_ref, vb_ref, qd_ref, kd_ref, vd_ref):
    cs, sn = cs_ref[...], sn_ref[...]
    csm, snm = csm_ref[...], snm_ref[...]
    ones = ones_ref[...]
    hd = HEAD_DIM
    q_scale = HEAD_DIM ** -0.5

    def put_heads(ref, base, slab):
        ref[base] = slab[:, :hd].astype(BF16)
        ref[base + 1] = slab[:, hd:].astype(BF16)

    for s in range(2):
        x = z_ref[:, SEC_A + LANES * s:SEC_A + LANES * (s + 1)]
        x = _rope_slab(_head_norm_slab(x, qn_ref[...], ones), cs, sn, 16) * (q_scale * LOG2E)
        put_heads(qa_ref, 2 * s, x)
    x = z_ref[:, SEC_A + 256:SEC_A + 384]
    put_heads(ka_ref, 0, _rope_slab(_head_norm_slab(x, kn_ref[...], ones), cs, sn, 16))
    put_heads(va_ref, 0, z_ref[:, SEC_A + 384:SEC_A + 512])

    for s in range(2):
        x = z_ref[:, SEC_D + LANES * s:SEC_D + LANES * (s + 1)]
        put_heads(qd_ref, 2 * s, _rope_slab(x, cs, sn, 16) * q_scale)
    put_heads(kd_ref, 0, _rope_slab(z_ref[:, SEC_D + 256:SEC_D + 384], cs, sn, 16))
    put_heads(vd_ref, 0, z_ref[:, SEC_D + 384:SEC_D + 512])

    cq = _rms(z_ref[:, SEC_B:SEC_B + MLA_Q_RANK], NORM_EPS) * mqn_ref[...]
    q = _dot(cq.astype(BF16), wuq_ref[...])
    ckv = _rms(z_ref[:, SEC_B + 256:SEC_B + 384], NORM_EPS) * mkvn_ref[...]
    kv = _dot(ckv.astype(BF16), wukv_ref[...])
    kr = _rope_slab(z_ref[:, SEC_B + 384:SEC_B + 512], csm, snm, 8)
    b_scale = MLA_QK_DIM ** -0.5 * LOG2E
    for h in range(GROUP_HEADS):
        qh = q[:, LANES * h:LANES * (h + 1)]
        qb_ref[h] = (_rope_slab(qh, csm, snm, 8) * b_scale).astype(BF16)
        kb_ref[h] = (kv[:, LANES * h:LANES * (h + 1)] + kr).astype(BF16)
        vb_ref[h] = kv[:, 4 * LANES + hd * h:4 * LANES + hd * (h + 1)].astype(BF16)


def _attn_prep(zabd, tabs, p, n_tiles):
    n = zabd.shape[0]
    cs, sn, csm, snm = tabs

    def tab_spec():
        return pl.BlockSpec((TM, LANES), lambda i: (i % n_tiles, 0))

    def row_spec(w):
        return pl.BlockSpec((1, w), lambda i: (0, 0))

    def heads_out(nh, w):
        return (pl.BlockSpec((nh, TM, w), lambda i: (0, i, 0)),
                jax.ShapeDtypeStruct((nh, n, w), BF16))

    outs = [heads_out(4, 64), heads_out(2, 64), heads_out(2, 64),
            heads_out(4, 128), heads_out(4, 128), heads_out(4, 64),
            heads_out(4, 64), heads_out(2, 64), heads_out(2, 64)]
    return pl.pallas_call(
        _attn_prep_kernel,
        grid=(n // TM,),
        in_specs=[pl.BlockSpec((TM, ABD_COLS), lambda i: (i, 0)),
                  tab_spec(), tab_spec(), tab_spec(), tab_spec(),
                  row_spec(LANES), row_spec(LANES), row_spec(MLA_Q_RANK), row_spec(MLA_KV_RANK),
                  pl.BlockSpec((MLA_Q_RANK, 512), lambda i: (0, 0)),
                  pl.BlockSpec((MLA_KV_RANK, 768), lambda i: (0, 0)),
                  pl.BlockSpec((LANES, LANES), lambda i: (0, 0))],
        out_specs=[o[0] for o in outs],
        out_shape=[o[1] for o in outs],
        compiler_params=_cparams(("parallel",)),
        name="attn_prep",
    )(zabd, cs, sn, csm, snm, p['qn'], p['kn'], p['mqn'], p['mkvn'], p['wuq'], p['wukv'], p['ones64'])


def _flash_kernel(q_ref, k_ref, v_ref, o_ref, *, shared_kv, tq, tk, ctx_len, s_tot):
    qi = pl.program_id(2)
    dv = v_ref.shape[-1]
    if shared_kv:
        streams = [(q_ref[...].reshape(2 * tq, q_ref.shape[-1]), 0)]
    else:
        streams = [(q_ref[0], 0), (q_ref[1], 1)]

    def chunk_update(carry, q, kv, start):
        m, l, acc = carry
        k = k_ref[kv, pl.ds(start, tk), :]
        v = v_ref[kv, pl.ds(start, tk), :]
        s = _dot_nt(q, k)
        m_new = jnp.maximum(m, jnp.max(s, axis=-1, keepdims=True))
        alpha = jnp.exp2(m - m_new)
        pr = jnp.exp2(s - m_new)
        l = alpha * l + jnp.sum(pr, axis=-1, keepdims=True)
        acc = alpha * acc + _dot(pr.astype(BF16), v)
        return m_new, l, acc

    def run(n_chunks):
        unroll = next(u for u in (3, 2, 1) if n_chunks % u == 0)
        init = tuple((jnp.full((q.shape[0], 1), NEG_INF, F32), jnp.zeros((q.shape[0], 1), F32),
                      jnp.zeros((q.shape[0], dv), F32)) for q, _ in streams)

        def body(g, carries):
            for u in range(unroll):
                start = pl.multiple_of((g * unroll + u) * tk, tk)
                carries = tuple(chunk_update(c, q, kv, start) for c, (q, kv) in zip(carries, streams))
            return carries

        groups = n_chunks // unroll
        carries = body(0, init) if groups == 1 else lax.fori_loop(0, groups, body, init)
        outs = [acc / l for _, l, acc in carries]
        if shared_kv:
            outs = [outs[0][:tq], outs[0][tq:]]
        o_ref[...] = jnp.concatenate(outs, axis=-1).astype(o_ref.dtype)

    is_ctx = qi * tq < ctx_len

    @pl.when(is_ctx)
    def _():
        run(ctx_len // tk)

    @pl.when(jnp.logical_not(is_ctx))
    def _():
        run(s_tot // tk)


def _flash(q, k, v, *, shared_kv, batch, s_tot, ctx_len):
    nh, n, dk = q.shape
    dv = v.shape[-1]
    nkv = 1 if shared_kv else 2
    tq = tk = TM
    nq = s_tot // tq
    kern = functools.partial(_flash_kernel, shared_kv=shared_kv, tq=tq, tk=tk,
                             ctx_len=ctx_len, s_tot=s_tot)
    return pl.pallas_call(
        kern,
        grid=(batch, nh // 2, nq),
        in_specs=[pl.BlockSpec((2, tq, dk), lambda b, p, i: (p, b * nq + i, 0)),
                  pl.BlockSpec((nkv, s_tot, dk), lambda b, p, i: (p, b, 0)),
                  pl.BlockSpec((nkv, s_tot, dv), lambda b, p, i: (p, b, 0))],
        out_specs=pl.BlockSpec((tq, 2 * dv), lambda b, p, i: (b * nq + i, p)),
        out_shape=jax.ShapeDtypeStruct((n, nh * dv), BF16),
        compiler_params=_cparams(("parallel", "parallel", "arbitrary")),
        name="flash_shared" if shared_kv else "flash_split",
    )(q, k, v)


def _window_kernel(sink_ref, q_ref, k_ref, v_ref, o_ref, *, ctx_len, s_tot):
    w = WINDOW
    g = pl.program_id(1)
    jb = pl.program_id(2)
    first_lat = ctx_len // w
    last_blk = s_tot // w - 1
    is_lat = jb >= first_lat
    pb = jnp.clip(jb - 1, 0, last_blk)
    nb = jnp.clip(jb + 1, 0, last_blk)

    def rows(ref, blk):
        return ref[pl.ds(pl.multiple_of(blk * w, w), w), :]

    q = q_ref[...].reshape(2 * w, HEAD_DIM)
    kw = jnp.concatenate([rows(k_ref, pb), rows(k_ref, jb), rows(k_ref, nb)], axis=0)
    vw = jnp.concatenate([rows(v_ref, pb), rows(v_ref, jb), rows(v_ref, nb)], axis=0)
    s_w = _dot_nt(q, kw)
    s_c = _dot_nt(q, k_ref[0:ctx_len, :])
    r = lax.broadcasted_iota(jnp.int32, s_w.shape, 0) % w
    c = lax.broadcasted_iota(jnp.int32, s_w.shape, 1)
    lo = jnp.where(is_lat, jnp.where(jb - 1 >= first_lat, 0, w), 3 * w)
    hi = jnp.where(jb + 1 <= last_blk, 3 * w - 1, 2 * w - 1)
    valid = jnp.logical_and(c >= jnp.maximum(r, lo), c <= jnp.minimum(r + 2 * w, hi))
    s_w = jnp.where(valid, s_w, NEG_INF)
    row = lax.broadcasted_iota(jnp.int32, (2 * w, 1), 0)
    sk = jnp.where(row < w, sink_ref[2 * g], sink_ref[2 * g + 1])
    m = jnp.maximum(jnp.maximum(jnp.max(s_w, axis=-1, keepdims=True),
                                jnp.max(s_c, axis=-1, keepdims=True)), sk)
    p_w = jnp.exp(s_w - m)
    p_c = jnp.exp(s_c - m)
    denom = jnp.exp(sk - m) + jnp.sum(p_w, axis=-1, keepdims=True) + jnp.sum(p_c, axis=-1, keepdims=True)
    o = (_dot(p_w.astype(BF16), vw) + _dot(p_c.astype(BF16), v_ref[0:ctx_len, :])) / denom
    o_ref[...] = jnp.concatenate([o[:w], o[w:]], axis=-1).astype(o_ref.dtype)


def _window_attn(sink, q, k, v, *, batch, s_tot, ctx_len):
    nh, n, dk = q.shape
    nblk = s_tot // WINDOW
    kern = functools.partial(_window_kernel, ctx_len=ctx_len, s_tot=s_tot)
    return pl.pallas_call(
        kern,
        grid=(batch, nh // 2, nblk),
        in_specs=[pl.BlockSpec(memory_space=pltpu.SMEM),
                  pl.BlockSpec((2, WINDOW, dk), lambda b, g, j: (g, b * nblk + j, 0)),
                  pl.BlockSpec((None, s_tot, dk), lambda b, g, j: (g, b, 0)),
                  pl.BlockSpec((None, s_tot, dk), lambda b, g, j: (g, b, 0))],
        out_specs=pl.BlockSpec((WINDOW, 2 * dk), lambda b, g, j: (b * nblk + j, g)),
        out_shape=jax.ShapeDtypeStruct((n, nh * dk), BF16),
        compiler_params=_cparams(("parallel", "parallel", "arbitrary")),
        name="window_attn",
    )(sink, q, k, v)


def _rwkv_prep_kernel(z_ref, zp_ref, zn_ref, mu_ref, w0_ref, w2_ref, a0_ref, a2_ref, kk_ref, ka_ref,
                      rk_ref, g2_ref, ones_ref,
                      lw_ref, kd_ref, a_ref, b_ref, r_ref, v_ref, g_ref, bonus_ref, *, n_tiles):
    j = pl.program_id(0) % n_tiles
    z = z_ref[...]
    rows = lax.broadcasted_iota(jnp.int32, z.shape, 0)
    prev_row = jnp.where(j <= 1, 0.0, zp_ref[7:8, :])
    next_row = jnp.where(jnp.logical_or(j == 0, j == n_tiles - 1), 0.0, zn_ref[0:1, :])
    z_prev = jnp.where(rows == 0, prev_row, pltpu.roll(z, 1, axis=0))
    z_next = jnp.where(rows == TM - 1, next_row, pltpu.roll(z, TM - 1, axis=0))
    zs = z + mu_ref[...] * (0.5 * (z_prev + z_next) - z)

    gw = GROUP_WIDTH
    r, k, v = zs[:, 0:gw], zs[:, gw:2 * gw], zs[:, 2 * gw:3 * gw]
    lora = zs[:, 3 * gw:3 * gw + LANES]
    gl = zs[:, 3 * gw + LANES:3 * gw + LANES + gw]
    wt = jnp.tanh(lora)
    ones = ones_ref[...]

    def seg_sum(x):
        return jnp.concatenate([_mm_exact_rhs(x[:, :LANES], ones), _mm_exact_rhs(x[:, LANES:], ones)], axis=-1)

    r_ref[...] = r
    v_ref[...] = v
    g_ref[...] = _mm3(_sigmoid(gl), g2_ref[...])
    kd_sum = None
    for d in range(2):
        u = w0_ref[d] + _mm3(wt, w2_ref[d])
        soft = jnp.maximum(-u, 0.0) + jnp.log(1.0 + jnp.exp(-jnp.abs(u)))
        lw_ref[d] = -jnp.exp(-soft - 0.5)
        gate = _sigmoid(a0_ref[d] + _mm3(lora, a2_ref[d]))
        kk = k * kk_ref[d]
        kk = kk / jnp.maximum(jnp.sqrt(seg_sum(kk * kk)), 1e-12)
        kd = k * (1.0 + (gate - 1.0) * ka_ref[d])
        kd_ref[d] = kd
        a_ref[d] = -kk
        b_ref[d] = kk * gate
        kd_sum = kd if kd_sum is None else kd_sum + kd
    bonus_ref[...] = seg_sum(r * kd_sum * rk_ref[...]) * v


def _rwkv_prep(zc, p, n_tiles):
    n = zc.shape[0]
    gw = GROUP_WIDTH
    nb8 = n // 8

    def full(shape):
        nd = len(shape)
        return pl.BlockSpec(shape, lambda i: (0,) * nd)

    dir_out = (pl.BlockSpec((2, TM, gw), lambda i: (0, i, 0)), jax.ShapeDtypeStruct((2, n, gw), F32))
    one_out = (pl.BlockSpec((TM, gw), lambda i: (i, 0)), jax.ShapeDtypeStruct((n, gw), F32))
    outs = [dir_out] * 4 + [one_out] * 4
    return pl.pallas_call(
        functools.partial(_rwkv_prep_kernel, n_tiles=n_tiles),
        grid=(n // TM,),
        in_specs=[pl.BlockSpec((TM, C_COLS_PAD), lambda i: (i, 0)),
                  pl.BlockSpec((8, C_COLS_PAD), lambda i: (jnp.maximum(i * (TM // 8) - 1, 0), 0)),
                  pl.BlockSpec((8, C_COLS_PAD), lambda i: (jnp.minimum((i + 1) * (TM // 8), nb8 - 1), 0)),
                  full((1, C_COLS_PAD)),
                  full((2, 1, gw)), full((2, LANES, gw)), full((2, 1, gw)), full((2, LANES, gw)),
                  full((2, 1, gw)), full((2, 1, gw)), full((1, gw)), full((gw, gw)),
                  full((LANES, LANES))],
        out_specs=[o[0] for o in outs],
        out_shape=[o[1] for o in outs],
        compiler_params=_cparams(("parallel",)),
        name="rwkv_prep",
    )(zc, zc, zc, p['mu'], p['w0'], p['w2p'], p['a0'], p['a2p'], p['k_k'], p['k_a'], p['r_k'],
      p['g2p'], p['ones64'])


def _rwkv_scan_kernel(lw_ref, k_ref, a_ref, b_ref, r_ref, v_ref, y_ref, s_ref,
                      wa_s, u0_s, yr_s, y0_s, bt_s, kt_s, *, reverse):
    @pl.when(pl.program_id(1) == 0)
    def _():
        s_ref[...] = jnp.zeros_like(s_ref)

    n_chunks = TM // CHUNK
    hd = HEAD_DIM
    row = lax.broadcasted_iota(jnp.int32, (TM, TM), 0)
    col = lax.broadcasted_iota(jnp.int32, (TM, TM), 1)
    same = (row // CHUNK) == (col // CHUNK)
    before = (col > row) if reverse else (col < row)
    m_strict = jnp.logical_and(same, before)
    m_incl = jnp.logical_and(same, jnp.logical_or(before, row == col))
    eye = (row == col).astype(F32)

    lw = lw_ref[...]
    cum = _mm_exact_lhs(m_incl.astype(BF16), lw)
    e_pos = jnp.exp(cum)
    e_neg = jnp.exp(-cum)
    at_all = a_ref[...] * jnp.exp(cum - lw)
    rt_all = r_ref[...] * e_pos
    bt_all = b_ref[...] * e_neg
    kt_all = k_ref[...] * e_neg
    v_all = v_ref[...]
    bt_s[...] = bt_all
    kt_s[...] = kt_all

    for h in range(GROUP_HEADS):
        sl = slice(hd * h, hd * (h + 1))
        at, rt, bt, kt = at_all[:, sl], rt_all[:, sl], bt_all[:, sl], kt_all[:, sl]
        vb = v_all[:, sl].astype(BF16)
        at16, rt16, bt16, kt16 = at.astype(BF16), rt.astype(BF16), bt.astype(BF16), kt.astype(BF16)
        ab = jnp.where(m_strict, _dot_nt(at16, bt16), 0.0)
        ak = jnp.where(m_strict, _dot_nt(at16, kt16), 0.0)
        rb = jnp.where(m_incl, _dot_nt(rt16, bt16), 0.0)
        rk = jnp.where(m_incl, _dot_nt(rt16, kt16), 0.0)
        pw = ab
        tinv = eye + ab
        for _ in range(int(math.log2(CHUNK)) - 1):
            pw = _mm3(pw, pw)
            tinv = tinv + _mm3(tinv, pw)
        x = jnp.concatenate([at, _dot(ak.astype(BF16), vb)], axis=-1)
        wu = _mm_exact_rhs(tinv, x.astype(BF16))
        wa, u0 = wu[:, :hd], wu[:, hd:]
        rb16 = rb.astype(BF16)
        wa_s[h] = wa
        u0_s[h] = u0
        yr_s[h] = rt + _dot(rb16, wa.astype(BF16))
        y0_s[h] = _dot(rb16, u0.astype(BF16)) + _dot(rk.astype(BF16), vb)

    order = range(n_chunks - 1, -1, -1) if reverse else range(n_chunks)
    for c in order:
        rs = slice(CHUNK * c, CHUNK * (c + 1))
        last = CHUNK * c if reverse else CHUNK * (c + 1) - 1
        decay = e_pos[last:last + 1, :]
        y_heads = []
        for h in range(GROUP_HEADS):
            sl = slice(hd * h, hd * (h + 1))
            s0 = s_ref[h]
            y_heads.append(_mm3nt(yr_s[h, rs, :], s0) + y0_s[h, rs, :])
            u = _mm3nt(wa_s[h, rs, :], s0) + u0_s[h, rs, :]
            upd = _mm3tn(u, bt_s[rs, sl]) + _mm3tn(v_all[rs, sl], kt_s[rs, sl])
            s_ref[h] = (s0 + upd) * decay[:, sl]
        y_ref[rs, :] = jnp.concatenate(y_heads, axis=-1)


def _mm3nt(a, b):
    ah, al = _split2(a)
    bh, bl = _split2(b)
    return _dot_nt(ah, bh) + (_dot_nt(ah, bl) + _dot_nt(al, bh))


def _mm3tn(a, b):
    ah, al = _split2(a)
    bh, bl = _split2(b)
    return _dot_tn(ah, bh) + (_dot_tn(ah, bl) + _dot_tn(al, bh))


def _rwkv_scan(lw, kd, a, b, r, v, *, d, batch, n_tiles):
    n = r.shape[0]
    gw = GROUP_WIDTH
    reverse = d == 1

    def tile(b_, s):
        if reverse:
            t = jnp.where(s == 0, 0, n_tiles - s)
        else:
            t = s
        return b_ * n_tiles + t

    dspec = pl.BlockSpec((None, TM, gw), lambda b_, s: (d, tile(b_, s), 0))
    spec = pl.BlockSpec((TM, gw), lambda b_, s: (tile(b_, s), 0))
    hs = pltpu.VMEM((GROUP_HEADS, TM, HEAD_DIM), F32)
    return pl.pallas_call(
        functools.partial(_rwkv_scan_kernel, reverse=reverse),
        grid=(batch, n_tiles),
        in_specs=[dspec, dspec, dspec, dspec, spec, spec],
        out_specs=spec,
        out_shape=jax.ShapeDtypeStruct((n, gw), F32),
        scratch_shapes=[pltpu.VMEM((GROUP_HEADS, HEAD_DIM, HEAD_DIM), F32), hs, hs, hs, hs,
                        pltpu.VMEM((TM, gw), F32), pltpu.VMEM((TM, gw), F32)],
        compiler_params=_cparams(("parallel", "arbitrary")),
        name="rwkv_scan_rev" if reverse else "rwkv_scan_fwd",
    )(lw, kd, a, b, r, v)


def _rwkv_out_kernel(yf_ref, yr_ref, bonus_ref, g_ref, lnw_ref, lnb_ref, ones_ref, o_ref):
    ones = ones_ref[...]

    def seg_mean(x):
        s = jnp.concatenate([_mm_exact_rhs(x[:, :LANES], ones), _mm_exact_rhs(x[:, LANES:], ones)], axis=-1)
        return s * (1.0 / HEAD_DIM)

    y = yf_ref[...] + yr_ref[...]
    dlt = y - seg_mean(y)
    yn = dlt * lax.rsqrt(seg_mean(dlt * dlt) + RWKV_GN_EPS) * lnw_ref[...] + lnb_ref[...]
    o_ref[...] = ((yn + bonus_ref[...]) * g_ref[...]).astype(o_ref.dtype)


def _rwkv_out(yf, yr, bonus, g, p):
    n, gw = yf.shape
    spec = pl.BlockSpec((TM, gw), lambda i: (i, 0))
    row = pl.BlockSpec((1, gw), lambda i: (0, 0))
    return pl.pallas_call(
        _rwkv_out_kernel,
        grid=(n // TM,),
        in_specs=[spec, spec, spec, spec, row, row, pl.BlockSpec((LANES, LANES), lambda i: (0, 0))],
        out_specs=spec,
        out_shape=jax.ShapeDtypeStruct((n, gw), BF16),
        compiler_params=_cparams(("parallel",)),
        name="rwkv_out",
    )(yf, yr, bonus, g, p['ln_w'], p['ln_b'], p['ones64'])


def _out_mlp_kernel(x_ref, oa_ref, ob_ref, oc_ref, od_ref, mod_ref, gpost_ref, gpre_ref, gpm_ref,
                    wo_ref, w1_ref, w2_ref, xo_ref):
    d = D_MODEL
    gw = GROUP_WIDTH
    mod = mod_ref[...]
    y = _dot(oa_ref[...], wo_ref[0:gw, :])
    y += _dot(ob_ref[...], wo_ref[gw:2 * gw, :])
    y += _dot(oc_ref[...], wo_ref[2 * gw:3 * gw, :])
    y += _dot(od_ref[...], wo_ref[3 * gw:4 * gw, :])
    x1 = x_ref[...] + mod[:, 2 * d:3 * d] * (_rms(y, NORM_EPS) * gpost_ref[...])
    h = _rms(x1, NORM_EPS) * gpre_ref[...] * (1.0 + mod[:, 4 * d:5 * d]) + mod[:, 3 * d:4 * d]
    u = jnp.maximum(_dot(h.astype(BF16), w1_ref[...]), 0.0)
    zz = _dot((u * u).astype(BF16), w2_ref[...])
    xo_ref[...] = x1 + mod[:, 5 * d:6 * d] * (_rms(zz, NORM_EPS) * gpm_ref[...])


def _out_mlp(xs, oa, ob, oc, od, mod3, p, mod_row):
    n = xs.shape[0]
    d = D_MODEL
    gw = GROUP_WIDTH
    ospec = pl.BlockSpec((TM, gw), lambda i: (i, 0))
    row = pl.BlockSpec((1, d), lambda i: (0, 0))

    def wspec(shape):
        return pl.BlockSpec(shape, lambda i: (0, 0), pipeline_mode=pl.Buffered(1))

    return pl.pallas_call(
        _out_mlp_kernel,
        grid=(n // TM,),
        in_specs=[pl.BlockSpec((TM, d), lambda i: (i, 0)), ospec, ospec, ospec, ospec,
                  pl.BlockSpec((None, 1, 6 * d), lambda i: (mod_row(i), 0, 0)),
                  row, row, row,
                  wspec((d, d)), wspec((d, D_FF)), wspec((D_FF, d))],
        out_specs=pl.BlockSpec((TM, d), lambda i: (i, 0)),
        out_shape=jax.ShapeDtypeStruct((n, d), F32),
        compiler_params=_cparams(("parallel",)),
        name="out_mlp",
    )(xs, oa, ob, oc, od, mod3, p['g_post_mix'], p['g_pre_mlp'], p['g_post_mlp'],
      p['w_out'], p['w_mlp1'], p['w_mlp2'])


def _rope_tables(seq, ctx_len):
    t = jnp.arange(seq, dtype=jnp.int32)
    row = (t // GRID_W).astype(F32)
    col = (t % GRID_W).astype(F32)

    def tables(rot_dim, lane_dim):
        n = rot_dim // 4
        inv = ROPE_THETA ** (-jnp.arange(n, dtype=F32) / n)
        ar, ac = row[:, None] * inv, col[:, None] * inv
        cos = jnp.concatenate([jnp.cos(ar), jnp.cos(ar), jnp.cos(ac), jnp.cos(ac)], axis=-1)
        sin = jnp.concatenate([-jnp.sin(ar), jnp.sin(ar), -jnp.sin(ac), jnp.sin(ac)], axis=-1)
        return cos, sin

    def with_ctx(tab, fill):
        return jnp.concatenate([jnp.full((ctx_len, tab.shape[1]), fill, F32), tab], axis=0)

    cos, sin = tables(HEAD_DIM, LANES)
    cs = with_ctx(jnp.tile(cos, (1, LANES // HEAD_DIM)), 1.0)
    sn = with_ctx(jnp.tile(sin, (1, LANES // HEAD_DIM)), 0.0)
    cosm, sinm = tables(MLA_ROPE_DIM, LANES)
    pad_l, pad_r = MLA_NOPE_DIM, LANES - MLA_QK_DIM
    csm = with_ctx(jnp.pad(cosm, ((0, 0), (pad_l, pad_r)), constant_values=1.0), 1.0)
    snm = with_ctx(jnp.pad(sinm, ((0, 0), (pad_l, pad_r))), 0.0)
    return cs, sn, csm, snm


def _layer_params(l, w):
    d = D_MODEL
    gw = GROUP_WIDTH
    w_in = w['w_in'][l]
    a_end, b_end, c_end = 512, 512 + 416, 512 + 416 + 1056
    zeros = lambda c: jnp.zeros((d, c), F32)
    w_in_p = jnp.concatenate([
        w_in[:, :a_end],
        w_in[:, a_end:a_end + 384], zeros(MLA_NOPE_DIM), w_in[:, a_end + 384:b_end], zeros(LANES - MLA_QK_DIM),
        w_in[:, c_end:],
        w_in[:, b_end:c_end], zeros(C_COLS_PAD - 1056)], axis=1).astype(BF16)
    wuq = w['mla_w_uq'][l].reshape(MLA_Q_RANK, GROUP_HEADS, MLA_QK_DIM)
    wuq = jnp.pad(wuq, ((0, 0), (0, 0), (0, LANES - MLA_QK_DIM))).reshape(MLA_Q_RANK, GROUP_HEADS * LANES)
    wukv = w['mla_w_ukv'][l].reshape(MLA_KV_RANK, GROUP_HEADS, MLA_NOPE_DIM + HEAD_DIM)
    wk = jnp.pad(wukv[:, :, :MLA_NOPE_DIM], ((0, 0), (0, 0), (0, LANES - MLA_NOPE_DIM)))
    wukv_p = jnp.concatenate([wk.reshape(MLA_KV_RANK, GROUP_HEADS * LANES),
                              wukv[:, :, MLA_NOPE_DIM:].reshape(MLA_KV_RANK, gw)], axis=1)
    lane = jnp.arange(LANES)
    ones64 = ((lane[:, None] // HEAD_DIM) == (lane[None, :] // HEAD_DIM)).astype(BF16)
    zl = jnp.zeros((2, LANES - RWKV_DECAY_LORA, gw), F32)
    return {
        'w_in_p': w_in_p,
        'g_pre_mix': w['g_pre_mix'][l].reshape(1, d),
        'g_post_mix': w['g_post_mix'][l].reshape(1, d),
        'g_pre_mlp': w['g_pre_mlp'][l].reshape(1, d),
        'g_post_mlp': w['g_post_mlp'][l].reshape(1, d),
        'qn': jnp.tile(w['gqa_q_norm'][l], 2).reshape(1, LANES),
        'kn': jnp.tile(w['gqa_k_norm'][l], 2).reshape(1, LANES),
        'mqn': w['mla_q_norm'][l].reshape(1, MLA_Q_RANK),
        'mkvn': w['mla_kv_norm'][l].reshape(1, MLA_KV_RANK),
        'wuq': wuq.astype(BF16),
        'wukv': wukv_p.astype(BF16),
        'ones64': ones64,
        'mu': jnp.pad(w['rwkv_mu'][l], (0, C_COLS_PAD - 1056)).reshape(1, C_COLS_PAD),
        'w0': w['rwkv_w0'][l].reshape(2, 1, gw),
        'w2p': jnp.concatenate([w['rwkv_w2'][l], zl], axis=1),
        'a0': w['rwkv_a0'][l].reshape(2, 1, gw),
        'a2p': jnp.concatenate([zl, w['rwkv_a2'][l]], axis=1),
        'k_k': w['rwkv_k_k'][l].reshape(2, 1, gw),
        'k_a': w['rwkv_k_a'][l].reshape(2, 1, gw),
        'r_k': w['rwkv_r_k'][l].reshape(1, gw),
        'g2p': jnp.pad(w['rwkv_g2'][l], ((0, gw - RWKV_GATE_LORA), (0, 0))),
        'ln_w': w['rwkv_ln_w'][l].reshape(1, gw),
        'ln_b': w['rwkv_ln_b'][l].reshape(1, gw),
        'sink': w['swa_sink'][l],
        'w_out': w['w_out'][l].astype(BF16),
        'w_mlp1': w['w_mlp1'][l].astype(BF16),
        'w_mlp2': w['w_mlp2'][l].astype(BF16),
    }


def kernel(x, c, ctx, c_ctx, w_mod, b_mod, g_pre_mix, g_post_mix, g_pre_mlp, g_post_mlp, w_in, gqa_q_norm, gqa_k_norm, mla_q_norm, mla_kv_norm, mla_w_uq, mla_w_ukv, rwkv_mu, rwkv_w0, rwkv_w2, rwkv_a0, rwkv_a2, rwkv_k_k, rwkv_k_a, rwkv_r_k, rwkv_g2, rwkv_ln_w, rwkv_ln_b, swa_sink, w_out, w_mlp1, w_mlp2):
    batch, seq, d = x.shape
    ctx_len = ctx.shape[1]
    depth = w_mod.shape[0]
    assert d == D_MODEL and ctx_len == TM and seq % TM == 0 and seq % GRID_W == 0
    assert batch + 1 <= 8
    s_tot = ctx_len + seq
    n_tiles = s_tot // TM
    w = dict(w_in=w_in, g_pre_mix=g_pre_mix, g_post_mix=g_post_mix, g_pre_mlp=g_pre_mlp,
             g_post_mlp=g_post_mlp, gqa_q_norm=gqa_q_norm, gqa_k_norm=gqa_k_norm, mla_q_norm=mla_q_norm,
             mla_kv_norm=mla_kv_norm, mla_w_uq=mla_w_uq, mla_w_ukv=mla_w_ukv, rwkv_mu=rwkv_mu,
             rwkv_w0=rwkv_w0, rwkv_w2=rwkv_w2, rwkv_a0=rwkv_a0, rwkv_a2=rwkv_a2, rwkv_k_k=rwkv_k_k,
             rwkv_k_a=rwkv_k_a, rwkv_r_k=rwkv_r_k, rwkv_g2=rwkv_g2, rwkv_ln_w=rwkv_ln_w,
             rwkv_ln_b=rwkv_ln_b, swa_sink=swa_sink, w_out=w_out, w_mlp1=w_mlp1, w_mlp2=w_mlp2)

    ct = jnp.concatenate([c, c_ctx[None, :], jnp.zeros((8 - batch - 1, d), F32)], axis=0).T
    mods = _modulation(ct, w_mod, b_mod, batch + 1)

    def mod_row(i):
        return jnp.where(i % n_tiles == 0, batch, i // n_tiles)

    tabs = _rope_tables(seq, ctx_len)
    xs = jnp.concatenate([ctx, x], axis=1).reshape(batch * s_tot, d)
    geo = dict(batch=batch, s_tot=s_tot, ctx_len=ctx_len)
    for l in range(depth):
        p = _layer_params(l, w)
        mod3 = mods[l].reshape(8, 1, 6 * d)
        zabd, zc = _inproj(xs, mod3, p['g_pre_mix'], p['w_in_p'], mod_row)
        qa, ka, va, qb, kb, vb, qd, kd, vd = _attn_prep(zabd, tabs, p, n_tiles)
        oa = _flash(qa, ka, va, shared_kv=True, **geo)
        ob = _flash(qb, kb, vb, shared_kv=False, **geo)
        od = _window_attn(p['sink'], qd, kd, vd, **geo)
        lw, rkd, ra, rb, rr, rv, rg, bonus = _rwkv_prep(zc, p, n_tiles)
        yf = _rwkv_scan(lw, rkd, ra, rb, rr, rv, d=0, batch=batch, n_tiles=n_tiles)
        yr = _rwkv_scan(lw, rkd, ra, rb, rr, rv, d=1, batch=batch, n_tiles=n_tiles)
        oc = _rwkv_out(yf, yr, bonus, rg, p)
        xs = _out_mlp(xs, oa, ob, oc, od, mod3, p, mod_row)
    return xs.reshape(batch, s_tot, d)[:, ctx_len:, :]
```

```python
import functools
import math

import jax
import jax.numpy as jnp
from jax import lax
from jax.experimental import pallas as pl
from jax.experimental.pallas import tpu as pltpu

F32 = jnp.float32
BF16 = jnp.bfloat16

D_MODEL = 1024
GRID_W = 64
HEAD_DIM = 64
GROUP_WIDTH = 256
GROUP_HEADS = 4
ROPE_THETA = 10000.0
NORM_EPS = 1e-6
NEG_INF = -1e30
MLA_Q_RANK = 256
MLA_KV_RANK = 128
MLA_NOPE_DIM = 64
MLA_ROPE_DIM = 32
MLA_QK_DIM = MLA_NOPE_DIM + MLA_ROPE_DIM
RWKV_DECAY_LORA = 64
RWKV_ICLR_LORA = 64
RWKV_GATE_LORA = 160
RWKV_GN_EPS = 64e-5
WINDOW = 128
D_FF = 4 * D_MODEL

LOG2E = math.log2(math.e)
LANES = 128
TM = 256
ONES_ROWS = 8
LOOKAHEAD = 3
CHUNK = 64
SUB = 16
SEC_A = 0
SEC_B = 512
SEC_D = 1024
ABD_COLS = 1536
C_COLS_PAD = 1152
VMEM_LIMIT = 56 * 1024 * 1024


def _cparams(sem):
    return pltpu.CompilerParams(dimension_semantics=sem, vmem_limit_bytes=VMEM_LIMIT)


def _dot(a, b):
    return jnp.dot(a, b, preferred_element_type=F32)


def _dot_nt(a, b):
    return lax.dot_general(a, b, (((1,), (1,)), ((), ())), preferred_element_type=F32)


def _dot_tn(a, b):
    return lax.dot_general(a, b, (((0,), (0,)), ((), ())), preferred_element_type=F32)


def _dot16(a, b):
    return _dot(a.astype(BF16), b.astype(BF16))


def _split2(x):
    hi = x.astype(BF16)
    lo = (x - hi.astype(F32)).astype(BF16)
    return hi, lo


def _mm_exact_rhs(a, b_bf16):
    hi, lo = _split2(a)
    return _dot(hi, b_bf16) + _dot(lo, b_bf16)


def _mm_exact_lhs(a_bf16, b):
    hi, lo = _split2(b)
    return _dot(a_bf16, hi) + _dot(a_bf16, lo)


def _mm3(a, b):
    ah, al = _split2(a)
    bh, bl = _split2(b)
    return _dot(ah, bh) + (_dot(ah, bl) + _dot(al, bh))


def _sigmoid(x):
    return 1.0 / (1.0 + jnp.exp(-x))


def _rms(x, eps):
    return x * lax.rsqrt(jnp.mean(x * x, axis=-1, keepdims=True) + eps)


def _mod_kernel(ct_ref, w_ref, b_ref, o_ref, *, n_rows):
    ct = ct_ref[...]
    st = ct * _sigmoid(ct)
    w = w_ref[...]
    rows = [jnp.sum(st[:, r:r + 1] * w, axis=0, keepdims=True) for r in range(n_rows)]
    rows.append(jnp.zeros((8 - n_rows, w.shape[1]), F32))
    o_ref[...] = jnp.concatenate(rows, axis=0) + b_ref[...]


def _modulation(ct, w_mod, b_mod, n_rows):
    depth, d, n6 = w_mod.shape
    tn = 1536
    return pl.pallas_call(
        functools.partial(_mod_kernel, n_rows=n_rows),
        grid=(depth, n6 // tn),
        in_specs=[pl.BlockSpec((d, 8), lambda l, j: (0, 0)),
                  pl.BlockSpec((None, d, tn), lambda l, j: (l, 0, j)),
                  pl.BlockSpec((None, 1, tn), lambda l, j: (l, 0, j))],
        out_specs=pl.BlockSpec((None, 8, tn), lambda l, j: (l, 0, j)),
        out_shape=jax.ShapeDtypeStruct((depth, 8, n6), F32),
        compiler_params=_cparams(("parallel", "parallel")),
        name="modulation",
    )(ct, w_mod, b_mod.reshape(depth, 1, n6))


def _inproj_kernel(x_ref, mod_ref, g_ref, w_ref, zabd_ref, zc_ref):
    d = D_MODEL
    mod = mod_ref[...]
    h = _rms(x_ref[...], NORM_EPS) * g_ref[...] * (1.0 + mod[:, d:2 * d]) + mod[:, 0:d]
    z = _dot(h.astype(BF16), w_ref[...])
    zabd_ref[...] = z[:, :ABD_COLS]
    zc_ref[...] = z[:, ABD_COLS:]


def _inproj(xs, mod3, g, w_in_p, mod_row):
    n = xs.shape[0]
    cols = w_in_p.shape[1]
    return pl.pallas_call(
        _inproj_kernel,
        grid=(n // TM,),
        in_specs=[pl.BlockSpec((TM, D_MODEL), lambda i: (i, 0)),
                  pl.BlockSpec((None, 1, 6 * D_MODEL), lambda i: (mod_row(i), 0, 0)),
                  pl.BlockSpec((1, D_MODEL), lambda i: (0, 0)),
                  pl.BlockSpec((D_MODEL, cols), lambda i: (0, 0))],
        out_specs=[pl.BlockSpec((TM, ABD_COLS), lambda i: (i, 0)),
                   pl.BlockSpec((TM, C_COLS_PAD), lambda i: (i, 0))],
        out_shape=[jax.ShapeDtypeStruct((n, ABD_COLS), F32),
                   jax.ShapeDtypeStruct((n, C_COLS_PAD), F32)],
        compiler_params=_cparams(("parallel",)),
        name="inproj",
    )(xs, mod3, g, w_in_p)


def _rope_slab(x, cos, sin_signed, shift):
    left = pltpu.roll(x, LANES - shift, axis=1)
    right = pltpu.roll(x, shift, axis=1)
    lane = lax.broadcasted_iota(jnp.int32, x.shape, 1)
    first = ((lane // shift) % 2) == 0
    return x * cos + jnp.where(first, left, right) * sin_signed


def _head_norm_slab(x, gain, ones):
    ss = _mm_exact_rhs(x * x, ones)
    return x * lax.rsqrt(ss * (1.0 / HEAD_DIM) + NORM_EPS) * gain


def _attn_prep_kernel(z_ref, cs_ref, sn_ref, csm_ref, snm_ref, qn_ref, kn_ref, mqn_ref, mkvn_ref,
                      wuq_ref, wukv_ref, ones_ref,
                      qa_ref, ka_ref, va_ref, qb_ref, kb_ref, vb_ref, qd_ref, kd_ref, vd_ref):
    cs, sn = cs_ref[...], sn_ref[...]
    csm, snm = csm_ref[...], snm_ref[...]
    ones = ones_ref[...]
    hd = HEAD_DIM
    q_scale = HEAD_DIM ** -0.5

    def put_heads(ref, base, slab):
        ref[base] = slab[:, :hd].astype(BF16)
        ref[base + 1] = slab[:, hd:].astype(BF16)

    def put_heads_t(ref, base, slab):
        st = slab.T.astype(BF16)
        one = jnp.ones((ONES_ROWS, st.shape[1]), BF16)
        ref[base] = jnp.concatenate([st[:hd], one], axis=0)
        ref[base + 1] = jnp.concatenate([st[hd:], one], axis=0)

    for s in range(2):
        x = z_ref[:, SEC_A + LANES * s:SEC_A + LANES * (s + 1)]
        x = _rope_slab(_head_norm_slab(x, qn_ref[...], ones), cs, sn, 16) * (q_scale * LOG2E)
        put_heads(qa_ref, 2 * s, x)
    x = z_ref[:, SEC_A + 256:SEC_A + 384]
    put_heads(ka_ref, 0, _rope_slab(_head_norm_slab(x, kn_ref[...], ones), cs, sn, 16))
    put_heads_t(va_ref, 0, z_ref[:, SEC_A + 384:SEC_A + 512])

    for s in range(2):
        x = z_ref[:, SEC_D + LANES * s:SEC_D + LANES * (s + 1)]
        put_heads(qd_ref, 2 * s, _rope_slab(x, cs, sn, 16) * q_scale)
    put_heads(kd_ref, 0, _rope_slab(z_ref[:, SEC_D + 256:SEC_D + 384], cs, sn, 16))
    put_heads(vd_ref, 0, z_ref[:, SEC_D + 384:SEC_D + 512])

    cq = _rms(z_ref[:, SEC_B:SEC_B + MLA_Q_RANK], NORM_EPS) * mqn_ref[...]
    q = _dot(cq.astype(BF16), wuq_ref[...])
    ckv = _rms(z_ref[:, SEC_B + 256:SEC_B + 384], NORM_EPS) * mkvn_ref[...]
    kv = _dot(ckv.astype(BF16), wukv_ref[...])
    kr = _rope_slab(z_ref[:, SEC_B + 384:SEC_B + 512], csm, snm, 8)
    b_scale = MLA_QK_DIM ** -0.5 * LOG2E
    for h in range(GROUP_HEADS):
        qh = q[:, LANES * h:LANES * (h + 1)]
        qb_ref[h] = (_rope_slab(qh, csm, snm, 8) * b_scale).astype(BF16)
        kb_ref[h] = (kv[:, LANES * h:LANES * (h + 1)] + kr).astype(BF16)
    for s in range(2):
        put_heads_t(vb_ref, 2 * s, kv[:, 4 * LANES + LANES * s:4 * LANES + LANES * (s + 1)])


def _attn_prep(zabd, tabs, p, n_tiles):
    n = zabd.shape[0]
    cs, sn, csm, snm = tabs

    def tab_spec():
        return pl.BlockSpec((TM, LANES), lambda i: (i % n_tiles, 0))

    def row_spec(w):
        return pl.BlockSpec((1, w), lambda i: (0, 0))

    def heads_out(nh, w):
        return (pl.BlockSpec((nh, TM, w), lambda i: (0, i, 0)),
                jax.ShapeDtypeStruct((nh, n, w), BF16))

    def heads_out_t(nh):
        return (pl.BlockSpec((nh, None, HEAD_DIM + ONES_ROWS, TM), lambda i: (0, i, 0, 0)),
                jax.ShapeDtypeStruct((nh, n // TM, HEAD_DIM + ONES_ROWS, TM), BF16))

    outs = [heads_out(4, 64), heads_out(2, 64), heads_out_t(2),
            heads_out(4, 128), heads_out(4, 128), heads_out_t(4),
            heads_out(4, 64), heads_out(2, 64), heads_out(2, 64)]
    return pl.pallas_call(
        _attn_prep_kernel,
        grid=(n // TM,),
        in_specs=[pl.BlockSpec((TM, ABD_COLS), lambda i: (i, 0)),
                  tab_spec(), tab_spec(), tab_spec(), tab_spec(),
                  row_spec(LANES), row_spec(LANES), row_spec(MLA_Q_RANK), row_spec(MLA_KV_RANK),
                  pl.BlockSpec((MLA_Q_RANK, 512), lambda i: (0, 0)),
                  pl.BlockSpec((MLA_KV_RANK, 768), lambda i: (0, 0)),
                  pl.BlockSpec((LANES, LANES), lambda i: (0, 0))],
        out_specs=[o[0] for o in outs],
        out_shape=[o[1] for o in outs],
        compiler_params=_cparams(("parallel",)),
        name="attn_prep",
    )(zabd, cs, sn, csm, snm, p['qn'], p['kn'], p['mqn'], p['mkvn'], p['wuq'], p['wukv'], p['ones64'])


def _flash_kernel(q_ref, k_ref, vt_ref, o_ref, *, shared_kv, tq, tk, ctx_len, s_tot):
    qi = pl.program_id(2)
    dv = vt_ref.shape[-2] - ONES_ROWS
    if shared_kv:
        streams = [(q_ref[...].reshape(2 * tq, q_ref.shape[-1]), 0)]
    else:
        streams = [(q_ref[0], 0), (q_ref[1], 1)]

    def scores(c):
        out = []
        for q, kv in streams:
            k = k_ref[kv, pl.ds(pl.multiple_of(c * tk, tk), tk), :]
            out.append(_dot_nt(k, q))
        return out

    def softmax_pv(c, sts, carries):
        new = []
        for st, (_, kv), (m, acc) in zip(sts, streams, carries):
            m_new = jnp.maximum(m, jnp.max(st, axis=0, keepdims=True))
            pt = jnp.exp2(st - m_new).astype(BF16)
            acc = jnp.exp2(m - m_new) * acc + _dot(vt_ref[kv, c], pt)
            new.append((m_new, acc))
        return tuple(new)

    def run(n_chunks):
        unroll = next(u for u in (11, 3, 2, 1) if n_chunks % u == 0)
        groups = n_chunks // unroll
        init = tuple((jnp.full((1, q.shape[0]), NEG_INF, F32), jnp.zeros((dv + ONES_ROWS, q.shape[0]), F32))
                     for q, _ in streams)

        def body(g, carries):
            queue = [scores(g * unroll + u) for u in range(min(LOOKAHEAD, unroll))]
            for u in range(unroll):
                if u + LOOKAHEAD < unroll:
                    queue.append(scores(g * unroll + u + LOOKAHEAD))
                carries = softmax_pv(g * unroll + u, queue.pop(0), carries)
            return carries

        carries = body(0, init) if groups == 1 else lax.fori_loop(0, groups, body, init)
        outs = [acc[:dv] / acc[dv:dv + 1] for _, acc in carries]
        if shared_kv:
            outs = [outs[0][:, :tq], outs[0][:, tq:]]
        o_ref[...] = jnp.concatenate(outs, axis=0).T.astype(o_ref.dtype)

    is_ctx = qi * tq < ctx_len

    @pl.when(is_ctx)
    def _():
        run(ctx_len // tk)

    @pl.when(jnp.logical_not(is_ctx))
    def _():
        run(s_tot // tk)


def _flash(q, k, vt, *, shared_kv, batch, s_tot, ctx_len):
    nh, n, dk = q.shape
    dv = vt.shape[-2] - ONES_ROWS
    nkv = 1 if shared_kv else 2
    tq = tk = TM
    nq = s_tot // tq
    kern = functools.partial(_flash_kernel, shared_kv=shared_kv, tq=tq, tk=tk,
                             ctx_len=ctx_len, s_tot=s_tot)
    return pl.pallas_call(
        kern,
        grid=(batch, nh // 2, nq),
        in_specs=[pl.BlockSpec((2, tq, dk), lambda b, p, i: (p, b * nq + i, 0)),
                  pl.BlockSpec((nkv, s_tot, dk), lambda b, p, i: (p, b, 0)),
                  pl.BlockSpec((nkv, s_tot // tk, dv + ONES_ROWS, tk), lambda b, p, i: (p, b, 0, 0))],
        out_specs=pl.BlockSpec((tq, 2 * dv), lambda b, p, i: (b * nq + i, p)),
        out_shape=jax.ShapeDtypeStruct((n, nh * dv), BF16),
        compiler_params=_cparams(("parallel", "parallel", "arbitrary")),
        name="flash_shared" if shared_kv else "flash_split",
    )(q, k, vt)


def _window_kernel(sink_ref, q_ref, k_ref, v_ref, o_ref, *, ctx_len, s_tot):
    w = WINDOW
    g = pl.program_id(1)
    jb = pl.program_id(2)
    first_lat = ctx_len // w
    last_blk = s_tot // w - 1
    is_lat = jb >= first_lat
    pb = jnp.clip(jb - 1, 0, last_blk)
    nb = jnp.clip(jb + 1, 0, last_blk)

    def rows(ref, blk):
        return ref[pl.ds(pl.multiple_of(blk * w, w), w), :]

    q = q_ref[...].reshape(2 * w, HEAD_DIM)
    kw = jnp.concatenate([rows(k_ref, pb), rows(k_ref, jb), rows(k_ref, nb)], axis=0)
    vw = jnp.concatenate([rows(v_ref, pb), rows(v_ref, jb), rows(v_ref, nb)], axis=0)
    s_w = _dot_nt(q, kw)
    s_c = _dot_nt(q, k_ref[0:ctx_len, :])
    r = lax.broadcasted_iota(jnp.int32, s_w.shape, 0) % w
    c = lax.broadcasted_iota(jnp.int32, s_w.shape, 1)
    lo = jnp.where(is_lat, jnp.where(jb - 1 >= first_lat, 0, w), 3 * w)
    hi = jnp.where(jb + 1 <= last_blk, 3 * w - 1, 2 * w - 1)
    valid = jnp.logical_and(c >= jnp.maximum(r, lo), c <= jnp.minimum(r + 2 * w, hi))
    s_w = jnp.where(valid, s_w, NEG_INF)
    row = lax.broadcasted_iota(jnp.int32, (2 * w, 1), 0)
    sk = jnp.where(row < w, sink_ref[2 * g], sink_ref[2 * g + 1])
    m = jnp.maximum(jnp.maximum(jnp.max(s_w, axis=-1, keepdims=True),
                                jnp.max(s_c, axis=-1, keepdims=True)), sk)
    p_w = jnp.exp(s_w - m)
    p_c = jnp.exp(s_c - m)
    denom = jnp.exp(sk - m) + jnp.sum(p_w, axis=-1, keepdims=True) + jnp.sum(p_c, axis=-1, keepdims=True)
    o = (_dot(p_w.astype(BF16), vw) + _dot(p_c.astype(BF16), v_ref[0:ctx_len, :])) / denom
    o_ref[...] = jnp.concatenate([o[:w], o[w:]], axis=-1).astype(o_ref.dtype)


def _window_attn(sink, q, k, v, *, batch, s_tot, ctx_len):
    nh, n, dk = q.shape
    nblk = s_tot // WINDOW
    kern = functools.partial(_window_kernel, ctx_len=ctx_len, s_tot=s_tot)
    return pl.pallas_call(
        kern,
        grid=(batch, nh // 2, nblk),
        in_specs=[pl.BlockSpec(memory_space=pltpu.SMEM),
                  pl.BlockSpec((2, WINDOW, dk), lambda b, g, j: (g, b * nblk + j, 0)),
                  pl.BlockSpec((None, s_tot, dk), lambda b, g, j: (g, b, 0)),
                  pl.BlockSpec((None, s_tot, dk), lambda b, g, j: (g, b, 0))],
        out_specs=pl.BlockSpec((WINDOW, 2 * dk), lambda b, g, j: (b * nblk + j, g)),
        out_shape=jax.ShapeDtypeStruct((n, nh * dk), BF16),
        compiler_params=_cparams(("parallel", "parallel", "arbitrary")),
        name="window_attn",
    )(sink, q, k, v)


def _rwkv_prep_kernel(z_ref, zp_ref, zn_ref, mu_ref, w0_ref, w2_ref, a0_ref, a2_ref, kk_ref, ka_ref,
                      rk_ref, g2_ref, ones_ref,
                      lw_ref, kd_ref, a_ref, b_ref, r_ref, v_ref, g_ref, bonus_ref, *, n_tiles):
    j = pl.program_id(0) % n_tiles
    z = z_ref[...]
    rows = lax.broadcasted_iota(jnp.int32, z.shape, 0)
    prev_row = jnp.where(j <= 1, 0.0, zp_ref[7:8, :])
    next_row = jnp.where(jnp.logical_or(j == 0, j == n_tiles - 1), 0.0, zn_ref[0:1, :])
    z_prev = jnp.where(rows == 0, prev_row, pltpu.roll(z, 1, axis=0))
    z_next = jnp.where(rows == TM - 1, next_row, pltpu.roll(z, TM - 1, axis=0))
    zs = z + mu_ref[...] * (0.5 * (z_prev + z_next) - z)

    gw = GROUP_WIDTH
    r, k, v = zs[:, 0:gw], zs[:, gw:2 * gw], zs[:, 2 * gw:3 * gw]
    lora = zs[:, 3 * gw:3 * gw + LANES]
    gl = zs[:, 3 * gw + LANES:3 * gw + LANES + gw]
    wt = jnp.tanh(lora)
    ones = ones_ref[...]

    def seg_sum(x):
        return jnp.concatenate([_mm_exact_rhs(x[:, :LANES], ones), _mm_exact_rhs(x[:, LANES:], ones)], axis=-1)

    r_ref[...] = r
    v_ref[...] = v
    g_ref[...] = _mm3(_sigmoid(gl), g2_ref[...])
    kd_sum = None
    for d in range(2):
        u = w0_ref[d] + _mm3(wt, w2_ref[d])
        soft = jnp.maximum(-u, 0.0) + jnp.log(1.0 + jnp.exp(-jnp.abs(u)))
        lw_ref[d] = -jnp.exp(-soft - 0.5)
        gate = _sigmoid(a0_ref[d] + _mm3(lora, a2_ref[d]))
        kk = k * kk_ref[d]
        kk = kk / jnp.maximum(jnp.sqrt(seg_sum(kk * kk)), 1e-12)
        kd = k * (1.0 + (gate - 1.0) * ka_ref[d])
        kd_ref[d] = kd
        a_ref[d] = -kk
        b_ref[d] = kk * gate
        kd_sum = kd if kd_sum is None else kd_sum + kd
    bonus_ref[...] = seg_sum(r * kd_sum * rk_ref[...]) * v


def _rwkv_prep(zc, p, n_tiles):
    n = zc.shape[0]
    gw = GROUP_WIDTH
    nb8 = n // 8

    def full(shape):
        nd = len(shape)
        return pl.BlockSpec(shape, lambda i: (0,) * nd)

    dir_out = (pl.BlockSpec((2, TM, gw), lambda i: (0, i, 0)), jax.ShapeDtypeStruct((2, n, gw), F32))
    one_out = (pl.BlockSpec((TM, gw), lambda i: (i, 0)), jax.ShapeDtypeStruct((n, gw), F32))
    outs = [dir_out] * 4 + [one_out] * 4
    return pl.pallas_call(
        functools.partial(_rwkv_prep_kernel, n_tiles=n_tiles),
        grid=(n // TM,),
        in_specs=[pl.BlockSpec((TM, C_COLS_PAD), lambda i: (i, 0)),
                  pl.BlockSpec((8, C_COLS_PAD), lambda i: (jnp.maximum(i * (TM // 8) - 1, 0), 0)),
                  pl.BlockSpec((8, C_COLS_PAD), lambda i: (jnp.minimum((i + 1) * (TM // 8), nb8 - 1), 0)),
                  full((1, C_COLS_PAD)),
                  full((2, 1, gw)), full((2, LANES, gw)), full((2, 1, gw)), full((2, LANES, gw)),
                  full((2, 1, gw)), full((2, 1, gw)), full((1, gw)), full((gw, gw)),
                  full((LANES, LANES))],
        out_specs=[o[0] for o in outs],
        out_shape=[o[1] for o in outs],
        compiler_params=_cparams(("parallel",)),
        name="rwkv_prep",
    )(zc, zc, zc, p['mu'], p['w0'], p['w2p'], p['a0'], p['a2p'], p['k_k'], p['k_a'], p['r_k'],
      p['g2p'], p['ones64'])


def _rwkv_scan_kernel(lw_ref, k_ref, a_ref, b_ref, r_ref, v_ref, y_ref, s_ref, *, reverse):
    @pl.when(pl.program_id(1) == 0)
    def _():
        s_ref[...] = jnp.zeros_like(s_ref)

    n_chunks = TM // CHUNK
    hd = HEAD_DIM
    row = lax.broadcasted_iota(jnp.int32, (TM, TM), 0)
    col = lax.broadcasted_iota(jnp.int32, (TM, TM), 1)
    same = (row // CHUNK) == (col // CHUNK)
    before = (col > row) if reverse else (col < row)
    m_strict = jnp.logical_and(same, before)
    m_incl = jnp.logical_and(same, jnp.logical_or(before, row == col))
    eye = (row == col).astype(F32)
    assert CHUNK // SUB == 4
    same_sub = (row // SUB) == (col // SUB)

    lw = lw_ref[...]
    cum = _mm_exact_lhs(m_incl.astype(BF16), lw)
    e_pos = jnp.exp(cum)
    e_neg = jnp.exp(-cum)
    at_all = a_ref[...] * jnp.exp(cum - lw)
    rt_all = r_ref[...] * e_pos
    bt_all = b_ref[...] * e_neg
    kt_all = k_ref[...] * e_neg
    v_all = v_ref[...]

    heads = range(GROUP_HEADS)
    sls = [slice(hd * h, hd * (h + 1)) for h in heads]
    at = [at_all[:, sl] for sl in sls]
    rt = [rt_all[:, sl] for sl in sls]
    at16 = [x.astype(BF16) for x in at]
    rt16 = [x.astype(BF16) for x in rt]
    bt16 = [bt_all[:, sl].astype(BF16) for sl in sls]
    kt16 = [kt_all[:, sl].astype(BF16) for sl in sls]
    v16 = [v_all[:, sl].astype(BF16) for sl in sls]
    ab = [jnp.where(m_strict, _dot_nt(at16[h], bt16[h]), 0.0) for h in heads]
    ak16 = [jnp.where(m_strict, _dot_nt(at16[h], kt16[h]), 0.0).astype(BF16) for h in heads]
    rb16 = [jnp.where(m_incl, _dot_nt(rt16[h], bt16[h]), 0.0).astype(BF16) for h in heads]
    rk16 = [jnp.where(m_incl, _dot_nt(rt16[h], kt16[h]), 0.0).astype(BF16) for h in heads]
    pw = [jnp.where(same_sub, x, 0.0) for x in ab]
    lo16 = [jnp.where(same_sub, 0.0, x).astype(BF16) for x in ab]
    td = [eye + x for x in pw]
    for _ in range(int(math.log2(SUB)) - 1):
        pw16 = [x.astype(BF16) for x in pw]
        pw = [_dot(x, x) for x in pw16]
        td = [t + _dot16(t, x) for t, x in zip(td, pw)]
    td16 = [x.astype(BF16) for x in td]
    m1 = [_dot(t, x) for t, x in zip(td16, lo16)]
    m1_16 = [x.astype(BF16) for x in m1]
    m2 = [_dot(x, x) for x in m1_16]
    m3 = [_dot16(x, y) for x, y in zip(m1_16, m2)]
    nn16 = [(eye + a1 + a2 + a3).astype(BF16) for a1, a2, a3 in zip(m1, m2, m3)]
    akv = [_dot(ak16[h], v16[h]) for h in heads]
    tx = [_dot(td16[h], jnp.concatenate([at[h], akv[h]], axis=-1).astype(BF16)) for h in heads]
    wu = [_dot16(nn16[h], tx[h]) for h in heads]
    wa = [x[:, :hd] for x in wu]
    u0 = [x[:, hd:] for x in wu]
    wa16 = [x.astype(BF16) for x in wa]
    u016 = [x.astype(BF16) for x in u0]
    yr16 = [(rt[h] + _dot(rb16[h], wa16[h])).astype(BF16) for h in heads]
    y0 = [_dot(rb16[h], u016[h]) + _dot(rk16[h], v16[h]) for h in heads]

    order = range(n_chunks - 1, -1, -1) if reverse else range(n_chunks)
    eye_h = eye[:hd, :hd]
    trans = {}
    for c in order:
        rs = slice(CHUNK * c, CHUNK * (c + 1))
        last = CHUNK * c if reverse else CHUNK * (c + 1) - 1
        decay = e_pos[last:last + 1, :]
        for h in heads:
            g = (eye_h + _mm3tn(wa[h][rs], bt_all[rs, sls[h]])) * decay[:, sls[h]]
            hh = (_dot_tn(u016[h][rs], bt16[h][rs]) + _dot_tn(v16[h][rs], kt16[h][rs])) * decay[:, sls[h]]
            trans[c, h] = (g, hh)
    for c in order:
        rs = slice(CHUNK * c, CHUNK * (c + 1))
        s0 = [s_ref[h] for h in heads]
        for h in heads:
            s_ref[h] = _mm3(s0[h], trans[c, h][0]) + trans[c, h][1]
        y_ref[rs, :] = jnp.concatenate(
            [_dot_nt(yr16[h][rs], s0[h].astype(BF16)) + y0[h][rs] for h in heads], axis=-1)


def _mm3tn(a, b):
    ah, al = _split2(a)
    bh, bl = _split2(b)
    return _dot_tn(ah, bh) + (_dot_tn(ah, bl) + _dot_tn(al, bh))


def _rwkv_scan(lw, kd, a, b, r, v, *, d, batch, n_tiles):
    n = r.shape[0]
    gw = GROUP_WIDTH
    reverse = d == 1

    def tile(b_, s):
        if reverse:
            t = jnp.where(s == 0, 0, n_tiles - s)
        else:
            t = s
        return b_ * n_tiles + t

    dspec = pl.BlockSpec((None, TM, gw), lambda b_, s: (d, tile(b_, s), 0))
    spec = pl.BlockSpec((TM, gw), lambda b_, s: (tile(b_, s), 0))
    return pl.pallas_call(
        functools.partial(_rwkv_scan_kernel, reverse=reverse),
        grid=(batch, n_tiles),
        in_specs=[dspec, dspec, dspec, dspec, spec, spec],
        out_specs=spec,
        out_shape=jax.ShapeDtypeStruct((n, gw), F32),
        scratch_shapes=[pltpu.VMEM((GROUP_HEADS, HEAD_DIM, HEAD_DIM), F32)],
        compiler_params=_cparams(("parallel", "arbitrary")),
        name="rwkv_scan_rev" if reverse else "rwkv_scan_fwd",
    )(lw, kd, a, b, r, v)


def _rwkv_out_kernel(yf_ref, yr_ref, bonus_ref, g_ref, lnw_ref, lnb_ref, ones_ref, o_ref):
    ones = ones_ref[...]

    def seg_mean(x):
        s = jnp.concatenate([_mm_exact_rhs(x[:, :LANES], ones), _mm_exact_rhs(x[:, LANES:], ones)], axis=-1)
        return s * (1.0 / HEAD_DIM)

    y = yf_ref[...] + yr_ref[...]
    dlt = y - seg_mean(y)
    yn = dlt * lax.rsqrt(seg_mean(dlt * dlt) + RWKV_GN_EPS) * lnw_ref[...] + lnb_ref[...]
    o_ref[...] = ((yn + bonus_ref[...]) * g_ref[...]).astype(o_ref.dtype)


def _rwkv_out(yf, yr, bonus, g, p):
    n, gw = yf.shape
    spec = pl.BlockSpec((TM, gw), lambda i: (i, 0))
    row = pl.BlockSpec((1, gw), lambda i: (0, 0))
    return pl.pallas_call(
        _rwkv_out_kernel,
        grid=(n // TM,),
        in_specs=[spec, spec, spec, spec, row, row, pl.BlockSpec((LANES, LANES), lambda i: (0, 0))],
        out_specs=spec,
        out_shape=jax.ShapeDtypeStruct((n, gw), BF16),
        compiler_params=_cparams(("parallel",)),
        name="rwkv_out",
    )(yf, yr, bonus, g, p['ln_w'], p['ln_b'], p['ones64'])


def _out_mlp_kernel(x_ref, oa_ref, ob_ref, oc_ref, od_ref, mod_ref, gpost_ref, gpre_ref, gpm_ref,
                    wo_ref, w1_ref, w2_ref, xo_ref):
    d = D_MODEL
    gw = GROUP_WIDTH
    mod = mod_ref[...]
    y = _dot(oa_ref[...], wo_ref[0:gw, :])
    y += _dot(ob_ref[...], wo_ref[gw:2 * gw, :])
    y += _dot(oc_ref[...], wo_ref[2 * gw:3 * gw, :])
    y += _dot(od_ref[...], wo_ref[3 * gw:4 * gw, :])
    x1 = x_ref[...] + mod[:, 2 * d:3 * d] * (_rms(y, NORM_EPS) * gpost_ref[...])
    h = _rms(x1, NORM_EPS) * gpre_ref[...] * (1.0 + mod[:, 4 * d:5 * d]) + mod[:, 3 * d:4 * d]
    u = jnp.maximum(_dot(h.astype(BF16), w1_ref[...]), 0.0)
    zz = _dot((u * u).astype(BF16), w2_ref[...])
    xo_ref[...] = x1 + mod[:, 5 * d:6 * d] * (_rms(zz, NORM_EPS) * gpm_ref[...])


def _out_mlp(xs, oa, ob, oc, od, mod3, p, mod_row):
    n = xs.shape[0]
    d = D_MODEL
    gw = GROUP_WIDTH
    ospec = pl.BlockSpec((TM, gw), lambda i: (i, 0))
    row = pl.BlockSpec((1, d), lambda i: (0, 0))

    def wspec(shape):
        return pl.BlockSpec(shape, lambda i: (0, 0), pipeline_mode=pl.Buffered(1))

    return pl.pallas_call(
        _out_mlp_kernel,
        grid=(n // TM,),
        in_specs=[pl.BlockSpec((TM, d), lambda i: (i, 0)), ospec, ospec, ospec, ospec,
                  pl.BlockSpec((None, 1, 6 * d), lambda i: (mod_row(i), 0, 0)),
                  row, row, row,
                  wspec((d, d)), wspec((d, D_FF)), wspec((D_FF, d))],
        out_specs=pl.BlockSpec((TM, d), lambda i: (i, 0)),
        out_shape=jax.ShapeDtypeStruct((n, d), F32),
        compiler_params=_cparams(("parallel",)),
        name="out_mlp",
    )(xs, oa, ob, oc, od, mod3, p['g_post_mix'], p['g_pre_mlp'], p['g_post_mlp'],
      p['w_out'], p['w_mlp1'], p['w_mlp2'])


def _rope_tables(seq, ctx_len):
    t = jnp.arange(seq, dtype=jnp.int32)
    row = (t // GRID_W).astype(F32)
    col = (t % GRID_W).astype(F32)

    def tables(rot_dim, lane_dim):
        n = rot_dim // 4
        inv = ROPE_THETA ** (-jnp.arange(n, dtype=F32) / n)
        ar, ac = row[:, None] * inv, col[:, None] * inv
        cos = jnp.concatenate([jnp.cos(ar), jnp.cos(ar), jnp.cos(ac), jnp.cos(ac)], axis=-1)
        sin = jnp.concatenate([-jnp.sin(ar), jnp.sin(ar), -jnp.sin(ac), jnp.sin(ac)], axis=-1)
        return cos, sin

    def with_ctx(tab, fill):
        return jnp.concatenate([jnp.full((ctx_len, tab.shape[1]), fill, F32), tab], axis=0)

    cos, sin = tables(HEAD_DIM, LANES)
    cs = with_ctx(jnp.tile(cos, (1, LANES // HEAD_DIM)), 1.0)
    sn = with_ctx(jnp.tile(sin, (1, LANES // HEAD_DIM)), 0.0)
    cosm, sinm = tables(MLA_ROPE_DIM, LANES)
    pad_l, pad_r = MLA_NOPE_DIM, LANES - MLA_QK_DIM
    csm = with_ctx(jnp.pad(cosm, ((0, 0), (pad_l, pad_r)), constant_values=1.0), 1.0)
    snm = with_ctx(jnp.pad(sinm, ((0, 0), (pad_l, pad_r))), 0.0)
    return cs, sn, csm, snm


def _layer_params(l, w):
    d = D_MODEL
    gw = GROUP_WIDTH
    w_in = w['w_in'][l]
    a_end, b_end, c_end = 512, 512 + 416, 512 + 416 + 1056
    zeros = lambda c: jnp.zeros((d, c), F32)
    w_in_p = jnp.concatenate([
        w_in[:, :a_end],
        w_in[:, a_end:a_end + 384], zeros(MLA_NOPE_DIM), w_in[:, a_end + 384:b_end], zeros(LANES - MLA_QK_DIM),
        w_in[:, c_end:],
        w_in[:, b_end:c_end], zeros(C_COLS_PAD - 1056)], axis=1).astype(BF16)
    wuq = w['mla_w_uq'][l].reshape(MLA_Q_RANK, GROUP_HEADS, MLA_QK_DIM)
    wuq = jnp.pad(wuq, ((0, 0), (0, 0), (0, LANES - MLA_QK_DIM))).reshape(MLA_Q_RANK, GROUP_HEADS * LANES)
    wukv = w['mla_w_ukv'][l].reshape(MLA_KV_RANK, GROUP_HEADS, MLA_NOPE_DIM + HEAD_DIM)
    wk = jnp.pad(wukv[:, :, :MLA_NOPE_DIM], ((0, 0), (0, 0), (0, LANES - MLA_NOPE_DIM)))
    wukv_p = jnp.concatenate([wk.reshape(MLA_KV_RANK, GROUP_HEADS * LANES),
                              wukv[:, :, MLA_NOPE_DIM:].reshape(MLA_KV_RANK, gw)], axis=1)
    lane = jnp.arange(LANES)
    ones64 = ((lane[:, None] // HEAD_DIM) == (lane[None, :] // HEAD_DIM)).astype(BF16)
    zl = jnp.zeros((2, LANES - RWKV_DECAY_LORA, gw), F32)
    return {
        'w_in_p': w_in_p,
        'g_pre_mix': w['g_pre_mix'][l].reshape(1, d),
        'g_post_mix': w['g_post_mix'][l].reshape(1, d),
        'g_pre_mlp': w['g_pre_mlp'][l].reshape(1, d),
        'g_post_mlp': w['g_post_mlp'][l].reshape(1, d),
        'qn': jnp.tile(w['gqa_q_norm'][l], 2).reshape(1, LANES),
        'kn': jnp.tile(w['gqa_k_norm'][l], 2).reshape(1, LANES),
        'mqn': w['mla_q_norm'][l].reshape(1, MLA_Q_RANK),
        'mkvn': w['mla_kv_norm'][l].reshape(1, MLA_KV_RANK),
        'wuq': wuq.astype(BF16),
        'wukv': wukv_p.astype(BF16),
        'ones64': ones64,
        'mu': jnp.pad(w['rwkv_mu'][l], (0, C_COLS_PAD - 1056)).reshape(1, C_COLS_PAD),
        'w0': w['rwkv_w0'][l].reshape(2, 1, gw),
        'w2p': jnp.concatenate([w['rwkv_w2'][l], zl], axis=1),
        'a0': w['rwkv_a0'][l].reshape(2, 1, gw),
        'a2p': jnp.concatenate([zl, w['rwkv_a2'][l]], axis=1),
        'k_k': w['rwkv_k_k'][l].reshape(2, 1, gw),
        'k_a': w['rwkv_k_a'][l].reshape(2, 1, gw),
        'r_k': w['rwkv_r_k'][l].reshape(1, gw),
        'g2p': jnp.pad(w['rwkv_g2'][l], ((0, gw - RWKV_GATE_LORA), (0, 0))),
        'ln_w': w['rwkv_ln_w'][l].reshape(1, gw),
        'ln_b': w['rwkv_ln_b'][l].reshape(1, gw),
        'sink': w['swa_sink'][l],
        'w_out': w['w_out'][l].astype(BF16),
        'w_mlp1': w['w_mlp1'][l].astype(BF16),
        'w_mlp2': w['w_mlp2'][l].astype(BF16),
    }


def kernel(x, c, ctx, c_ctx, w_mod, b_mod, g_pre_mix, g_post_mix, g_pre_mlp, g_post_mlp, w_in, gqa_q_norm, gqa_k_norm, mla_q_norm, mla_kv_norm, mla_w_uq, mla_w_ukv, rwkv_mu, rwkv_w0, rwkv_w2, rwkv_a0, rwkv_a2, rwkv_k_k, rwkv_k_a, rwkv_r_k, rwkv_g2, rwkv_ln_w, rwkv_ln_b, swa_sink, w_out, w_mlp1, w_mlp2):
    batch, seq, d = x.shape
    ctx_len = ctx.shape[1]
    depth = w_mod.shape[0]
    assert d == D_MODEL and ctx_len == TM and seq % TM == 0 and seq % GRID_W == 0
    assert batch + 1 <= 8
    s_tot = ctx_len + seq
    n_tiles = s_tot // TM
    w = dict(w_in=w_in, g_pre_mix=g_pre_mix, g_post_mix=g_post_mix, g_pre_mlp=g_pre_mlp,
             g_post_mlp=g_post_mlp, gqa_q_norm=gqa_q_norm, gqa_k_norm=gqa_k_norm, mla_q_norm=mla_q_norm,
             mla_kv_norm=mla_kv_norm, mla_w_uq=mla_w_uq, mla_w_ukv=mla_w_ukv, rwkv_mu=rwkv_mu,
             rwkv_w0=rwkv_w0, rwkv_w2=rwkv_w2, rwkv_a0=rwkv_a0, rwkv_a2=rwkv_a2, rwkv_k_k=rwkv_k_k,
             rwkv_k_a=rwkv_k_a, rwkv_r_k=rwkv_r_k, rwkv_g2=rwkv_g2, rwkv_ln_w=rwkv_ln_w,
             rwkv_ln_b=rwkv_ln_b, swa_sink=swa_sink, w_out=w_out, w_mlp1=w_mlp1, w_mlp2=w_mlp2)

    ct = jnp.concatenate([c, c_ctx[None, :], jnp.zeros((8 - batch - 1, d), F32)], axis=0).T
    mods = _modulation(ct, w_mod, b_mod, batch + 1)

    def mod_row(i):
        return jnp.where(i % n_tiles == 0, batch, i // n_tiles)

    tabs = _rope_tables(seq, ctx_len)
    xs = jnp.concatenate([ctx, x], axis=1).reshape(batch * s_tot, d)
    geo = dict(batch=batch, s_tot=s_tot, ctx_len=ctx_len)
    for l in range(depth):
        p = _layer_params(l, w)
        mod3 = mods[l].reshape(8, 1, 6 * d)
        zabd, zc = _inproj(xs, mod3, p['g_pre_mix'], p['w_in_p'], mod_row)
        qa, ka, va, qb, kb, vb, qd, kd, vd = _attn_prep(zabd, tabs, p, n_tiles)
        oa = _flash(qa, ka, va, shared_kv=True, **geo)
        ob = _flash(qb, kb, vb, shared_kv=False, **geo)
        od = _window_attn(p['sink'], qd, kd, vd, **geo)
        lw, rkd, ra, rb, rr, rv, rg, bonus = _rwkv_prep(zc, p, n_tiles)
        yf = _rwkv_scan(lw, rkd, ra, rb, rr, rv, d=0, batch=batch, n_tiles=n_tiles)
        yr = _rwkv_scan(lw, rkd, ra, rb, rr, rv, d=1, batch=batch, n_tiles=n_tiles)
        oc = _rwkv_out(yf, yr, bonus, rg, p)
        xs = _out_mlp(xs, oa, ob, oc, od, mod3, p, mod_row)
    return xs.reshape(batch, s_tot, d)[:, ctx_len:, :]
```

```python
import functools
import math

import jax
import jax.numpy as jnp
from jax import lax
from jax.experimental import pallas as pl
from jax.experimental.pallas import tpu as pltpu

F32 = jnp.float32
BF16 = jnp.bfloat16

D_MODEL = 1024
GRID_W = 64
HEAD_DIM = 64
GROUP_WIDTH = 256
GROUP_HEADS = 4
ROPE_THETA = 10000.0
NORM_EPS = 1e-6
NEG_INF = -1e30
MLA_Q_RANK = 256
MLA_KV_RANK = 128
MLA_NOPE_DIM = 64
MLA_ROPE_DIM = 32
MLA_QK_DIM = MLA_NOPE_DIM + MLA_ROPE_DIM
RWKV_DECAY_LORA = 64
RWKV_ICLR_LORA = 64
RWKV_GATE_LORA = 160
RWKV_GN_EPS = 64e-5
WINDOW = 128
D_FF = 4 * D_MODEL

LOG2E = math.log2(math.e)
LANES = 128
TM = 256
ONES_ROWS = 8
LOOKAHEAD = 2
WINDOW_BLOCKS = 2
CHUNK = 64
SUB = 16
SEC_A = 0
SEC_B = 512
SEC_D = 1024
ABD_COLS = 1536
C_COLS_PAD = 1152
VMEM_LIMIT = 56 * 1024 * 1024


def _cparams(sem):
    return pltpu.CompilerParams(dimension_semantics=sem, vmem_limit_bytes=VMEM_LIMIT)


def _dot(a, b):
    return jnp.dot(a, b, preferred_element_type=F32)


def _dot_nt(a, b):
    return lax.dot_general(a, b, (((1,), (1,)), ((), ())), preferred_element_type=F32)


def _dot_tn(a, b):
    return lax.dot_general(a, b, (((0,), (0,)), ((), ())), preferred_element_type=F32)


def _dot16(a, b):
    return _dot(a.astype(BF16), b.astype(BF16))


def _split2(x):
    hi = x.astype(BF16)
    lo = (x - hi.astype(F32)).astype(BF16)
    return hi, lo


def _mm_exact_rhs(a, b_bf16):
    hi, lo = _split2(a)
    return _dot(hi, b_bf16) + _dot(lo, b_bf16)


def _mm_exact_lhs(a_bf16, b):
    hi, lo = _split2(b)
    return _dot(a_bf16, hi) + _dot(a_bf16, lo)


def _mm3(a, b):
    ah, al = _split2(a)
    bh, bl = _split2(b)
    return _dot(ah, bh) + (_dot(ah, bl) + _dot(al, bh))


def _sigmoid(x):
    return 1.0 / (1.0 + jnp.exp(-x))


def _rms(x, eps):
    return x * lax.rsqrt(jnp.mean(x * x, axis=-1, keepdims=True) + eps)


def _mod_kernel(ct_ref, w_ref, b_ref, o_ref, *, n_rows):
    ct = ct_ref[...]
    st = ct * _sigmoid(ct)
    w = w_ref[...]
    rows = [jnp.sum(st[:, r:r + 1] * w, axis=0, keepdims=True) for r in range(n_rows)]
    rows.append(jnp.zeros((8 - n_rows, w.shape[1]), F32))
    o_ref[...] = jnp.concatenate(rows, axis=0) + b_ref[...]


def _modulation(ct, w_mod, b_mod, n_rows):
    depth, d, n6 = w_mod.shape
    tn = 1536
    return pl.pallas_call(
        functools.partial(_mod_kernel, n_rows=n_rows),
        grid=(depth, n6 // tn),
        in_specs=[pl.BlockSpec((d, 8), lambda l, j: (0, 0)),
                  pl.BlockSpec((None, d, tn), lambda l, j: (l, 0, j)),
                  pl.BlockSpec((None, 1, tn), lambda l, j: (l, 0, j))],
        out_specs=pl.BlockSpec((None, 8, tn), lambda l, j: (l, 0, j)),
        out_shape=jax.ShapeDtypeStruct((depth, 8, n6), F32),
        compiler_params=_cparams(("parallel", "parallel")),
        name="modulation",
    )(ct, w_mod, b_mod.reshape(depth, 1, n6))


def _rope_slab(x, cos, sin_signed, shift):
    left = pltpu.roll(x, LANES - shift, axis=1)
    right = pltpu.roll(x, shift, axis=1)
    lane = lax.broadcasted_iota(jnp.int32, x.shape, 1)
    first = ((lane // shift) % 2) == 0
    return x * cos + jnp.where(first, left, right) * sin_signed


def _head_norm_slab(x, gain, ones):
    ss = _mm_exact_rhs(x * x, ones)
    return x * lax.rsqrt(ss * (1.0 / HEAD_DIM) + NORM_EPS) * gain


def _inproj_prep_kernel(xc_ref, xl_ref, mod_ref, g_ref, w_ref,
                        cs_ref, sn_ref, csm_ref, snm_ref, qn_ref, kn_ref, mqn_ref, mkvn_ref,
                        wuq_ref, wukv_ref, ones_ref,
                        qa_ref, ka_ref, va_ref, qb_ref, kb_ref, vb_ref, qd_ref, kd_ref, vd_ref, zc_ref,
                        z_ref, *, n_tiles):
    d = D_MODEL
    mod = mod_ref[...]
    is_ctx = pl.program_id(0) % n_tiles == 0
    x = jnp.where(is_ctx, xc_ref[...], xl_ref[...])
    h = _rms(x, NORM_EPS) * g_ref[...] * (1.0 + mod[:, d:2 * d]) + mod[:, 0:d]
    z = _dot(h.astype(BF16), w_ref[...])
    z_ref[...] = z[:, :ABD_COLS]
    zc_ref[...] = z[:, ABD_COLS:]
    cs, sn = cs_ref[...], sn_ref[...]
    csm, snm = csm_ref[...], snm_ref[...]
    ones = ones_ref[...]
    hd = HEAD_DIM
    q_scale = HEAD_DIM ** -0.5

    def put_heads(ref, base, slab):
        ref[base] = slab[:, :hd].astype(BF16)
        ref[base + 1] = slab[:, hd:].astype(BF16)

    def put_heads_t(ref, base, slab, ones_rows=ONES_ROWS):
        st = slab.T.astype(BF16)
        if ones_rows:
            one = jnp.ones((ones_rows, st.shape[1]), BF16)
            ref[base] = jnp.concatenate([st[:hd], one], axis=0)
            ref[base + 1] = jnp.concatenate([st[hd:], one], axis=0)
        else:
            ref[base] = st[:hd]
            ref[base + 1] = st[hd:]

    for s in range(2):
        x = z_ref[:, SEC_A + LANES * s:SEC_A + LANES * (s + 1)]
        x = _rope_slab(_head_norm_slab(x, qn_ref[...], ones), cs, sn, 16) * (q_scale * LOG2E)
        put_heads_t(qa_ref, 2 * s, x, 0)
    x = z_ref[:, SEC_A + 256:SEC_A + 384]
    put_heads(ka_ref, 0, _rope_slab(_head_norm_slab(x, kn_ref[...], ones), cs, sn, 16))
    put_heads_t(va_ref, 0, z_ref[:, SEC_A + 384:SEC_A + 512])

    for s in range(2):
        x = z_ref[:, SEC_D + LANES * s:SEC_D + LANES * (s + 1)]
        put_heads(qd_ref, 2 * s, _rope_slab(x, cs, sn, 16) * q_scale)
    put_heads(kd_ref, 0, _rope_slab(z_ref[:, SEC_D + 256:SEC_D + 384], cs, sn, 16))
    put_heads(vd_ref, 0, z_ref[:, SEC_D + 384:SEC_D + 512])

    cq = _rms(z_ref[:, SEC_B:SEC_B + MLA_Q_RANK], NORM_EPS) * mqn_ref[...]
    q = _dot(cq.astype(BF16), wuq_ref[...])
    ckv = _rms(z_ref[:, SEC_B + 256:SEC_B + 384], NORM_EPS) * mkvn_ref[...]
    kv = _dot(ckv.astype(BF16), wukv_ref[...])
    kr = _rope_slab(z_ref[:, SEC_B + 384:SEC_B + 512], csm, snm, 8)
    b_scale = MLA_QK_DIM ** -0.5 * LOG2E
    for h in range(GROUP_HEADS):
        qh = q[:, LANES * h:LANES * (h + 1)]
        qb_ref[h] = (_rope_slab(qh, csm, snm, 8) * b_scale).T.astype(BF16)
        kb_ref[h] = (kv[:, LANES * h:LANES * (h + 1)] + kr).astype(BF16)
    for s in range(2):
        put_heads_t(vb_ref, 2 * s, kv[:, 4 * LANES + LANES * s:4 * LANES + LANES * (s + 1)])


def _inproj_prep(xc, xl, xc_map, xl_map, mod3, tabs, p, mod_row, n, n_tiles):
    cs, sn, csm, snm = tabs
    cols = p['w_in_p'].shape[1]

    def tab_spec():
        return pl.BlockSpec((TM, LANES), lambda i: (i % n_tiles, 0))

    def row_spec(w):
        return pl.BlockSpec((1, w), lambda i: (0, 0))

    def heads_out(nh, w):
        return (pl.BlockSpec((nh, TM, w), lambda i: (0, i, 0)),
                jax.ShapeDtypeStruct((nh, n, w), BF16))

    def heads_out_t(nh, rows=HEAD_DIM + ONES_ROWS):
        return (pl.BlockSpec((nh, None, rows, TM), lambda i: (0, i, 0, 0)),
                jax.ShapeDtypeStruct((nh, n // TM, rows, TM), BF16))

    outs = [heads_out_t(4, HEAD_DIM), heads_out(2, 64), heads_out_t(2),
            heads_out_t(4, LANES), heads_out(4, 128), heads_out_t(4),
            heads_out(4, 64), heads_out(2, 64), heads_out(2, 64),
            (pl.BlockSpec((TM, C_COLS_PAD), lambda i: (i, 0)), jax.ShapeDtypeStruct((n, C_COLS_PAD), F32))]
    return pl.pallas_call(
        functools.partial(_inproj_prep_kernel, n_tiles=n_tiles),
        grid=(n // TM,),
        in_specs=[pl.BlockSpec((TM, D_MODEL), xc_map), pl.BlockSpec((TM, D_MODEL), xl_map),
                  pl.BlockSpec((None, 1, 6 * D_MODEL), lambda i: (mod_row(i), 0, 0)),
                  row_spec(D_MODEL),
                  pl.BlockSpec((D_MODEL, cols), lambda i: (0, 0)),
                  tab_spec(), tab_spec(), tab_spec(), tab_spec(),
                  row_spec(LANES), row_spec(LANES), row_spec(MLA_Q_RANK), row_spec(MLA_KV_RANK),
                  pl.BlockSpec((MLA_Q_RANK, 512), lambda i: (0, 0)),
                  pl.BlockSpec((MLA_KV_RANK, 768), lambda i: (0, 0)),
                  pl.BlockSpec((LANES, LANES), lambda i: (0, 0))],
        out_specs=[o[0] for o in outs],
        out_shape=[o[1] for o in outs],
        scratch_shapes=[pltpu.VMEM((TM, ABD_COLS), F32)],
        compiler_params=_cparams(("parallel",)),
        name="inproj_prep",
    )(xc, xl, mod3, p['g_pre_mix'], p['w_in_p'], cs, sn, csm, snm,
      p['qn'], p['kn'], p['mqn'], p['mkvn'], p['wuq'], p['wukv'], p['ones64'])


def _flash_kernel(q_ref, k_ref, vt_ref, o_ref, *, shared_kv, tq, tk, ctx_len, s_tot):
    qi = pl.program_id(2)
    dv = vt_ref.shape[-2] - ONES_ROWS
    if shared_kv:
        streams = [(jnp.concatenate([q_ref[0], q_ref[1]], axis=1), 0)]
    else:
        streams = [(q_ref[0], 0), (q_ref[1], 1)]

    def scores(c):
        out = []
        for q, kv in streams:
            k = k_ref[kv, pl.ds(pl.multiple_of(c * tk, tk), tk), :]
            out.append(_dot(k, q))
        return out

    def softmax_pv(c, sts, carries):
        new = []
        for st, (_, kv), (m, acc) in zip(sts, streams, carries):
            m_new = jnp.maximum(m, jnp.max(st, axis=0, keepdims=True))
            pt = jnp.exp2(st - m_new).astype(BF16)
            acc = jnp.exp2(m - m_new) * acc + _dot(vt_ref[kv, c], pt)
            new.append((m_new, acc))
        return tuple(new)

    def run(n_chunks):
        unroll = next(u for u in (33, 11, 3, 2, 1) if n_chunks % u == 0)
        groups = n_chunks // unroll
        init = tuple((jnp.full((1, q.shape[1]), NEG_INF, F32), jnp.zeros((dv + ONES_ROWS, q.shape[1]), F32))
                     for q, _ in streams)

        def body(g, carries):
            queue = [scores(g * unroll + u) for u in range(min(LOOKAHEAD, unroll))]
            for u in range(unroll):
                if u + LOOKAHEAD < unroll:
                    queue.append(scores(g * unroll + u + LOOKAHEAD))
                carries = softmax_pv(g * unroll + u, queue.pop(0), carries)
            return carries

        carries = body(0, init) if groups == 1 else lax.fori_loop(0, groups, body, init)
        outs = [acc[:dv] / acc[dv:dv + 1] for _, acc in carries]
        if shared_kv:
            outs = [outs[0][:, :tq], outs[0][:, tq:]]
        o_ref[...] = jnp.concatenate(outs, axis=0).T.astype(o_ref.dtype)

    is_ctx = qi * tq < ctx_len

    @pl.when(is_ctx)
    def _():
        run(ctx_len // tk)

    @pl.when(jnp.logical_not(is_ctx))
    def _():
        run(s_tot // tk)


def _flash(qt, k, vt, *, shared_kv, batch, s_tot, ctx_len):
    nh, _, dk, _ = qt.shape
    n = k.shape[1]
    dv = vt.shape[-2] - ONES_ROWS
    nkv = 1 if shared_kv else 2
    tq = tk = TM
    nq = s_tot // tq
    kern = functools.partial(_flash_kernel, shared_kv=shared_kv, tq=tq, tk=tk,
                             ctx_len=ctx_len, s_tot=s_tot)
    return pl.pallas_call(
        kern,
        grid=(batch, nh // 2, nq),
        in_specs=[pl.BlockSpec((2, None, dk, tq), lambda b, p, i: (p, b * nq + i, 0, 0)),
                  pl.BlockSpec((nkv, s_tot, dk), lambda b, p, i: (p, b, 0)),
                  pl.BlockSpec((nkv, s_tot // tk, dv + ONES_ROWS, tk), lambda b, p, i: (p, b, 0, 0))],
        out_specs=pl.BlockSpec((tq, 2 * dv), lambda b, p, i: (b * nq + i, p)),
        out_shape=jax.ShapeDtypeStruct((n, nh * dv), BF16),
        compiler_params=_cparams(("parallel", "parallel", "arbitrary")),
        name="flash_shared" if shared_kv else "flash_split",
    )(qt, k, vt)


def _window_kernel(sink_ref, q_ref, k_ref, v_ref, o_ref, *, ctx_len, s_tot):
    w = WINDOW
    g = pl.program_id(1)
    first_lat = ctx_len // w
    last_blk = s_tot // w - 1
    r = lax.broadcasted_iota(jnp.int32, (2 * w, 3 * w), 0) % w
    c = lax.broadcasted_iota(jnp.int32, (2 * w, 3 * w), 1)
    row = lax.broadcasted_iota(jnp.int32, (2 * w, 1), 0)
    sk = jnp.where(row < w, sink_ref[2 * g], sink_ref[2 * g + 1])
    k_ctx = k_ref[0:ctx_len, :]
    v_ctx = v_ref[0:ctx_len, :]

    def rows(ref, blk):
        return ref[pl.ds(pl.multiple_of(blk * w, w), w), :]

    for i in range(WINDOW_BLOCKS):
        jb = pl.program_id(2) * WINDOW_BLOCKS + i
        is_lat = jb >= first_lat
        pb = jnp.clip(jb - 1, 0, last_blk)
        nb = jnp.clip(jb + 1, 0, last_blk)
        q = q_ref[:, w * i:w * (i + 1), :].reshape(2 * w, HEAD_DIM)
        kw = jnp.concatenate([rows(k_ref, pb), rows(k_ref, jb), rows(k_ref, nb)], axis=0)
        vw = jnp.concatenate([rows(v_ref, pb), rows(v_ref, jb), rows(v_ref, nb)], axis=0)
        s_w = _dot_nt(q, kw)
        s_c = _dot_nt(q, k_ctx)
        lo = jnp.where(is_lat, jnp.where(jb - 1 >= first_lat, 0, w), 3 * w)
        hi = jnp.where(jb + 1 <= last_blk, 3 * w - 1, 2 * w - 1)
        valid = jnp.logical_and(c >= jnp.maximum(r, lo), c <= jnp.minimum(r + 2 * w, hi))
        s_w = jnp.where(valid, s_w, NEG_INF)
        m = jnp.maximum(jnp.maximum(jnp.max(s_w, axis=-1, keepdims=True),
                                    jnp.max(s_c, axis=-1, keepdims=True)), sk)
        p_w = jnp.exp(s_w - m)
        p_c = jnp.exp(s_c - m)
        denom = jnp.exp(sk - m) + jnp.sum(p_w, axis=-1, keepdims=True) + jnp.sum(p_c, axis=-1, keepdims=True)
        o = (_dot(p_w.astype(BF16), vw) + _dot(p_c.astype(BF16), v_ctx)) / denom
        o_ref[w * i:w * (i + 1), :] = jnp.concatenate([o[:w], o[w:]], axis=-1).astype(o_ref.dtype)


def _window_attn(sink, q, k, v, *, batch, s_tot, ctx_len):
    nh, n, dk = q.shape
    rows = WINDOW * WINDOW_BLOCKS
    assert ctx_len % rows == 0 and s_tot % rows == 0
    nstep = s_tot // rows
    kern = functools.partial(_window_kernel, ctx_len=ctx_len, s_tot=s_tot)
    return pl.pallas_call(
        kern,
        grid=(batch, nh // 2, nstep),
        in_specs=[pl.BlockSpec(memory_space=pltpu.SMEM),
                  pl.BlockSpec((2, rows, dk), lambda b, g, j: (g, b * nstep + j, 0)),
                  pl.BlockSpec((None, s_tot, dk), lambda b, g, j: (g, b, 0)),
                  pl.BlockSpec((None, s_tot, dk), lambda b, g, j: (g, b, 0))],
        out_specs=pl.BlockSpec((rows, 2 * dk), lambda b, g, j: (b * nstep + j, g)),
        out_shape=jax.ShapeDtypeStruct((n, nh * dk), BF16),
        compiler_params=_cparams(("parallel", "parallel", "arbitrary")),
        name="window_attn",
    )(sink, q, k, v)


def _rwkv_prep_kernel(z_ref, zp_ref, zn_ref, mu_ref, w0_ref, w2_ref, a0_ref, a2_ref, kk_ref, ka_ref,
                      rk_ref, g2_ref, ones_ref,
                      lw_ref, kd_ref, a_ref, b_ref, r_ref, v_ref, g_ref, bonus_ref, *, n_tiles):
    j = pl.program_id(0) % n_tiles
    z = z_ref[...]
    rows = lax.broadcasted_iota(jnp.int32, z.shape, 0)
    prev_row = jnp.where(j <= 1, 0.0, zp_ref[7:8, :])
    next_row = jnp.where(jnp.logical_or(j == 0, j == n_tiles - 1), 0.0, zn_ref[0:1, :])
    z_prev = jnp.where(rows == 0, prev_row, pltpu.roll(z, 1, axis=0))
    z_next = jnp.where(rows == TM - 1, next_row, pltpu.roll(z, TM - 1, axis=0))
    zs = z + mu_ref[...] * (0.5 * (z_prev + z_next) - z)

    gw = GROUP_WIDTH
    r, k, v = zs[:, 0:gw], zs[:, gw:2 * gw], zs[:, 2 * gw:3 * gw]
    lora = zs[:, 3 * gw:3 * gw + LANES]
    gl = zs[:, 3 * gw + LANES:3 * gw + LANES + gw]
    wt = jnp.tanh(lora)
    ones = ones_ref[...]

    def seg_sum(x):
        return jnp.concatenate([_mm_exact_rhs(x[:, :LANES], ones), _mm_exact_rhs(x[:, LANES:], ones)], axis=-1)

    r_ref[...] = r
    v_ref[...] = v
    g_ref[...] = _mm3(_sigmoid(gl), g2_ref[...])
    kd_sum = None
    for d in range(2):
        u = w0_ref[d] + _mm3(wt, w2_ref[d])
        soft = jnp.maximum(-u, 0.0) + jnp.log(1.0 + jnp.exp(-jnp.abs(u)))
        lw_ref[d] = -jnp.exp(-soft - 0.5)
        gate = _sigmoid(a0_ref[d] + _mm3(lora, a2_ref[d]))
        kk = k * kk_ref[d]
        kk = kk / jnp.maximum(jnp.sqrt(seg_sum(kk * kk)), 1e-12)
        kd = k * (1.0 + (gate - 1.0) * ka_ref[d])
        kd_ref[d] = kd
        a_ref[d] = -kk
        b_ref[d] = kk * gate
        kd_sum = kd if kd_sum is None else kd_sum + kd
    bonus_ref[...] = seg_sum(r * kd_sum * rk_ref[...]) * v


def _rwkv_prep(zc, p, n_tiles):
    n = zc.shape[0]
    gw = GROUP_WIDTH
    nb8 = n // 8

    def full(shape):
        nd = len(shape)
        return pl.BlockSpec(shape, lambda i: (0,) * nd)

    dir_out = (pl.BlockSpec((2, TM, gw), lambda i: (0, i, 0)), jax.ShapeDtypeStruct((2, n, gw), F32))
    one_out = (pl.BlockSpec((TM, gw), lambda i: (i, 0)), jax.ShapeDtypeStruct((n, gw), F32))
    outs = [dir_out] * 4 + [one_out] * 4
    return pl.pallas_call(
        functools.partial(_rwkv_prep_kernel, n_tiles=n_tiles),
        grid=(n // TM,),
        in_specs=[pl.BlockSpec((TM, C_COLS_PAD), lambda i: (i, 0)),
                  pl.BlockSpec((8, C_COLS_PAD), lambda i: (jnp.maximum(i * (TM // 8) - 1, 0), 0)),
                  pl.BlockSpec((8, C_COLS_PAD), lambda i: (jnp.minimum((i + 1) * (TM // 8), nb8 - 1), 0)),
                  full((1, C_COLS_PAD)),
                  full((2, 1, gw)), full((2, LANES, gw)), full((2, 1, gw)), full((2, LANES, gw)),
                  full((2, 1, gw)), full((2, 1, gw)), full((1, gw)), full((gw, gw)),
                  full((LANES, LANES))],
        out_specs=[o[0] for o in outs],
        out_shape=[o[1] for o in outs],
        compiler_params=_cparams(("parallel",)),
        name="rwkv_prep",
    )(zc, zc, zc, p['mu'], p['w0'], p['w2p'], p['a0'], p['a2p'], p['k_k'], p['k_a'], p['r_k'],
      p['g2p'], p['ones64'])


def _rwkv_scan_kernel(lw_ref, k_ref, a_ref, b_ref, r_ref, v_ref, y_ref, s_ref, *, reverse):
    @pl.when(pl.program_id(0) == 0)
    def _():
        s_ref[...] = jnp.zeros_like(s_ref)

    batch = lw_ref.shape[0]
    n_chunks = TM // CHUNK
    hd = HEAD_DIM
    row = lax.broadcasted_iota(jnp.int32, (TM, TM), 0)
    col = lax.broadcasted_iota(jnp.int32, (TM, TM), 1)
    same = (row // CHUNK) == (col // CHUNK)
    before = (col > row) if reverse else (col < row)
    m_strict = jnp.logical_and(same, before)
    m_incl = jnp.logical_and(same, jnp.logical_or(before, row == col))
    m_incl16 = m_incl.astype(BF16)
    eye = (row == col).astype(F32)
    assert CHUNK // SUB == 4
    same_sub = (row // SUB) == (col // SUB)

    items = [(b, h) for b in range(batch) for h in range(GROUP_HEADS)]
    idx = range(len(items))
    e_pos, at_all, rt_all, bt_all, kt_all, v_all = [], [], [], [], [], []
    for b in range(batch):
        lw = lw_ref[b]
        cum = _mm_exact_lhs(m_incl16, lw)
        e_pos.append(jnp.exp(cum))
        e_neg = jnp.exp(-cum)
        at_all.append(a_ref[b] * jnp.exp(cum - lw))
        rt_all.append(r_ref[b] * e_pos[b])
        bt_all.append(b_ref[b] * e_neg)
        kt_all.append(k_ref[b] * e_neg)
        v_all.append(v_ref[b])

    def head(arrs, b, h):
        return arrs[b][:, hd * h:hd * (h + 1)]

    at = [head(at_all, b, h) for b, h in items]
    rt = [head(rt_all, b, h) for b, h in items]
    bt = [head(bt_all, b, h) for b, h in items]
    at16 = [x.astype(BF16) for x in at]
    rt16 = [x.astype(BF16) for x in rt]
    bt16 = [x.astype(BF16) for x in bt]
    kt16 = [head(kt_all, b, h).astype(BF16) for b, h in items]
    v16 = [head(v_all, b, h).astype(BF16) for b, h in items]
    ab = [jnp.where(m_strict, _dot_nt(at16[i], bt16[i]), 0.0) for i in idx]
    ak16 = [jnp.where(m_strict, _dot_nt(at16[i], kt16[i]), 0.0).astype(BF16) for i in idx]
    rb16 = [jnp.where(m_incl, _dot_nt(rt16[i], bt16[i]), 0.0).astype(BF16) for i in idx]
    rk16 = [jnp.where(m_incl, _dot_nt(rt16[i], kt16[i]), 0.0).astype(BF16) for i in idx]
    pw = [jnp.where(same_sub, x, 0.0) for x in ab]
    lo16 = [jnp.where(same_sub, 0.0, x).astype(BF16) for x in ab]
    td = [eye + x for x in pw]
    for _ in range(int(math.log2(SUB)) - 1):
        pw16 = [x.astype(BF16) for x in pw]
        pw = [_dot(x, x) for x in pw16]
        td = [t + _dot16(t, x) for t, x in zip(td, pw)]
    td16 = [x.astype(BF16) for x in td]
    m1 = [_dot(t, x) for t, x in zip(td16, lo16)]
    m1_16 = [x.astype(BF16) for x in m1]
    m2 = [_dot(x, x) for x in m1_16]
    m3 = [_dot16(x, y) for x, y in zip(m1_16, m2)]
    nn16 = [(eye + a1 + a2 + a3).astype(BF16) for a1, a2, a3 in zip(m1, m2, m3)]
    akv = [_dot(ak16[i], v16[i]) for i in idx]
    tx = [_dot(td16[i], jnp.concatenate([at[i], akv[i]], axis=-1).astype(BF16)) for i in idx]
    wu = [_dot16(nn16[i], tx[i]) for i in idx]
    wa = [x[:, :hd] for x in wu]
    u016 = [x[:, hd:].astype(BF16) for x in wu]
    wa16 = [x.astype(BF16) for x in wa]
    yr16 = [(rt[i] + _dot(rb16[i], wa16[i])).astype(BF16) for i in idx]
    y0 = [_dot(rb16[i], u016[i]) + _dot(rk16[i], v16[i]) for i in idx]

    order = range(n_chunks - 1, -1, -1) if reverse else range(n_chunks)
    eye_h = eye[:hd, :hd]
    trans = {}
    for c in order:
        rs = slice(CHUNK * c, CHUNK * (c + 1))
        last = CHUNK * c if reverse else CHUNK * (c + 1) - 1
        for i, (b, h) in enumerate(items):
            decay = e_pos[b][last:last + 1, hd * h:hd * (h + 1)]
            g = (eye_h + _mm3tn(wa[i][rs], bt[i][rs])) * decay
            hh = (_dot_tn(u016[i][rs], bt16[i][rs]) + _dot_tn(v16[i][rs], kt16[i][rs])) * decay
            trans[c, i] = (g, hh)
    for c in order:
        rs = slice(CHUNK * c, CHUNK * (c + 1))
        s0 = [s_ref[b, h] for b, h in items]
        for i, (b, h) in enumerate(items):
            s_ref[b, h] = _mm3(s0[i], trans[c, i][0]) + trans[c, i][1]
        for b in range(batch):
            y_ref[b, rs, :] = jnp.concatenate(
                [_dot_nt(yr16[i][rs], s0[i].astype(BF16)) + y0[i][rs] for i, (bb, _) in enumerate(items) if bb == b],
                axis=-1)


def _mm3tn(a, b):
    ah, al = _split2(a)
    bh, bl = _split2(b)
    return _dot_tn(ah, bh) + (_dot_tn(ah, bl) + _dot_tn(al, bh))


def _rwkv_scan(lw, kd, a, b, r, v, *, d, batch, n_tiles):
    n, gw = r.shape
    s_tot = n // batch
    reverse = d == 1

    def tile(s):
        if reverse:
            return jnp.where(s == 0, 0, n_tiles - s)
        return s

    dspec = pl.BlockSpec((None, batch, TM, gw), lambda s: (d, 0, tile(s), 0))
    spec = pl.BlockSpec((batch, TM, gw), lambda s: (0, tile(s), 0))
    per_dir = [x.reshape(2, batch, s_tot, gw) for x in (lw, kd, a, b)]
    shared = [x.reshape(batch, s_tot, gw) for x in (r, v)]
    y = pl.pallas_call(
        functools.partial(_rwkv_scan_kernel, reverse=reverse),
        grid=(n_tiles,),
        in_specs=[dspec, dspec, dspec, dspec, spec, spec],
        out_specs=spec,
        out_shape=jax.ShapeDtypeStruct((batch, s_tot, gw), F32),
        scratch_shapes=[pltpu.VMEM((batch, GROUP_HEADS, HEAD_DIM, HEAD_DIM), F32)],
        compiler_params=_cparams(("arbitrary",)),
        name="rwkv_scan_rev" if reverse else "rwkv_scan_fwd",
    )(*per_dir, *shared)
    return y.reshape(n, gw)


def _out_mlp_kernel(xc_ref, xl_ref, oa_ref, ob_ref, od_ref, yf_ref, yr_ref, bonus_ref, g_ref,
                    lnw_ref, lnb_ref, ones_ref, mod_ref, gpost_ref, gpre_ref, gpm_ref,
                    wo_ref, w1_ref, w2_ref, xo_ref, *, ctx_first):
    d = D_MODEL
    gw = GROUP_WIDTH
    mod = mod_ref[...]
    ones = ones_ref[...]

    def seg_mean(t):
        sm = jnp.concatenate([_mm_exact_rhs(t[:, :LANES], ones), _mm_exact_rhs(t[:, LANES:], ones)], axis=-1)
        return sm * (1.0 / HEAD_DIM)

    yy = yf_ref[...] + yr_ref[...]
    dlt = yy - seg_mean(yy)
    yn = dlt * lax.rsqrt(seg_mean(dlt * dlt) + RWKV_GN_EPS) * lnw_ref[...] + lnb_ref[...]
    oc = ((yn + bonus_ref[...]) * g_ref[...]).astype(BF16)

    y = _dot(oa_ref[...], wo_ref[0:gw, :])
    y += _dot(ob_ref[...], wo_ref[gw:2 * gw, :])
    y += _dot(oc, wo_ref[2 * gw:3 * gw, :])
    y += _dot(od_ref[...], wo_ref[3 * gw:4 * gw, :])
    x = jnp.where(jnp.logical_and(ctx_first, pl.program_id(1) == 0), xc_ref[...], xl_ref[...])
    x1 = x + mod[:, 2 * d:3 * d] * (_rms(y, NORM_EPS) * gpost_ref[...])
    h = _rms(x1, NORM_EPS) * gpre_ref[...] * (1.0 + mod[:, 4 * d:5 * d]) + mod[:, 3 * d:4 * d]
    u = jnp.maximum(_dot(h.astype(BF16), w1_ref[...]), 0.0)
    zz = _dot((u * u).astype(BF16), w2_ref[...])
    xo_ref[...] = x1 + mod[:, 5 * d:6 * d] * (_rms(zz, NORM_EPS) * gpm_ref[...])


def _out_mlp(xc, xl, xc_map, xl_map, oa, ob, od, yf, yr, bonus, rg, mod3, p, *, batch, n_tiles, skip_ctx):
    d = D_MODEL
    gw = GROUP_WIDTH
    off = 1 if skip_ctx else 0
    nt = n_tiles - off

    def tile(b, j):
        return (b * n_tiles + off + j, 0)

    def mrow(b, j):
        return (jnp.where(off + j == 0, batch, b), 0, 0)

    ospec = pl.BlockSpec((TM, gw), tile)
    row = pl.BlockSpec((1, d), lambda b, j: (0, 0))
    grow = pl.BlockSpec((1, gw), lambda b, j: (0, 0))

    def wspec(shape):
        return pl.BlockSpec(shape, lambda b, j: (0, 0), pipeline_mode=pl.Buffered(1))

    return pl.pallas_call(
        functools.partial(_out_mlp_kernel, ctx_first=not skip_ctx),
        grid=(batch, nt),
        in_specs=[pl.BlockSpec((TM, d), lambda b, j: xc_map(b, off + j)),
                  pl.BlockSpec((TM, d), lambda b, j: xl_map(b, off + j)),
                  ospec, ospec, ospec, ospec, ospec, ospec, ospec,
                  grow, grow, pl.BlockSpec((LANES, LANES), lambda b, j: (0, 0)),
                  pl.BlockSpec((None, 1, 6 * d), mrow),
                  row, row, row,
                  wspec((d, d)), wspec((d, D_FF)), wspec((D_FF, d))],
        out_specs=pl.BlockSpec((TM, d), lambda b, j: (b * nt + j, 0)),
        out_shape=jax.ShapeDtypeStruct((batch * nt * TM, d), F32),
        compiler_params=_cparams(("parallel", "parallel")),
        name="out_mlp",
    )(xc, xl, oa, ob, od, yf, yr, bonus, rg, p['ln_w'], p['ln_b'], p['ones64'], mod3,
      p['g_post_mix'], p['g_pre_mlp'], p['g_post_mlp'], p['w_out'], p['w_mlp1'], p['w_mlp2'])


def _rope_tables(seq, ctx_len):
    t = jnp.arange(seq, dtype=jnp.int32)
    row = (t // GRID_W).astype(F32)
    col = (t % GRID_W).astype(F32)

    def tables(rot_dim, lane_dim):
        n = rot_dim // 4
        inv = ROPE_THETA ** (-jnp.arange(n, dtype=F32) / n)
        ar, ac = row[:, None] * inv, col[:, None] * inv
        cos = jnp.concatenate([jnp.cos(ar), jnp.cos(ar), jnp.cos(ac), jnp.cos(ac)], axis=-1)
        sin = jnp.concatenate([-jnp.sin(ar), jnp.sin(ar), -jnp.sin(ac), jnp.sin(ac)], axis=-1)
        return cos, sin

    def with_ctx(tab, fill):
        return jnp.concatenate([jnp.full((ctx_len, tab.shape[1]), fill, F32), tab], axis=0)

    cos, sin = tables(HEAD_DIM, LANES)
    cs = with_ctx(jnp.tile(cos, (1, LANES // HEAD_DIM)), 1.0)
    sn = with_ctx(jnp.tile(sin, (1, LANES // HEAD_DIM)), 0.0)
    cosm, sinm = tables(MLA_ROPE_DIM, LANES)
    pad_l, pad_r = MLA_NOPE_DIM, LANES - MLA_QK_DIM
    csm = with_ctx(jnp.pad(cosm, ((0, 0), (pad_l, pad_r)), constant_values=1.0), 1.0)
    snm = with_ctx(jnp.pad(sinm, ((0, 0), (pad_l, pad_r))), 0.0)
    return cs, sn, csm, snm


def _layer_params(l, w):
    d = D_MODEL
    gw = GROUP_WIDTH
    w_in = w['w_in'][l]
    a_end, b_end, c_end = 512, 512 + 416, 512 + 416 + 1056
    zeros = lambda c: jnp.zeros((d, c), F32)
    w_in_p = jnp.concatenate([
        w_in[:, :a_end],
        w_in[:, a_end:a_end + 384], zeros(MLA_NOPE_DIM), w_in[:, a_end + 384:b_end], zeros(LANES - MLA_QK_DIM),
        w_in[:, c_end:],
        w_in[:, b_end:c_end], zeros(C_COLS_PAD - 1056)], axis=1).astype(BF16)
    wuq = w['mla_w_uq'][l].reshape(MLA_Q_RANK, GROUP_HEADS, MLA_QK_DIM)
    wuq = jnp.pad(wuq, ((0, 0), (0, 0), (0, LANES - MLA_QK_DIM))).reshape(MLA_Q_RANK, GROUP_HEADS * LANES)
    wukv = w['mla_w_ukv'][l].reshape(MLA_KV_RANK, GROUP_HEADS, MLA_NOPE_DIM + HEAD_DIM)
    wk = jnp.pad(wukv[:, :, :MLA_NOPE_DIM], ((0, 0), (0, 0), (0, LANES - MLA_NOPE_DIM)))
    wukv_p = jnp.concatenate([wk.reshape(MLA_KV_RANK, GROUP_HEADS * LANES),
                              wukv[:, :, MLA_NOPE_DIM:].reshape(MLA_KV_RANK, gw)], axis=1)
    lane = jnp.arange(LANES)
    ones64 = ((lane[:, None] // HEAD_DIM) == (lane[None, :] // HEAD_DIM)).astype(BF16)
    zl = jnp.zeros((2, LANES - RWKV_DECAY_LORA, gw), F32)
    return {
        'w_in_p': w_in_p,
        'g_pre_mix': w['g_pre_mix'][l].reshape(1, d),
        'g_post_mix': w['g_post_mix'][l].reshape(1, d),
        'g_pre_mlp': w['g_pre_mlp'][l].reshape(1, d),
        'g_post_mlp': w['g_post_mlp'][l].reshape(1, d),
        'qn': jnp.tile(w['gqa_q_norm'][l], 2).reshape(1, LANES),
        'kn': jnp.tile(w['gqa_k_norm'][l], 2).reshape(1, LANES),
        'mqn': w['mla_q_norm'][l].reshape(1, MLA_Q_RANK),
        'mkvn': w['mla_kv_norm'][l].reshape(1, MLA_KV_RANK),
        'wuq': wuq.astype(BF16),
        'wukv': wukv_p.astype(BF16),
        'ones64': ones64,
        'mu': jnp.pad(w['rwkv_mu'][l], (0, C_COLS_PAD - 1056)).reshape(1, C_COLS_PAD),
        'w0': w['rwkv_w0'][l].reshape(2, 1, gw),
        'w2p': jnp.concatenate([w['rwkv_w2'][l], zl], axis=1),
        'a0': w['rwkv_a0'][l].reshape(2, 1, gw),
        'a2p': jnp.concatenate([zl, w['rwkv_a2'][l]], axis=1),
        'k_k': w['rwkv_k_k'][l].reshape(2, 1, gw),
        'k_a': w['rwkv_k_a'][l].reshape(2, 1, gw),
        'r_k': w['rwkv_r_k'][l].reshape(1, gw),
        'g2p': jnp.pad(w['rwkv_g2'][l], ((0, gw - RWKV_GATE_LORA), (0, 0))),
        'ln_w': w['rwkv_ln_w'][l].reshape(1, gw),
        'ln_b': w['rwkv_ln_b'][l].reshape(1, gw),
        'sink': w['swa_sink'][l],
        'w_out': w['w_out'][l].astype(BF16),
        'w_mlp1': w['w_mlp1'][l].astype(BF16),
        'w_mlp2': w['w_mlp2'][l].astype(BF16),
    }


def kernel(x, c, ctx, c_ctx, w_mod, b_mod, g_pre_mix, g_post_mix, g_pre_mlp, g_post_mlp, w_in, gqa_q_norm, gqa_k_norm, mla_q_norm, mla_kv_norm, mla_w_uq, mla_w_ukv, rwkv_mu, rwkv_w0, rwkv_w2, rwkv_a0, rwkv_a2, rwkv_k_k, rwkv_k_a, rwkv_r_k, rwkv_g2, rwkv_ln_w, rwkv_ln_b, swa_sink, w_out, w_mlp1, w_mlp2):
    batch, seq, d = x.shape
    ctx_len = ctx.shape[1]
    depth = w_mod.shape[0]
    assert d == D_MODEL and ctx_len == TM and seq % TM == 0 and seq % GRID_W == 0
    assert batch + 1 <= 8
    s_tot = ctx_len + seq
    n_tiles = s_tot // TM
    w = dict(w_in=w_in, g_pre_mix=g_pre_mix, g_post_mix=g_post_mix, g_pre_mlp=g_pre_mlp,
             g_post_mlp=g_post_mlp, gqa_q_norm=gqa_q_norm, gqa_k_norm=gqa_k_norm, mla_q_norm=mla_q_norm,
             mla_kv_norm=mla_kv_norm, mla_w_uq=mla_w_uq, mla_w_ukv=mla_w_ukv, rwkv_mu=rwkv_mu,
             rwkv_w0=rwkv_w0, rwkv_w2=rwkv_w2, rwkv_a0=rwkv_a0, rwkv_a2=rwkv_a2, rwkv_k_k=rwkv_k_k,
             rwkv_k_a=rwkv_k_a, rwkv_r_k=rwkv_r_k, rwkv_g2=rwkv_g2, rwkv_ln_w=rwkv_ln_w,
             rwkv_ln_b=rwkv_ln_b, swa_sink=swa_sink, w_out=w_out, w_mlp1=w_mlp1, w_mlp2=w_mlp2)

    ct = jnp.concatenate([c, c_ctx[None, :], jnp.zeros((8 - batch - 1, d), F32)], axis=0).T
    mods = _modulation(ct, w_mod, b_mod, batch + 1)

    def mod_row(i):
        return jnp.where(i % n_tiles == 0, batch, i // n_tiles)

    tabs = _rope_tables(seq, ctx_len)
    geo = dict(batch=batch, s_tot=s_tot, ctx_len=ctx_len)
    n = batch * s_tot
    n_lat = n_tiles - 1
    xc, xl = ctx.reshape(batch * ctx_len, d), x.reshape(batch * seq, d)
    xc_map = lambda b, j: (b, 0)
    xl_map = lambda b, j: (b * n_lat + jnp.maximum(j - 1, 0), 0)
    for l in range(depth):
        p = _layer_params(l, w)
        mod3 = mods[l].reshape(8, 1, 6 * d)
        flat = lambda m: (lambda i: m(i // n_tiles, i % n_tiles))
        qa, ka, va, qb, kb, vb, qd, kd, vd, zc = _inproj_prep(
            xc, xl, flat(xc_map), flat(xl_map), mod3, tabs, p, mod_row, n, n_tiles)
        oa = _flash(qa, ka, va, shared_kv=True, **geo)
        ob = _flash(qb, kb, vb, shared_kv=False, **geo)
        od = _window_attn(p['sink'], qd, kd, vd, **geo)
        lw, rkd, ra, rb, rr, rv, rg, bonus = _rwkv_prep(zc, p, n_tiles)
        yf = _rwkv_scan(lw, rkd, ra, rb, rr, rv, d=0, batch=batch, n_tiles=n_tiles)
        yr = _rwkv_scan(lw, rkd, ra, rb, rr, rv, d=1, batch=batch, n_tiles=n_tiles)
        xs = _out_mlp(xc, xl, xc_map, xl_map, oa, ob, od, yf, yr, bonus, rg, mod3, p,
                      batch=batch, n_tiles=n_tiles, skip_ctx=l == depth - 1)
        xc = xl = xs
        xc_map = lambda b, j: (b * n_tiles, 0)
        xl_map = lambda b, j: (b * n_tiles + j, 0)
    return xs.reshape(batch, seq, d)
```

```python
import functools
import math

import jax
import jax.numpy as jnp
from jax import lax
from jax.experimental import pallas as pl
from jax.experimental.pallas import tpu as pltpu

F32 = jnp.float32
BF16 = jnp.bfloat16

D_MODEL = 1024
GRID_W = 64
HEAD_DIM = 64
GROUP_WIDTH = 256
GROUP_HEADS = 4
ROPE_THETA = 10000.0
NORM_EPS = 1e-6
NEG_INF = -1e30
MLA_Q_RANK = 256
MLA_KV_RANK = 128
MLA_NOPE_DIM = 64
MLA_ROPE_DIM = 32
MLA_QK_DIM = MLA_NOPE_DIM + MLA_ROPE_DIM
RWKV_DECAY_LORA = 64
RWKV_ICLR_LORA = 64
RWKV_GATE_LORA = 160
RWKV_GN_EPS = 64e-5
WINDOW = 128
D_FF = 4 * D_MODEL

LOG2E = math.log2(math.e)
LANES = 128
TM = 256
ONES_ROWS = 8
LOOKAHEAD = 2
WINDOW_BLOCKS = 11
CHUNK = 64
SUB = 16
SEC_A = 0
SEC_B = 512
SEC_D = 1024
ABD_COLS = 1536
C_COLS_PAD = 1152
VMEM_LIMIT = 56 * 1024 * 1024


def _cparams(sem):
    return pltpu.CompilerParams(dimension_semantics=sem, vmem_limit_bytes=VMEM_LIMIT)


def _dot(a, b):
    return jnp.dot(a, b, preferred_element_type=F32)


def _dot_nt(a, b):
    return lax.dot_general(a, b, (((1,), (1,)), ((), ())), preferred_element_type=F32)


def _dot_tn(a, b):
    return lax.dot_general(a, b, (((0,), (0,)), ((), ())), preferred_element_type=F32)


def _dot16(a, b):
    return _dot(a.astype(BF16), b.astype(BF16))


def _split2(x):
    hi = x.astype(BF16)
    lo = (x - hi.astype(F32)).astype(BF16)
    return hi, lo


def _mm_exact_rhs(a, b_bf16):
    hi, lo = _split2(a)
    return _dot(hi, b_bf16) + _dot(lo, b_bf16)


def _mm_exact_lhs(a_bf16, b):
    hi, lo = _split2(b)
    return _dot(a_bf16, hi) + _dot(a_bf16, lo)


def _mm3(a, b):
    ah, al = _split2(a)
    bh, bl = _split2(b)
    return _dot(ah, bh) + (_dot(ah, bl) + _dot(al, bh))


def _sigmoid(x):
    return 1.0 / (1.0 + jnp.exp(-x))


def _rms(x, eps):
    return x * lax.rsqrt(jnp.mean(x * x, axis=-1, keepdims=True) + eps)


def _mod_kernel(ct_ref, w_ref, b_ref, o_ref, *, n_rows):
    ct = ct_ref[...]
    st = ct * _sigmoid(ct)
    w = w_ref[...]
    rows = [jnp.sum(st[:, r:r + 1] * w, axis=0, keepdims=True) for r in range(n_rows)]
    rows.append(jnp.zeros((8 - n_rows, w.shape[1]), F32))
    o_ref[...] = jnp.concatenate(rows, axis=0) + b_ref[...]


def _modulation(ct, w_mod, b_mod, n_rows):
    depth, d, n6 = w_mod.shape
    tn = 1536
    return pl.pallas_call(
        functools.partial(_mod_kernel, n_rows=n_rows),
        grid=(depth, n6 // tn),
        in_specs=[pl.BlockSpec((d, 8), lambda l, j: (0, 0)),
                  pl.BlockSpec((None, d, tn), lambda l, j: (l, 0, j)),
                  pl.BlockSpec((None, 1, tn), lambda l, j: (l, 0, j))],
        out_specs=pl.BlockSpec((None, 8, tn), lambda l, j: (l, 0, j)),
        out_shape=jax.ShapeDtypeStruct((depth, 8, n6), F32),
        compiler_params=_cparams(("parallel", "parallel")),
        name="modulation",
    )(ct, w_mod, b_mod.reshape(depth, 1, n6))


def _rope_slab(x, cos, sin_signed, shift):
    left = pltpu.roll(x, LANES - shift, axis=1)
    right = pltpu.roll(x, shift, axis=1)
    lane = lax.broadcasted_iota(jnp.int32, x.shape, 1)
    first = ((lane // shift) % 2) == 0
    return x * cos + jnp.where(first, left, right) * sin_signed


def _head_norm_slab(x, gain, ones):
    ss = _mm_exact_rhs(x * x, ones)
    return x * lax.rsqrt(ss * (1.0 / HEAD_DIM) + NORM_EPS) * gain


def _inproj_prep_kernel(xc_ref, xl_ref, mod_ref, g_ref, w_ref,
                        cs_ref, sn_ref, csm_ref, snm_ref, qn_ref, kn_ref, mqn_ref, mkvn_ref,
                        wuq_ref, wukv_ref, ones_ref,
                        qa_ref, ka_ref, va_ref, qb_ref, kb_ref, vb_ref, qd_ref, kd_ref, vd_ref, zc_ref,
                        z_ref, *, n_tiles):
    d = D_MODEL
    mod = mod_ref[...]
    is_ctx = pl.program_id(0) % n_tiles == 0
    x = jnp.where(is_ctx, xc_ref[...], xl_ref[...])
    h = _rms(x, NORM_EPS) * g_ref[...] * (1.0 + mod[:, d:2 * d]) + mod[:, 0:d]
    z = _dot(h.astype(BF16), w_ref[...])
    z_ref[...] = z[:, :ABD_COLS]
    zc_ref[...] = z[:, ABD_COLS:]
    cs, sn = cs_ref[...], sn_ref[...]
    csm, snm = csm_ref[...], snm_ref[...]
    ones = ones_ref[...]
    hd = HEAD_DIM
    q_scale = HEAD_DIM ** -0.5

    def put_heads(ref, base, slab):
        ref[base] = slab[:, :hd].astype(BF16)
        ref[base + 1] = slab[:, hd:].astype(BF16)

    def put_heads_t(ref, base, slab, ones_rows=ONES_ROWS):
        st = slab.T.astype(BF16)
        if ones_rows:
            one = jnp.ones((ones_rows, st.shape[1]), BF16)
            ref[base] = jnp.concatenate([st[:hd], one], axis=0)
            ref[base + 1] = jnp.concatenate([st[hd:], one], axis=0)
        else:
            ref[base] = st[:hd]
            ref[base + 1] = st[hd:]

    for s in range(2):
        x = z_ref[:, SEC_A + LANES * s:SEC_A + LANES * (s + 1)]
        x = _rope_slab(_head_norm_slab(x, qn_ref[...], ones), cs, sn, 16) * (q_scale * LOG2E)
        put_heads_t(qa_ref, 2 * s, x, 0)
    x = z_ref[:, SEC_A + 256:SEC_A + 384]
    put_heads(ka_ref, 0, _rope_slab(_head_norm_slab(x, kn_ref[...], ones), cs, sn, 16))
    put_heads_t(va_ref, 0, z_ref[:, SEC_A + 384:SEC_A + 512])

    for s in range(2):
        x = z_ref[:, SEC_D + LANES * s:SEC_D + LANES * (s + 1)]
        put_heads(qd_ref, 2 * s, _rope_slab(x, cs, sn, 16) * q_scale)
    put_heads(kd_ref, 0, _rope_slab(z_ref[:, SEC_D + 256:SEC_D + 384], cs, sn, 16))
    put_heads(vd_ref, 0, z_ref[:, SEC_D + 384:SEC_D + 512])

    cq = _rms(z_ref[:, SEC_B:SEC_B + MLA_Q_RANK], NORM_EPS) * mqn_ref[...]
    q = _dot(cq.astype(BF16), wuq_ref[...])
    ckv = _rms(z_ref[:, SEC_B + 256:SEC_B + 384], NORM_EPS) * mkvn_ref[...]
    kv = _dot(ckv.astype(BF16), wukv_ref[...])
    kr = _rope_slab(z_ref[:, SEC_B + 384:SEC_B + 512], csm, snm, 8)
    b_scale = MLA_QK_DIM ** -0.5 * LOG2E
    for h in range(GROUP_HEADS):
        qh = q[:, LANES * h:LANES * (h + 1)]
        qb_ref[h] = (_rope_slab(qh, csm, snm, 8) * b_scale).T.astype(BF16)
        kb_ref[h] = (kv[:, LANES * h:LANES * (h + 1)] + kr).astype(BF16)
    for s in range(2):
        put_heads_t(vb_ref, 2 * s, kv[:, 4 * LANES + LANES * s:4 * LANES + LANES * (s + 1)])


def _inproj_prep(xc, xl, xc_map, xl_map, mod3, tabs, p, mod_row, n, n_tiles):
    cs, sn, csm, snm = tabs
    cols = p['w_in_p'].shape[1]

    def tab_spec():
        return pl.BlockSpec((TM, LANES), lambda i: (i % n_tiles, 0))

    def row_spec(w):
        return pl.BlockSpec((1, w), lambda i: (0, 0))

    def heads_out(nh, w):
        return (pl.BlockSpec((nh, TM, w), lambda i: (0, i, 0)),
                jax.ShapeDtypeStruct((nh, n, w), BF16))

    def heads_out_t(nh, rows=HEAD_DIM + ONES_ROWS):
        return (pl.BlockSpec((nh, None, rows, TM), lambda i: (0, i, 0, 0)),
                jax.ShapeDtypeStruct((nh, n // TM, rows, TM), BF16))

    outs = [heads_out_t(4, HEAD_DIM), heads_out(2, 64), heads_out_t(2),
            heads_out_t(4, LANES), heads_out(4, 128), heads_out_t(4),
            heads_out(4, 64), heads_out(2, 64), heads_out(2, 64),
            (pl.BlockSpec((TM, C_COLS_PAD), lambda i: (i, 0)), jax.ShapeDtypeStruct((n, C_COLS_PAD), F32))]
    return pl.pallas_call(
        functools.partial(_inproj_prep_kernel, n_tiles=n_tiles),
        grid=(n // TM,),
        in_specs=[pl.BlockSpec((TM, D_MODEL), xc_map), pl.BlockSpec((TM, D_MODEL), xl_map),
                  pl.BlockSpec((None, 1, 6 * D_MODEL), lambda i: (mod_row(i), 0, 0)),
                  row_spec(D_MODEL),
                  pl.BlockSpec((D_MODEL, cols), lambda i: (0, 0)),
                  tab_spec(), tab_spec(), tab_spec(), tab_spec(),
                  row_spec(LANES), row_spec(LANES), row_spec(MLA_Q_RANK), row_spec(MLA_KV_RANK),
                  pl.BlockSpec((MLA_Q_RANK, 512), lambda i: (0, 0)),
                  pl.BlockSpec((MLA_KV_RANK, 768), lambda i: (0, 0)),
                  pl.BlockSpec((LANES, LANES), lambda i: (0, 0))],
        out_specs=[o[0] for o in outs],
        out_shape=[o[1] for o in outs],
        scratch_shapes=[pltpu.VMEM((TM, ABD_COLS), F32)],
        compiler_params=_cparams(("parallel",)),
        name="inproj_prep",
    )(xc, xl, mod3, p['g_pre_mix'], p['w_in_p'], cs, sn, csm, snm,
      p['qn'], p['kn'], p['mqn'], p['mkvn'], p['wuq'], p['wukv'], p['ones64'])


def _flash_kernel(q_ref, k_ref, vt_ref, o_ref, *, shared_kv, tq, tk, ctx_len, s_tot):
    qi = pl.program_id(2)
    dv = vt_ref.shape[-2] - ONES_ROWS
    if shared_kv:
        streams = [(jnp.concatenate([q_ref[0], q_ref[1]], axis=1), 0)]
    else:
        streams = [(q_ref[0], 0), (q_ref[1], 1)]

    def scores(c):
        out = []
        for q, kv in streams:
            k = k_ref[kv, pl.ds(pl.multiple_of(c * tk, tk), tk), :]
            out.append(_dot(k, q))
        return out

    def softmax_pv(c, sts, carries):
        new = []
        for st, (_, kv), (m, acc) in zip(sts, streams, carries):
            m_new = jnp.maximum(m, jnp.max(st, axis=0, keepdims=True))
            pt = jnp.exp2(st - m_new).astype(BF16)
            acc = jnp.exp2(m - m_new) * acc + _dot(vt_ref[kv, c], pt)
            new.append((m_new, acc))
        return tuple(new)

    def run(n_chunks):
        unroll = next(u for u in (33, 11, 3, 2, 1) if n_chunks % u == 0)
        groups = n_chunks // unroll
        init = tuple((jnp.full((1, q.shape[1]), NEG_INF, F32), jnp.zeros((dv + ONES_ROWS, q.shape[1]), F32))
                     for q, _ in streams)

        def body(g, carries):
            queue = [scores(g * unroll + u) for u in range(min(LOOKAHEAD, unroll))]
            for u in range(unroll):
                if u + LOOKAHEAD < unroll:
                    queue.append(scores(g * unroll + u + LOOKAHEAD))
                carries = softmax_pv(g * unroll + u, queue.pop(0), carries)
            return carries

        carries = body(0, init) if groups == 1 else lax.fori_loop(0, groups, body, init)
        outs = [acc[:dv] / acc[dv:dv + 1] for _, acc in carries]
        if shared_kv:
            outs = [outs[0][:, :tq], outs[0][:, tq:]]
        o_ref[...] = jnp.concatenate(outs, axis=0).T.astype(o_ref.dtype)

    is_ctx = qi * tq < ctx_len

    @pl.when(is_ctx)
    def _():
        run(ctx_len // tk)

    @pl.when(jnp.logical_not(is_ctx))
    def _():
        run(s_tot // tk)


def _flash(qt, k, vt, *, shared_kv, batch, s_tot, ctx_len):
    nh, _, dk, _ = qt.shape
    n = k.shape[1]
    dv = vt.shape[-2] - ONES_ROWS
    nkv = 1 if shared_kv else 2
    tq = tk = TM
    nq = s_tot // tq
    kern = functools.partial(_flash_kernel, shared_kv=shared_kv, tq=tq, tk=tk,
                             ctx_len=ctx_len, s_tot=s_tot)
    return pl.pallas_call(
        kern,
        grid=(batch, nh // 2, nq),
        in_specs=[pl.BlockSpec((2, None, dk, tq), lambda b, p, i: (p, b * nq + i, 0, 0)),
                  pl.BlockSpec((nkv, s_tot, dk), lambda b, p, i: (p, b, 0)),
                  pl.BlockSpec((nkv, s_tot // tk, dv + ONES_ROWS, tk), lambda b, p, i: (p, b, 0, 0))],
        out_specs=pl.BlockSpec((tq, 2 * dv), lambda b, p, i: (b * nq + i, p)),
        out_shape=jax.ShapeDtypeStruct((n, nh * dv), BF16),
        compiler_params=_cparams(("parallel", "parallel", "arbitrary")),
        name="flash_shared" if shared_kv else "flash_split",
    )(qt, k, vt)


def _window_kernel(sink_ref, q_ref, k_ref, v_ref, o_ref, *, ctx_len, s_tot, n_blocks):
    w = WINDOW
    g = pl.program_id(1)
    first_lat = ctx_len // w
    last_blk = s_tot // w - 1
    r = lax.broadcasted_iota(jnp.int32, (2 * w, 3 * w), 0) % w
    c = lax.broadcasted_iota(jnp.int32, (2 * w, 3 * w), 1)
    row = lax.broadcasted_iota(jnp.int32, (2 * w, 1), 0)
    sk = jnp.where(row < w, sink_ref[2 * g], sink_ref[2 * g + 1])
    k_ctx = k_ref[0:ctx_len, :]
    v_ctx = v_ref[0:ctx_len, :]

    def rows(ref, blk):
        return ref[pl.ds(pl.multiple_of(blk * w, w), w), :]

    for i in range(n_blocks):
        jb = pl.program_id(2) * n_blocks + i
        is_lat = jb >= first_lat
        pb = jnp.clip(jb - 1, 0, last_blk)
        nb = jnp.clip(jb + 1, 0, last_blk)
        q = q_ref[:, w * i:w * (i + 1), :].reshape(2 * w, HEAD_DIM)
        kw = jnp.concatenate([rows(k_ref, pb), rows(k_ref, jb), rows(k_ref, nb)], axis=0)
        vw = jnp.concatenate([rows(v_ref, pb), rows(v_ref, jb), rows(v_ref, nb)], axis=0)
        s_w = _dot_nt(q, kw)
        s_c = _dot_nt(q, k_ctx)
        lo = jnp.where(is_lat, jnp.where(jb - 1 >= first_lat, 0, w), 3 * w)
        hi = jnp.where(jb + 1 <= last_blk, 3 * w - 1, 2 * w - 1)
        valid = jnp.logical_and(c >= jnp.maximum(r, lo), c <= jnp.minimum(r + 2 * w, hi))
        s_w = jnp.where(valid, s_w, NEG_INF)
        m = jnp.maximum(jnp.maximum(jnp.max(s_w, axis=-1, keepdims=True),
                                    jnp.max(s_c, axis=-1, keepdims=True)), sk)
        p_w = jnp.exp(s_w - m)
        p_c = jnp.exp(s_c - m)
        denom = jnp.exp(sk - m) + jnp.sum(p_w, axis=-1, keepdims=True) + jnp.sum(p_c, axis=-1, keepdims=True)
        o = (_dot(p_w.astype(BF16), vw) + _dot(p_c.astype(BF16), v_ctx)) / denom
        o_ref[w * i:w * (i + 1), :] = jnp.concatenate([o[:w], o[w:]], axis=-1).astype(o_ref.dtype)


def _window_attn(sink, q, k, v, *, batch, s_tot, ctx_len):
    nh, n, dk = q.shape
    nblk = s_tot // WINDOW
    wb = next(u for u in (WINDOW_BLOCKS, 6, 3, 2, 1) if nblk % u == 0)
    rows = WINDOW * wb
    nstep = nblk // wb
    kern = functools.partial(_window_kernel, ctx_len=ctx_len, s_tot=s_tot, n_blocks=wb)
    return pl.pallas_call(
        kern,
        grid=(batch, nh // 2, nstep),
        in_specs=[pl.BlockSpec(memory_space=pltpu.SMEM),
                  pl.BlockSpec((2, rows, dk), lambda b, g, j: (g, b * nstep + j, 0)),
                  pl.BlockSpec((None, s_tot, dk), lambda b, g, j: (g, b, 0)),
                  pl.BlockSpec((None, s_tot, dk), lambda b, g, j: (g, b, 0))],
        out_specs=pl.BlockSpec((rows, 2 * dk), lambda b, g, j: (b * nstep + j, g)),
        out_shape=jax.ShapeDtypeStruct((n, nh * dk), BF16),
        compiler_params=_cparams(("parallel", "parallel", "arbitrary")),
        name="window_attn",
    )(sink, q, k, v)


def _rwkv_prep_kernel(z_ref, zp_ref, zn_ref, mu_ref, w0_ref, w2_ref, a0_ref, a2_ref, kk_ref, ka_ref,
                      rk_ref, g2_ref, ones_ref,
                      lw_ref, kd_ref, a_ref, b_ref, r_ref, v_ref, g_ref, bonus_ref, *, n_tiles):
    j = pl.program_id(0) % n_tiles
    z = z_ref[...]
    rows = lax.broadcasted_iota(jnp.int32, z.shape, 0)
    prev_row = jnp.where(j <= 1, 0.0, zp_ref[7:8, :])
    next_row = jnp.where(jnp.logical_or(j == 0, j == n_tiles - 1), 0.0, zn_ref[0:1, :])
    z_prev = jnp.where(rows == 0, prev_row, pltpu.roll(z, 1, axis=0))
    z_next = jnp.where(rows == TM - 1, next_row, pltpu.roll(z, TM - 1, axis=0))
    zs = z + mu_ref[...] * (0.5 * (z_prev + z_next) - z)

    gw = GROUP_WIDTH
    r, k, v = zs[:, 0:gw], zs[:, gw:2 * gw], zs[:, 2 * gw:3 * gw]
    lora = zs[:, 3 * gw:3 * gw + LANES]
    gl = zs[:, 3 * gw + LANES:3 * gw + LANES + gw]
    wt = jnp.tanh(lora)
    ones = ones_ref[...]

    def seg_sum(x):
        return jnp.concatenate([_mm_exact_rhs(x[:, :LANES], ones), _mm_exact_rhs(x[:, LANES:], ones)], axis=-1)

    r_ref[...] = r
    v_ref[...] = v
    g_ref[...] = _mm3(_sigmoid(gl), g2_ref[...])
    kd_sum = None
    for d in range(2):
        u = w0_ref[d] + _mm3(wt, w2_ref[d])
        soft = jnp.maximum(-u, 0.0) + jnp.log(1.0 + jnp.exp(-jnp.abs(u)))
        lw_ref[d] = -jnp.exp(-soft - 0.5)
        gate = _sigmoid(a0_ref[d] + _mm3(lora, a2_ref[d]))
        kk = k * kk_ref[d]
        kk = kk / jnp.maximum(jnp.sqrt(seg_sum(kk * kk)), 1e-12)
        kd = k * (1.0 + (gate - 1.0) * ka_ref[d])
        kd_ref[d] = kd
        a_ref[d] = -kk
        b_ref[d] = kk * gate
        kd_sum = kd if kd_sum is None else kd_sum + kd
    bonus_ref[...] = seg_sum(r * kd_sum * rk_ref[...]) * v


def _rwkv_prep(zc, p, n_tiles):
    n = zc.shape[0]
    gw = GROUP_WIDTH
    nb8 = n // 8

    def full(shape):
        nd = len(shape)
        return pl.BlockSpec(shape, lambda i: (0,) * nd)

    dir_out = (pl.BlockSpec((2, TM, gw), lambda i: (0, i, 0)), jax.ShapeDtypeStruct((2, n, gw), F32))
    one_out = (pl.BlockSpec((TM, gw), lambda i: (i, 0)), jax.ShapeDtypeStruct((n, gw), F32))
    outs = [dir_out] * 4 + [one_out] * 4
    return pl.pallas_call(
        functools.partial(_rwkv_prep_kernel, n_tiles=n_tiles),
        grid=(n // TM,),
        in_specs=[pl.BlockSpec((TM, C_COLS_PAD), lambda i: (i, 0)),
                  pl.BlockSpec((8, C_COLS_PAD), lambda i: (jnp.maximum(i * (TM // 8) - 1, 0), 0)),
                  pl.BlockSpec((8, C_COLS_PAD), lambda i: (jnp.minimum((i + 1) * (TM // 8), nb8 - 1), 0)),
                  full((1, C_COLS_PAD)),
                  full((2, 1, gw)), full((2, LANES, gw)), full((2, 1, gw)), full((2, LANES, gw)),
                  full((2, 1, gw)), full((2, 1, gw)), full((1, gw)), full((gw, gw)),
                  full((LANES, LANES))],
        out_specs=[o[0] for o in outs],
        out_shape=[o[1] for o in outs],
        compiler_params=_cparams(("parallel",)),
        name="rwkv_prep",
    )(zc, zc, zc, p['mu'], p['w0'], p['w2p'], p['a0'], p['a2p'], p['k_k'], p['k_a'], p['r_k'],
      p['g2p'], p['ones64'])


def _rwkv_scan_kernel(lw_ref, k_ref, a_ref, b_ref, r_ref, v_ref, y_ref, s_ref, *, reverse):
    @pl.when(pl.program_id(0) == 0)
    def _():
        s_ref[...] = jnp.zeros_like(s_ref)

    batch = lw_ref.shape[0]
    n_chunks = TM // CHUNK
    hd = HEAD_DIM
    row = lax.broadcasted_iota(jnp.int32, (TM, TM), 0)
    col = lax.broadcasted_iota(jnp.int32, (TM, TM), 1)
    same = (row // CHUNK) == (col // CHUNK)
    before = (col > row) if reverse else (col < row)
    m_strict = jnp.logical_and(same, before)
    m_incl = jnp.logical_and(same, jnp.logical_or(before, row == col))
    m_incl16 = m_incl.astype(BF16)
    eye = (row == col).astype(F32)
    assert CHUNK // SUB == 4
    same_sub = (row // SUB) == (col // SUB)

    items = [(b, h) for b in range(batch) for h in range(GROUP_HEADS)]
    idx = range(len(items))
    e_pos, at_all, rt_all, bt_all, kt_all, v_all = [], [], [], [], [], []
    for b in range(batch):
        lw = lw_ref[b]
        cum = _mm_exact_lhs(m_incl16, lw)
        e_pos.append(jnp.exp(cum))
        e_neg = jnp.exp(-cum)
        at_all.append(a_ref[b] * jnp.exp(cum - lw))
        rt_all.append(r_ref[b] * e_pos[b])
        bt_all.append(b_ref[b] * e_neg)
        kt_all.append(k_ref[b] * e_neg)
        v_all.append(v_ref[b])

    def head(arrs, b, h):
        return arrs[b][:, hd * h:hd * (h + 1)]

    at = [head(at_all, b, h) for b, h in items]
    rt = [head(rt_all, b, h) for b, h in items]
    bt = [head(bt_all, b, h) for b, h in items]
    at16 = [x.astype(BF16) for x in at]
    rt16 = [x.astype(BF16) for x in rt]
    bt16 = [x.astype(BF16) for x in bt]
    kt16 = [head(kt_all, b, h).astype(BF16) for b, h in items]
    v16 = [head(v_all, b, h).astype(BF16) for b, h in items]
    ab = [jnp.where(m_strict, _dot_nt(at16[i], bt16[i]), 0.0) for i in idx]
    ak16 = [jnp.where(m_strict, _dot_nt(at16[i], kt16[i]), 0.0).astype(BF16) for i in idx]
    rk16 = [jnp.where(m_incl, _dot_nt(rt16[i], kt16[i]), 0.0).astype(BF16) for i in idx]
    n_c = TM // CHUNK

    def wide(m):
        out = m[:CHUNK]
        for c in range(1, n_c):
            out = out + m[CHUNK * c:CHUNK * (c + 1)]
        return out

    def bdiag16(wm):
        return jnp.where(same, jnp.concatenate([wm] * n_c, axis=0), 0.0).astype(BF16)

    def wdot(wl, bd16):
        return _dot(wl.astype(BF16), bd16)

    eye_w = wide(eye)
    ld = [jnp.where(same_sub, x, 0.0) for x in ab]
    lo16 = [jnp.where(same_sub, 0.0, x).astype(BF16) for x in ab]
    pw_w = [wide(x) for x in ld]
    pw_bd = [x.astype(BF16) for x in ld]
    td_w = [eye_w + x for x in pw_w]
    for _ in range(int(math.log2(SUB)) - 1):
        pw_w = [wdot(x, y) for x, y in zip(pw_w, pw_bd)]
        pw_bd = [bdiag16(x) for x in pw_w]
        td_w = [t + wdot(t, y) for t, y in zip(td_w, pw_bd)]
    m1_w = [wdot(t, x) for t, x in zip(td_w, lo16)]
    m1_bd = [bdiag16(x) for x in m1_w]
    m2_w = [wdot(x, y) for x, y in zip(m1_w, m1_bd)]
    m3_w = [wdot(x, bdiag16(y)) for x, y in zip(m1_w, m2_w)]
    nn_w = [eye_w + a1 + a2 + a3 for a1, a2, a3 in zip(m1_w, m2_w, m3_w)]
    akv = [_dot(ak16[i], v16[i]) for i in idx]

    rowc = lax.broadcasted_iota(jnp.int32, (TM, n_c * 2 * hd), 0) // CHUNK
    colc = lax.broadcasted_iota(jnp.int32, (TM, n_c * 2 * hd), 1) // (2 * hd)
    same_x = rowc == colc

    def spread16(xr):
        return jnp.where(same_x, jnp.concatenate([xr] * n_c, axis=1), 0.0).astype(BF16)

    def unwide(xw):
        return jnp.concatenate([xw[:, 2 * hd * c:2 * hd * (c + 1)] for c in range(n_c)], axis=0)

    tx = [wdot(td_w[i], spread16(jnp.concatenate([at[i], akv[i]], axis=-1))) for i in idx]
    wu = [unwide(wdot(nn_w[i], spread16(unwide(tx[i])))) for i in idx]
    wa = [x[:, :hd] for x in wu]
    u016 = [x[:, hd:].astype(BF16) for x in wu]
    rb_w = [wide(jnp.where(m_incl, _dot_nt(rt16[i], bt16[i]), 0.0)) for i in idx]
    rbwu = [unwide(wdot(rb_w[i], spread16(wu[i]))) for i in idx]
    yr16 = [(rt[i] + rbwu[i][:, :hd]).astype(BF16) for i in idx]
    y0 = [rbwu[i][:, hd:] + _dot(rk16[i], v16[i]) for i in idx]

    order = range(n_chunks - 1, -1, -1) if reverse else range(n_chunks)
    eye_h = eye[:hd, :hd]
    trans = {}
    for c in order:
        rs = slice(CHUNK * c, CHUNK * (c + 1))
        last = CHUNK * c if reverse else CHUNK * (c + 1) - 1
        for i, (b, h) in enumerate(items):
            decay = e_pos[b][last:last + 1, hd * h:hd * (h + 1)]
            g = (eye_h + _mm3tn(wa[i][rs], bt[i][rs])) * decay
            hh = (_dot_tn(u016[i][rs], bt16[i][rs]) + _dot_tn(v16[i][rs], kt16[i][rs])) * decay
            trans[c, i] = (g, hh)
    for c in order:
        rs = slice(CHUNK * c, CHUNK * (c + 1))
        s0 = [s_ref[b, h] for b, h in items]
        for i, (b, h) in enumerate(items):
            s_ref[b, h] = _mm3(s0[i], trans[c, i][0]) + trans[c, i][1]
        for b in range(batch):
            y_ref[b, rs, :] = jnp.concatenate(
                [_dot_nt(yr16[i][rs], s0[i].astype(BF16)) + y0[i][rs] for i, (bb, _) in enumerate(items) if bb == b],
                axis=-1)


def _mm3tn(a, b):
    ah, al = _split2(a)
    bh, bl = _split2(b)
    return _dot_tn(ah, bh) + (_dot_tn(ah, bl) + _dot_tn(al, bh))


def _rwkv_scan(lw, kd, a, b, r, v, *, d, batch, n_tiles):
    n, gw = r.shape
    s_tot = n // batch
    reverse = d == 1

    def tile(s):
        if reverse:
            return jnp.where(s == 0, 0, n_tiles - s)
        return s

    dspec = pl.BlockSpec((None, batch, TM, gw), lambda s: (d, 0, tile(s), 0))
    spec = pl.BlockSpec((batch, TM, gw), lambda s: (0, tile(s), 0))
    per_dir = [x.reshape(2, batch, s_tot, gw) for x in (lw, kd, a, b)]
    shared = [x.reshape(batch, s_tot, gw) for x in (r, v)]
    y = pl.pallas_call(
        functools.partial(_rwkv_scan_kernel, reverse=reverse),
        grid=(n_tiles,),
        in_specs=[dspec, dspec, dspec, dspec, spec, spec],
        out_specs=spec,
        out_shape=jax.ShapeDtypeStruct((batch, s_tot, gw), F32),
        scratch_shapes=[pltpu.VMEM((batch, GROUP_HEADS, HEAD_DIM, HEAD_DIM), F32)],
        compiler_params=_cparams(("arbitrary",)),
        name="rwkv_scan_rev" if reverse else "rwkv_scan_fwd",
    )(*per_dir, *shared)
    return y.reshape(n, gw)


def _out_mlp_kernel(xc_ref, xl_ref, oa_ref, ob_ref, od_ref, yf_ref, yr_ref, bonus_ref, g_ref,
                    lnw_ref, lnb_ref, ones_ref, mod_ref, gpost_ref, gpre_ref, gpm_ref,
                    wo_ref, w1_ref, w2_ref, xo_ref, *, ctx_first):
    d = D_MODEL
    gw = GROUP_WIDTH
    mod = mod_ref[...]
    ones = ones_ref[...]

    def seg_mean(t):
        sm = jnp.concatenate([_mm_exact_rhs(t[:, :LANES], ones), _mm_exact_rhs(t[:, LANES:], ones)], axis=-1)
        return sm * (1.0 / HEAD_DIM)

    yy = yf_ref[...] + yr_ref[...]
    dlt = yy - seg_mean(yy)
    yn = dlt * lax.rsqrt(seg_mean(dlt * dlt) + RWKV_GN_EPS) * lnw_ref[...] + lnb_ref[...]
    oc = ((yn + bonus_ref[...]) * g_ref[...]).astype(BF16)

    y = _dot(oa_ref[...], wo_ref[0:gw, :])
    y += _dot(ob_ref[...], wo_ref[gw:2 * gw, :])
    y += _dot(oc, wo_ref[2 * gw:3 * gw, :])
    y += _dot(od_ref[...], wo_ref[3 * gw:4 * gw, :])
    x = jnp.where(jnp.logical_and(ctx_first, pl.program_id(1) == 0), xc_ref[...], xl_ref[...])
    x1 = x + mod[:, 2 * d:3 * d] * (_rms(y, NORM_EPS) * gpost_ref[...])
    h = _rms(x1, NORM_EPS) * gpre_ref[...] * (1.0 + mod[:, 4 * d:5 * d]) + mod[:, 3 * d:4 * d]
    u = jnp.maximum(_dot(h.astype(BF16), w1_ref[...]), 0.0)
    zz = _dot((u * u).astype(BF16), w2_ref[...])
    xo_ref[...] = x1 + mod[:, 5 * d:6 * d] * (_rms(zz, NORM_EPS) * gpm_ref[...])


def _out_mlp(xc, xl, xc_map, xl_map, oa, ob, od, yf, yr, bonus, rg, mod3, p, *, batch, n_tiles, skip_ctx):
    d = D_MODEL
    gw = GROUP_WIDTH
    off = 1 if skip_ctx else 0
    nt = n_tiles - off

    def tile(b, j):
        return (b * n_tiles + off + j, 0)

    def mrow(b, j):
        return (jnp.where(off + j == 0, batch, b), 0, 0)

    ospec = pl.BlockSpec((TM, gw), tile)
    row = pl.BlockSpec((1, d), lambda b, j: (0, 0))
    grow = pl.BlockSpec((1, gw), lambda b, j: (0, 0))

    def wspec(shape):
        return pl.BlockSpec(shape, lambda b, j: (0, 0), pipeline_mode=pl.Buffered(1))

    return pl.pallas_call(
        functools.partial(_out_mlp_kernel, ctx_first=not skip_ctx),
        grid=(batch, nt),
        in_specs=[pl.BlockSpec((TM, d), lambda b, j: xc_map(b, off + j)),
                  pl.BlockSpec((TM, d), lambda b, j: xl_map(b, off + j)),
                  ospec, ospec, ospec, ospec, ospec, ospec, ospec,
                  grow, grow, pl.BlockSpec((LANES, LANES), lambda b, j: (0, 0)),
                  pl.BlockSpec((None, 1, 6 * d), mrow),
                  row, row, row,
                  wspec((d, d)), wspec((d, D_FF)), wspec((D_FF, d))],
        out_specs=pl.BlockSpec((TM, d), lambda b, j: (b * nt + j, 0)),
        out_shape=jax.ShapeDtypeStruct((batch * nt * TM, d), F32),
        compiler_params=_cparams(("parallel", "parallel")),
        name="out_mlp",
    )(xc, xl, oa, ob, od, yf, yr, bonus, rg, p['ln_w'], p['ln_b'], p['ones64'], mod3,
      p['g_post_mix'], p['g_pre_mlp'], p['g_post_mlp'], p['w_out'], p['w_mlp1'], p['w_mlp2'])


def _rope_tables(seq, ctx_len):
    t = jnp.arange(seq, dtype=jnp.int32)
    row = (t // GRID_W).astype(F32)
    col = (t % GRID_W).astype(F32)

    def tables(rot_dim, lane_dim):
        n = rot_dim // 4
        inv = ROPE_THETA ** (-jnp.arange(n, dtype=F32) / n)
        ar, ac = row[:, None] * inv, col[:, None] * inv
        cos = jnp.concatenate([jnp.cos(ar), jnp.cos(ar), jnp.cos(ac), jnp.cos(ac)], axis=-1)
        sin = jnp.concatenate([-jnp.sin(ar), jnp.sin(ar), -jnp.sin(ac), jnp.sin(ac)], axis=-1)
        return cos, sin

    def with_ctx(tab, fill):
        return jnp.concatenate([jnp.full((ctx_len, tab.shape[1]), fill, F32), tab], axis=0)

    cos, sin = tables(HEAD_DIM, LANES)
    cs = with_ctx(jnp.tile(cos, (1, LANES // HEAD_DIM)), 1.0)
    sn = with_ctx(jnp.tile(sin, (1, LANES // HEAD_DIM)), 0.0)
    cosm, sinm = tables(MLA_ROPE_DIM, LANES)
    pad_l, pad_r = MLA_NOPE_DIM, LANES - MLA_QK_DIM
    csm = with_ctx(jnp.pad(cosm, ((0, 0), (pad_l, pad_r)), constant_values=1.0), 1.0)
    snm = with_ctx(jnp.pad(sinm, ((0, 0), (pad_l, pad_r))), 0.0)
    return cs, sn, csm, snm


def _layer_params(l, w):
    d = D_MODEL
    gw = GROUP_WIDTH
    w_in = w['w_in'][l]
    a_end, b_end, c_end = 512, 512 + 416, 512 + 416 + 1056
    zeros = lambda c: jnp.zeros((d, c), F32)
    w_in_p = jnp.concatenate([
        w_in[:, :a_end],
        w_in[:, a_end:a_end + 384], zeros(MLA_NOPE_DIM), w_in[:, a_end + 384:b_end], zeros(LANES - MLA_QK_DIM),
        w_in[:, c_end:],
        w_in[:, b_end:c_end], zeros(C_COLS_PAD - 1056)], axis=1).astype(BF16)
    wuq = w['mla_w_uq'][l].reshape(MLA_Q_RANK, GROUP_HEADS, MLA_QK_DIM)
    wuq = jnp.pad(wuq, ((0, 0), (0, 0), (0, LANES - MLA_QK_DIM))).reshape(MLA_Q_RANK, GROUP_HEADS * LANES)
    wukv = w['mla_w_ukv'][l].reshape(MLA_KV_RANK, GROUP_HEADS, MLA_NOPE_DIM + HEAD_DIM)
    wk = jnp.pad(wukv[:, :, :MLA_NOPE_DIM], ((0, 0), (0, 0), (0, LANES - MLA_NOPE_DIM)))
    wukv_p = jnp.concatenate([wk.reshape(MLA_KV_RANK, GROUP_HEADS * LANES),
                              wukv[:, :, MLA_NOPE_DIM:].reshape(MLA_KV_RANK, gw)], axis=1)
    lane = jnp.arange(LANES)
    ones64 = ((lane[:, None] // HEAD_DIM) == (lane[None, :] // HEAD_DIM)).astype(BF16)
    zl = jnp.zeros((2, LANES - RWKV_DECAY_LORA, gw), F32)
    return {
        'w_in_p': w_in_p,
        'g_pre_mix': w['g_pre_mix'][l].reshape(1, d),
        'g_post_mix': w['g_post_mix'][l].reshape(1, d),
        'g_pre_mlp': w['g_pre_mlp'][l].reshape(1, d),
        'g_post_mlp': w['g_post_mlp'][l].reshape(1, d),
        'qn': jnp.tile(w['gqa_q_norm'][l], 2).reshape(1, LANES),
        'kn': jnp.tile(w['gqa_k_norm'][l], 2).reshape(1, LANES),
        'mqn': w['mla_q_norm'][l].reshape(1, MLA_Q_RANK),
        'mkvn': w['mla_kv_norm'][l].reshape(1, MLA_KV_RANK),
        'wuq': wuq.astype(BF16),
        'wukv': wukv_p.astype(BF16),
        'ones64': ones64,
        'mu': jnp.pad(w['rwkv_mu'][l], (0, C_COLS_PAD - 1056)).reshape(1, C_COLS_PAD),
        'w0': w['rwkv_w0'][l].reshape(2, 1, gw),
        'w2p': jnp.concatenate([w['rwkv_w2'][l], zl], axis=1),
        'a0': w['rwkv_a0'][l].reshape(2, 1, gw),
        'a2p': jnp.concatenate([zl, w['rwkv_a2'][l]], axis=1),
        'k_k': w['rwkv_k_k'][l].reshape(2, 1, gw),
        'k_a': w['rwkv_k_a'][l].reshape(2, 1, gw),
        'r_k': w['rwkv_r_k'][l].reshape(1, gw),
        'g2p': jnp.pad(w['rwkv_g2'][l], ((0, gw - RWKV_GATE_LORA), (0, 0))),
        'ln_w': w['rwkv_ln_w'][l].reshape(1, gw),
        'ln_b': w['rwkv_ln_b'][l].reshape(1, gw),
        'sink': w['swa_sink'][l],
        'w_out': w['w_out'][l].astype(BF16),
        'w_mlp1': w['w_mlp1'][l].astype(BF16),
        'w_mlp2': w['w_mlp2'][l].astype(BF16),
    }


def kernel(x, c, ctx, c_ctx, w_mod, b_mod, g_pre_mix, g_post_mix, g_pre_mlp, g_post_mlp, w_in, gqa_q_norm, gqa_k_norm, mla_q_norm, mla_kv_norm, mla_w_uq, mla_w_ukv, rwkv_mu, rwkv_w0, rwkv_w2, rwkv_a0, rwkv_a2, rwkv_k_k, rwkv_k_a, rwkv_r_k, rwkv_g2, rwkv_ln_w, rwkv_ln_b, swa_sink, w_out, w_mlp1, w_mlp2):
    batch, seq, d = x.shape
    ctx_len = ctx.shape[1]
    depth = w_mod.shape[0]
    assert d == D_MODEL and ctx_len == TM and seq % TM == 0 and seq % GRID_W == 0
    assert batch + 1 <= 8
    s_tot = ctx_len + seq
    n_tiles = s_tot // TM
    w = dict(w_in=w_in, g_pre_mix=g_pre_mix, g_post_mix=g_post_mix, g_pre_mlp=g_pre_mlp,
             g_post_mlp=g_post_mlp, gqa_q_norm=gqa_q_norm, gqa_k_norm=gqa_k_norm, mla_q_norm=mla_q_norm,
             mla_kv_norm=mla_kv_norm, mla_w_uq=mla_w_uq, mla_w_ukv=mla_w_ukv, rwkv_mu=rwkv_mu,
             rwkv_w0=rwkv_w0, rwkv_w2=rwkv_w2, rwkv_a0=rwkv_a0, rwkv_a2=rwkv_a2, rwkv_k_k=rwkv_k_k,
             rwkv_k_a=rwkv_k_a, rwkv_r_k=rwkv_r_k, rwkv_g2=rwkv_g2, rwkv_ln_w=rwkv_ln_w,
             rwkv_ln_b=rwkv_ln_b, swa_sink=swa_sink, w_out=w_out, w_mlp1=w_mlp1, w_mlp2=w_mlp2)

    ct = jnp.concatenate([c, c_ctx[None, :], jnp.zeros((8 - batch - 1, d), F32)], axis=0).T
    mods = _modulation(ct, w_mod, b_mod, batch + 1)

    def mod_row(i):
        return jnp.where(i % n_tiles == 0, batch, i // n_tiles)

    tabs = _rope_tables(seq, ctx_len)
    geo = dict(batch=batch, s_tot=s_tot, ctx_len=ctx_len)
    n = batch * s_tot
    n_lat = n_tiles - 1
    xc, xl = ctx.reshape(batch * ctx_len, d), x.reshape(batch * seq, d)
    xc_map = lambda b, j: (b, 0)
    xl_map = lambda b, j: (b * n_lat + jnp.maximum(j - 1, 0), 0)
    for l in range(depth):
        p = _layer_params(l, w)
        mod3 = mods[l].reshape(8, 1, 6 * d)
        flat = lambda m: (lambda i: m(i // n_tiles, i % n_tiles))
        qa, ka, va, qb, kb, vb, qd, kd, vd, zc = _inproj_prep(
            xc, xl, flat(xc_map), flat(xl_map), mod3, tabs, p, mod_row, n, n_tiles)
        oa = _flash(qa, ka, va, shared_kv=True, **geo)
        ob = _flash(qb, kb, vb, shared_kv=False, **geo)
        od = _window_attn(p['sink'], qd, kd, vd, **geo)
        lw, rkd, ra, rb, rr, rv, rg, bonus = _rwkv_prep(zc, p, n_tiles)
        yf = _rwkv_scan(lw, rkd, ra, rb, rr, rv, d=0, batch=batch, n_tiles=n_tiles)
        yr = _rwkv_scan(lw, rkd, ra, rb, rr, rv, d=1, batch=batch, n_tiles=n_tiles)
        xs = _out_mlp(xc, xl, xc_map, xl_map, oa, ob, od, yf, yr, bonus, rg, mod3, p,
                      batch=batch, n_tiles=n_tiles, skip_ctx=l == depth - 1)
        xc = xl = xs
        xc_map = lambda b, j: (b * n_tiles, 0)
        xl_map = lambda b, j: (b * n_tiles + j, 0)
    return xs.reshape(batch, seq, d)
```

```python
import functools
import math

import jax
import jax.numpy as jnp
from jax import lax
from jax.experimental import pallas as pl
from jax.experimental.pallas import tpu as pltpu

F32 = jnp.float32
BF16 = jnp.bfloat16

D_MODEL = 1024
GRID_W = 64
HEAD_DIM = 64
GROUP_WIDTH = 256
GROUP_HEADS = 4
ROPE_THETA = 10000.0
NORM_EPS = 1e-6
NEG_INF = -1e30
MLA_Q_RANK = 256
MLA_KV_RANK = 128
MLA_NOPE_DIM = 64
MLA_ROPE_DIM = 32
MLA_QK_DIM = MLA_NOPE_DIM + MLA_ROPE_DIM
RWKV_DECAY_LORA = 64
RWKV_ICLR_LORA = 64
RWKV_GATE_LORA = 160
RWKV_GN_EPS = 64e-5
WINDOW = 128
D_FF = 4 * D_MODEL

LOG2E = math.log2(math.e)
LANES = 128
TM = 256
ONES_ROWS = 8
LOOKAHEAD = 2
WINDOW_BLOCKS = 11
CHUNK = 64
SUB = 16
GQA_KV_WIDTH = 2 * HEAD_DIM
A_COLS = GROUP_WIDTH + 2 * GQA_KV_WIDTH
B_COLS = MLA_Q_RANK + MLA_KV_RANK + MLA_ROPE_DIM
C_COLS = 3 * GROUP_WIDTH + RWKV_DECAY_LORA + RWKV_ICLR_LORA + RWKV_GATE_LORA
SEC_W = 512
SEC_A = 0
SEC_B = SEC_A + SEC_W
SEC_D = SEC_B + SEC_W
ABD_COLS = SEC_D + SEC_W
C_COLS_PAD = 1152
MOD_BLOCK = 1536
ROPE_SHIFT = HEAD_DIM // 4
ROPE_SHIFT_MLA = MLA_ROPE_DIM // 4
VMEM_LIMIT = 56 * 1024 * 1024


def _cparams(sem):
    return pltpu.CompilerParams(dimension_semantics=sem, vmem_limit_bytes=VMEM_LIMIT)


def _dot(a, b):
    return jnp.dot(a, b, preferred_element_type=F32)


def _dot_nt(a, b):
    return lax.dot_general(a, b, (((1,), (1,)), ((), ())), preferred_element_type=F32)


def _dot_tn(a, b):
    return lax.dot_general(a, b, (((0,), (0,)), ((), ())), preferred_element_type=F32)


def _split2(x):
    hi = x.astype(BF16)
    lo = (x - hi.astype(F32)).astype(BF16)
    return hi, lo


def _mm_exact_rhs(a, b_bf16):
    hi, lo = _split2(a)
    return _dot(hi, b_bf16) + _dot(lo, b_bf16)


def _mm_exact_lhs(a_bf16, b):
    hi, lo = _split2(b)
    return _dot(a_bf16, hi) + _dot(a_bf16, lo)


def _mm3(a, b):
    ah, al = _split2(a)
    bh, bl = _split2(b)
    return _dot(ah, bh) + (_dot(ah, bl) + _dot(al, bh))


def _sigmoid(x):
    return 1.0 / (1.0 + jnp.exp(-x))


def _rms(x, eps):
    return x * lax.rsqrt(jnp.mean(x * x, axis=-1, keepdims=True) + eps)


def _mod_kernel(ct_ref, w_ref, b_ref, o_ref, *, n_rows):
    ct = ct_ref[...]
    st = ct * _sigmoid(ct)
    w = w_ref[...]
    rows = [jnp.sum(st[:, r:r + 1] * w, axis=0, keepdims=True) for r in range(n_rows)]
    rows.append(jnp.zeros((8 - n_rows, w.shape[1]), F32))
    o_ref[...] = jnp.concatenate(rows, axis=0) + b_ref[...]


def _modulation(ct, w_mod, b_mod, n_rows):
    depth, d, n6 = w_mod.shape
    tn = MOD_BLOCK
    return pl.pallas_call(
        functools.partial(_mod_kernel, n_rows=n_rows),
        grid=(depth, n6 // tn),
        in_specs=[pl.BlockSpec((d, 8), lambda l, j: (0, 0)),
                  pl.BlockSpec((None, d, tn), lambda l, j: (l, 0, j)),
                  pl.BlockSpec((None, 1, tn), lambda l, j: (l, 0, j))],
        out_specs=pl.BlockSpec((None, 8, tn), lambda l, j: (l, 0, j)),
        out_shape=jax.ShapeDtypeStruct((depth, 8, n6), F32),
        compiler_params=_cparams(("parallel", "parallel")),
        name="modulation",
    )(ct, w_mod, b_mod.reshape(depth, 1, n6))


def _rope_slab(x, cos, sin_signed, shift):
    left = pltpu.roll(x, LANES - shift, axis=1)
    right = pltpu.roll(x, shift, axis=1)
    lane = lax.broadcasted_iota(jnp.int32, x.shape, 1)
    first = ((lane // shift) % 2) == 0
    return x * cos + jnp.where(first, left, right) * sin_signed


def _head_norm_slab(x, gain, ones):
    ss = _mm_exact_rhs(x * x, ones)
    return x * lax.rsqrt(ss * (1.0 / HEAD_DIM) + NORM_EPS) * gain


def _inproj_prep_kernel(xc_ref, xl_ref, mod_ref, g_ref, w_ref,
                        cs_ref, sn_ref, csm_ref, snm_ref, qn_ref, kn_ref, mqn_ref, mkvn_ref,
                        wuq_ref, wukv_ref, ones_ref,
                        qa_ref, ka_ref, va_ref, qb_ref, kb_ref, vb_ref, qd_ref, kd_ref, vd_ref, zc_ref,
                        z_ref, *, n_tiles):
    d = D_MODEL
    mod = mod_ref[...]
    is_ctx = pl.program_id(0) % n_tiles == 0
    x = jnp.where(is_ctx, xc_ref[...], xl_ref[...])
    h16 = (_rms(x, NORM_EPS) * g_ref[...] * (1.0 + mod[:, d:2 * d]) + mod[:, 0:d]).astype(BF16)
    cs, sn = cs_ref[...], sn_ref[...]
    csm, snm = csm_ref[...], snm_ref[...]
    ones = ones_ref[...]
    hd = HEAD_DIM
    q_scale = HEAD_DIM ** -0.5

    def project(lo, hi):
        return _dot(h16, w_ref[:, lo:hi])

    def put_heads(ref, base, slab):
        ref[base] = slab[:, :hd].astype(BF16)
        ref[base + 1] = slab[:, hd:].astype(BF16)

    def put_heads_t(ref, base, slab, ones_rows=ONES_ROWS):
        st = slab.T.astype(BF16)
        if ones_rows:
            one = jnp.ones((ones_rows, st.shape[1]), BF16)
            ref[base] = jnp.concatenate([st[:hd], one], axis=0)
            ref[base + 1] = jnp.concatenate([st[hd:], one], axis=0)
        else:
            ref[base] = st[:hd]
            ref[base + 1] = st[hd:]

    z_ref[:, SEC_B:SEC_B + SEC_W] = project(SEC_B, SEC_B + SEC_W)
    z_ref[:, SEC_A:SEC_A + SEC_W] = project(SEC_A, SEC_A + SEC_W)

    cq = _rms(z_ref[:, SEC_B:SEC_B + MLA_Q_RANK], NORM_EPS) * mqn_ref[...]
    q = _dot(cq.astype(BF16), wuq_ref[...])
    ckv = _rms(z_ref[:, SEC_B + MLA_Q_RANK:SEC_B + MLA_Q_RANK + MLA_KV_RANK], NORM_EPS) * mkvn_ref[...]
    kv = _dot(ckv.astype(BF16), wukv_ref[...])

    z_ref[:, SEC_D:SEC_D + SEC_W] = project(SEC_D, SEC_D + SEC_W)

    for s in range(2):
        x = z_ref[:, SEC_A + LANES * s:SEC_A + LANES * (s + 1)]
        x = _rope_slab(_head_norm_slab(x, qn_ref[...], ones), cs, sn, ROPE_SHIFT) * (q_scale * LOG2E)
        put_heads_t(qa_ref, 2 * s, x, 0)
    x = z_ref[:, SEC_A + GROUP_WIDTH:SEC_A + GROUP_WIDTH + GQA_KV_WIDTH]
    put_heads(ka_ref, 0, _rope_slab(_head_norm_slab(x, kn_ref[...], ones), cs, sn, ROPE_SHIFT))
    put_heads_t(va_ref, 0, z_ref[:, SEC_A + GROUP_WIDTH + GQA_KV_WIDTH:SEC_A + SEC_W])

    zc_ref[...] = project(ABD_COLS, ABD_COLS + C_COLS_PAD)

    kr = _rope_slab(z_ref[:, SEC_B + MLA_Q_RANK + MLA_KV_RANK:SEC_B + SEC_W], csm, snm, ROPE_SHIFT_MLA)
    b_scale = MLA_QK_DIM ** -0.5 * LOG2E
    for h in range(GROUP_HEADS):
        qh = q[:, LANES * h:LANES * (h + 1)]
        qb_ref[h] = (_rope_slab(qh, csm, snm, ROPE_SHIFT_MLA) * b_scale).T.astype(BF16)
        kb_ref[h] = (kv[:, LANES * h:LANES * (h + 1)] + kr).astype(BF16)
    for s in range(2):
        put_heads_t(vb_ref, 2 * s, kv[:, LANES * (GROUP_HEADS + s):LANES * (GROUP_HEADS + s + 1)])

    for s in range(2):
        x = z_ref[:, SEC_D + LANES * s:SEC_D + LANES * (s + 1)]
        put_heads(qd_ref, 2 * s, _rope_slab(x, cs, sn, ROPE_SHIFT) * q_scale)
    put_heads(kd_ref, 0, _rope_slab(z_ref[:, SEC_D + GROUP_WIDTH:SEC_D + GROUP_WIDTH + GQA_KV_WIDTH], cs, sn, ROPE_SHIFT))
    put_heads(vd_ref, 0, z_ref[:, SEC_D + GROUP_WIDTH + GQA_KV_WIDTH:SEC_D + SEC_W])


def _inproj_prep(xc, xl, xc_map, xl_map, mod3, tabs, p, mod_row, n, n_tiles):
    cs, sn, csm, snm = tabs
    cols = p['w_in_p'].shape[1]

    def tab_spec():
        return pl.BlockSpec((TM, LANES), lambda i: (i % n_tiles, 0))

    def row_spec(w):
        return pl.BlockSpec((1, w), lambda i: (0, 0))

    def heads_out(nh, w):
        return (pl.BlockSpec((nh, TM, w), lambda i: (0, i, 0)),
                jax.ShapeDtypeStruct((nh, n, w), BF16))

    def heads_out_t(nh, rows=HEAD_DIM + ONES_ROWS):
        return (pl.BlockSpec((nh, None, rows, TM), lambda i: (0, i, 0, 0)),
                jax.ShapeDtypeStruct((nh, n // TM, rows, TM), BF16))

    outs = [heads_out_t(4, HEAD_DIM), heads_out(2, 64), heads_out_t(2),
            heads_out_t(4, LANES), heads_out(4, 128), heads_out_t(4),
            heads_out(4, 64), heads_out(2, 64), heads_out(2, 64),
            (pl.BlockSpec((TM, C_COLS_PAD), lambda i: (i, 0)), jax.ShapeDtypeStruct((n, C_COLS_PAD), F32))]
    return pl.pallas_call(
        functools.partial(_inproj_prep_kernel, n_tiles=n_tiles),
        grid=(n // TM,),
        in_specs=[pl.BlockSpec((TM, D_MODEL), xc_map), pl.BlockSpec((TM, D_MODEL), xl_map),
                  pl.BlockSpec((None, 1, 6 * D_MODEL), lambda i: (mod_row(i), 0, 0)),
                  row_spec(D_MODEL),
                  pl.BlockSpec((D_MODEL, cols), lambda i: (0, 0)),
                  tab_spec(), tab_spec(), tab_spec(), tab_spec(),
                  row_spec(LANES), row_spec(LANES), row_spec(MLA_Q_RANK), row_spec(MLA_KV_RANK),
                  pl.BlockSpec((MLA_Q_RANK, GROUP_HEADS * LANES), lambda i: (0, 0)),
                  pl.BlockSpec((MLA_KV_RANK, GROUP_HEADS * LANES + GROUP_WIDTH), lambda i: (0, 0)),
                  pl.BlockSpec((LANES, LANES), lambda i: (0, 0))],
        out_specs=[o[0] for o in outs],
        out_shape=[o[1] for o in outs],
        scratch_shapes=[pltpu.VMEM((TM, ABD_COLS), F32)],
        compiler_params=_cparams(("parallel",)),
        name="inproj_prep",
    )(xc, xl, mod3, p['g_pre_mix'], p['w_in_p'], cs, sn, csm, snm,
      p['qn'], p['kn'], p['mqn'], p['mkvn'], p['wuq'], p['wukv'], p['ones64'])


def _flash_kernel(q_ref, k_ref, vt_ref, o_ref, *, shared_kv, tq, tk, ctx_len, s_tot):
    qi = pl.program_id(2)
    dv = vt_ref.shape[-2] - ONES_ROWS
    if shared_kv:
        streams = [(jnp.concatenate([q_ref[0], q_ref[1]], axis=1), 0)]
    else:
        streams = [(q_ref[0], 0), (q_ref[1], 1)]

    def scores(c):
        out = []
        for q, kv in streams:
            k = k_ref[kv, pl.ds(pl.multiple_of(c * tk, tk), tk), :]
            out.append(_dot(k, q))
        return out

    def softmax_pv(c, sts, carries):
        new = []
        for st, (_, kv), (m, acc) in zip(sts, streams, carries):
            m_new = jnp.maximum(m, jnp.max(st, axis=0, keepdims=True))
            pt = jnp.exp2(st - m_new).astype(BF16)
            acc = jnp.exp2(m - m_new) * acc + _dot(vt_ref[kv, c], pt)
            new.append((m_new, acc))
        return tuple(new)

    def run(n_chunks):
        unroll = next(u for u in (33, 11, 3, 2, 1) if n_chunks % u == 0)
        groups = n_chunks // unroll
        init = tuple((jnp.full((1, q.shape[1]), NEG_INF, F32), jnp.zeros((dv + ONES_ROWS, q.shape[1]), F32))
                     for q, _ in streams)

        def body(g, carries):
            queue = [scores(g * unroll + u) for u in range(min(LOOKAHEAD, unroll))]
            for u in range(unroll):
                if u + LOOKAHEAD < unroll:
                    queue.append(scores(g * unroll + u + LOOKAHEAD))
                carries = softmax_pv(g * unroll + u, queue.pop(0), carries)
            return carries

        carries = body(0, init) if groups == 1 else lax.fori_loop(0, groups, body, init)
        outs = [acc[:dv] / acc[dv:dv + 1] for _, acc in carries]
        if shared_kv:
            outs = [outs[0][:, :tq], outs[0][:, tq:]]
        o_ref[...] = jnp.concatenate(outs, axis=0).T.astype(o_ref.dtype)

    is_ctx = qi * tq < ctx_len

    @pl.when(is_ctx)
    def _():
        run(ctx_len // tk)

    @pl.when(jnp.logical_not(is_ctx))
    def _():
        run(s_tot // tk)


def _flash(qt, k, vt, *, shared_kv, batch, s_tot, ctx_len):
    nh, _, dk, _ = qt.shape
    n = k.shape[1]
    dv = vt.shape[-2] - ONES_ROWS
    nkv = 1 if shared_kv else 2
    tq = tk = TM
    nq = s_tot // tq
    kern = functools.partial(_flash_kernel, shared_kv=shared_kv, tq=tq, tk=tk,
                             ctx_len=ctx_len, s_tot=s_tot)
    return pl.pallas_call(
        kern,
        grid=(batch, nh // 2, nq),
        in_specs=[pl.BlockSpec((2, None, dk, tq), lambda b, p, i: (p, b * nq + i, 0, 0)),
                  pl.BlockSpec((nkv, s_tot, dk), lambda b, p, i: (p, b, 0)),
                  pl.BlockSpec((nkv, s_tot // tk, dv + ONES_ROWS, tk), lambda b, p, i: (p, b, 0, 0))],
        out_specs=pl.BlockSpec((tq, 2 * dv), lambda b, p, i: (b * nq + i, p)),
        out_shape=jax.ShapeDtypeStruct((n, nh * dv), BF16),
        compiler_params=_cparams(("parallel", "parallel", "arbitrary")),
        name="flash_shared" if shared_kv else "flash_split",
    )(qt, k, vt)


def _window_kernel(sink_ref, q_ref, k_ref, v_ref, o_ref, *, ctx_len, s_tot, n_blocks):
    w = WINDOW
    g = pl.program_id(1)
    first_lat = ctx_len // w
    last_blk = s_tot // w - 1
    r = lax.broadcasted_iota(jnp.int32, (2 * w, 3 * w), 0) % w
    c = lax.broadcasted_iota(jnp.int32, (2 * w, 3 * w), 1)
    row = lax.broadcasted_iota(jnp.int32, (2 * w, 1), 0)
    sk = jnp.where(row < w, sink_ref[2 * g], sink_ref[2 * g + 1])
    k_ctx = k_ref[0:ctx_len, :]
    v_ctx = v_ref[0:ctx_len, :]

    def rows(ref, blk):
        return ref[pl.ds(pl.multiple_of(blk * w, w), w), :]

    for i in range(n_blocks):
        jb = pl.program_id(2) * n_blocks + i
        is_lat = jb >= first_lat
        pb = jnp.clip(jb - 1, 0, last_blk)
        nb = jnp.clip(jb + 1, 0, last_blk)
        q = q_ref[:, w * i:w * (i + 1), :].reshape(2 * w, HEAD_DIM)
        kw = jnp.concatenate([rows(k_ref, pb), rows(k_ref, jb), rows(k_ref, nb)], axis=0)
        vw = jnp.concatenate([rows(v_ref, pb), rows(v_ref, jb), rows(v_ref, nb)], axis=0)
        s_w = _dot_nt(q, kw)
        s_c = _dot_nt(q, k_ctx)
        lo = jnp.where(is_lat, jnp.where(jb - 1 >= first_lat, 0, w), 3 * w)
        hi = jnp.where(jb + 1 <= last_blk, 3 * w - 1, 2 * w - 1)
        valid = jnp.logical_and(c >= jnp.maximum(r, lo), c <= jnp.minimum(r + 2 * w, hi))
        s_w = jnp.where(valid, s_w, NEG_INF)
        m = jnp.maximum(jnp.maximum(jnp.max(s_w, axis=-1, keepdims=True),
                                    jnp.max(s_c, axis=-1, keepdims=True)), sk)
        p_w = jnp.exp(s_w - m)
        p_c = jnp.exp(s_c - m)
        denom = jnp.exp(sk - m) + jnp.sum(p_w, axis=-1, keepdims=True) + jnp.sum(p_c, axis=-1, keepdims=True)
        o = (_dot(p_w.astype(BF16), vw) + _dot(p_c.astype(BF16), v_ctx)) / denom
        o_ref[w * i:w * (i + 1), :] = jnp.concatenate([o[:w], o[w:]], axis=-1).astype(o_ref.dtype)


def _window_attn(sink, q, k, v, *, batch, s_tot, ctx_len):
    nh, n, dk = q.shape
    nblk = s_tot // WINDOW
    wb = next(u for u in (WINDOW_BLOCKS, 6, 3, 2, 1) if nblk % u == 0)
    rows = WINDOW * wb
    nstep = nblk // wb
    kern = functools.partial(_window_kernel, ctx_len=ctx_len, s_tot=s_tot, n_blocks=wb)
    return pl.pallas_call(
        kern,
        grid=(batch, nh // 2, nstep),
        in_specs=[pl.BlockSpec(memory_space=pltpu.SMEM),
                  pl.BlockSpec((2, rows, dk), lambda b, g, j: (g, b * nstep + j, 0)),
                  pl.BlockSpec((None, s_tot, dk), lambda b, g, j: (g, b, 0)),
                  pl.BlockSpec((None, s_tot, dk), lambda b, g, j: (g, b, 0))],
        out_specs=pl.BlockSpec((rows, 2 * dk), lambda b, g, j: (b * nstep + j, g)),
        out_shape=jax.ShapeDtypeStruct((n, nh * dk), BF16),
        compiler_params=_cparams(("parallel", "parallel", "arbitrary")),
        name="window_attn",
    )(sink, q, k, v)


def _rwkv_prep_kernel(z_ref, zp_ref, zn_ref, mu_ref, w0_ref, w2_ref, a0_ref, a2_ref, kk_ref, ka_ref,
                      rk_ref, g2_ref, ones_ref,
                      lw_ref, kd_ref, a_ref, b_ref, r_ref, v_ref, g_ref, bonus_ref, *, n_tiles):
    j = pl.program_id(0) % n_tiles
    z = z_ref[...]
    rows = lax.broadcasted_iota(jnp.int32, z.shape, 0)
    prev_row = jnp.where(j <= 1, 0.0, zp_ref[7:8, :])
    next_row = jnp.where(jnp.logical_or(j == 0, j == n_tiles - 1), 0.0, zn_ref[0:1, :])
    z_prev = jnp.where(rows == 0, prev_row, pltpu.roll(z, 1, axis=0))
    z_next = jnp.where(rows == TM - 1, next_row, pltpu.roll(z, TM - 1, axis=0))
    zs = z + mu_ref[...] * (0.5 * (z_prev + z_next) - z)

    gw = GROUP_WIDTH
    r, k, v = zs[:, 0:gw], zs[:, gw:2 * gw], zs[:, 2 * gw:3 * gw]
    lora = zs[:, 3 * gw:3 * gw + LANES]
    gl = zs[:, 3 * gw + LANES:3 * gw + LANES + gw]
    wt = jnp.tanh(lora)
    ones = ones_ref[...]

    def seg_sum(x):
        return jnp.concatenate([_mm_exact_rhs(x[:, :LANES], ones), _mm_exact_rhs(x[:, LANES:], ones)], axis=-1)

    r_ref[...] = r
    v_ref[...] = v
    g_ref[...] = _mm3(_sigmoid(gl), g2_ref[...])
    kd_sum = None
    for d in range(2):
        u = w0_ref[d] + _mm3(wt, w2_ref[d])
        soft = jnp.maximum(-u, 0.0) + jnp.log(1.0 + jnp.exp(-jnp.abs(u)))
        lw_ref[d] = -jnp.exp(-soft - 0.5)
        gate = _sigmoid(a0_ref[d] + _mm3(lora, a2_ref[d]))
        kk = k * kk_ref[d]
        kk = kk / jnp.maximum(jnp.sqrt(seg_sum(kk * kk)), 1e-12)
        kd = k * (1.0 + (gate - 1.0) * ka_ref[d])
        kd_ref[d] = kd
        a_ref[d] = -kk
        b_ref[d] = kk * gate
        kd_sum = kd if kd_sum is None else kd_sum + kd
    bonus_ref[...] = seg_sum(r * kd_sum * rk_ref[...]) * v


def _rwkv_prep(zc, p, n_tiles):
    n = zc.shape[0]
    gw = GROUP_WIDTH
    nb8 = n // 8

    def full(shape):
        nd = len(shape)
        return pl.BlockSpec(shape, lambda i: (0,) * nd)

    dir_out = (pl.BlockSpec((2, TM, gw), lambda i: (0, i, 0)), jax.ShapeDtypeStruct((2, n, gw), F32))
    one_out = (pl.BlockSpec((TM, gw), lambda i: (i, 0)), jax.ShapeDtypeStruct((n, gw), F32))
    outs = [dir_out] * 4 + [one_out] * 4
    return pl.pallas_call(
        functools.partial(_rwkv_prep_kernel, n_tiles=n_tiles),
        grid=(n // TM,),
        in_specs=[pl.BlockSpec((TM, C_COLS_PAD), lambda i: (i, 0)),
                  pl.BlockSpec((8, C_COLS_PAD), lambda i: (jnp.maximum(i * (TM // 8) - 1, 0), 0)),
                  pl.BlockSpec((8, C_COLS_PAD), lambda i: (jnp.minimum((i + 1) * (TM // 8), nb8 - 1), 0)),
                  full((1, C_COLS_PAD)),
                  full((2, 1, gw)), full((2, LANES, gw)), full((2, 1, gw)), full((2, LANES, gw)),
                  full((2, 1, gw)), full((2, 1, gw)), full((1, gw)), full((gw, gw)),
                  full((LANES, LANES))],
        out_specs=[o[0] for o in outs],
        out_shape=[o[1] for o in outs],
        compiler_params=_cparams(("parallel",)),
        name="rwkv_prep",
    )(zc, zc, zc, p['mu'], p['w0'], p['w2p'], p['a0'], p['a2p'], p['k_k'], p['k_a'], p['r_k'],
      p['g2p'], p['ones64'])


def _rwkv_scan_kernel(lw_ref, k_ref, a_ref, b_ref, r_ref, v_ref, y_ref, s_ref, *, reverse):
    @pl.when(pl.program_id(0) == 0)
    def _():
        s_ref[...] = jnp.zeros_like(s_ref)

    batch = lw_ref.shape[0]
    n_chunks = TM // CHUNK
    hd = HEAD_DIM
    row = lax.broadcasted_iota(jnp.int32, (TM, TM), 0)
    col = lax.broadcasted_iota(jnp.int32, (TM, TM), 1)
    same = (row // CHUNK) == (col // CHUNK)
    before = (col > row) if reverse else (col < row)
    m_strict = jnp.logical_and(same, before)
    m_incl = jnp.logical_and(same, jnp.logical_or(before, row == col))
    m_incl16 = m_incl.astype(BF16)
    eye = (row == col).astype(F32)
    assert CHUNK // SUB == 4
    same_sub = (row // SUB) == (col // SUB)

    items = [(b, h) for b in range(batch) for h in range(GROUP_HEADS)]
    idx = range(len(items))
    e_pos, at_all, rt_all, bt_all, kt_all, v_all = [], [], [], [], [], []
    for b in range(batch):
        lw = lw_ref[b]
        cum = _mm_exact_lhs(m_incl16, lw)
        e_pos.append(jnp.exp(cum))
        e_neg = jnp.exp(-cum)
        at_all.append(a_ref[b] * jnp.exp(cum - lw))
        rt_all.append(r_ref[b] * e_pos[b])
        bt_all.append(b_ref[b] * e_neg)
        kt_all.append(k_ref[b] * e_neg)
        v_all.append(v_ref[b])

    def head(arrs, b, h):
        return arrs[b][:, hd * h:hd * (h + 1)]

    at = [head(at_all, b, h) for b, h in items]
    rt = [head(rt_all, b, h) for b, h in items]
    bt = [head(bt_all, b, h) for b, h in items]
    at16 = [x.astype(BF16) for x in at]
    rt16 = [x.astype(BF16) for x in rt]
    bt16 = [x.astype(BF16) for x in bt]
    kt16 = [head(kt_all, b, h).astype(BF16) for b, h in items]
    v16 = [head(v_all, b, h).astype(BF16) for b, h in items]
    ab = [jnp.where(m_strict, _dot_nt(at16[i], bt16[i]), 0.0) for i in idx]
    ak16 = [jnp.where(m_strict, _dot_nt(at16[i], kt16[i]), 0.0).astype(BF16) for i in idx]
    rk16 = [jnp.where(m_incl, _dot_nt(rt16[i], kt16[i]), 0.0).astype(BF16) for i in idx]
    n_c = TM // CHUNK

    def wide(m):
        out = m[:CHUNK]
        for c in range(1, n_c):
            out = out + m[CHUNK * c:CHUNK * (c + 1)]
        return out

    def bdiag16(wm):
        return jnp.where(same, jnp.concatenate([wm] * n_c, axis=0), 0.0).astype(BF16)

    def wdot(wl, bd16):
        return _dot(wl.astype(BF16), bd16)

    eye_w = wide(eye)
    ld = [jnp.where(same_sub, x, 0.0) for x in ab]
    lo16 = [jnp.where(same_sub, 0.0, x).astype(BF16) for x in ab]
    pw_w = [wide(x) for x in ld]
    pw_bd = [x.astype(BF16) for x in ld]
    td_w = [eye_w + x for x in pw_w]
    for _ in range(int(math.log2(SUB)) - 1):
        pw_w = [wdot(x, y) for x, y in zip(pw_w, pw_bd)]
        pw_bd = [bdiag16(x) for x in pw_w]
        td_w = [t + wdot(t, y) for t, y in zip(td_w, pw_bd)]
    m1_w = [wdot(t, x) for t, x in zip(td_w, lo16)]
    m1_bd = [bdiag16(x) for x in m1_w]
    m2_w = [wdot(x, y) for x, y in zip(m1_w, m1_bd)]
    m3_w = [wdot(x, bdiag16(y)) for x, y in zip(m1_w, m2_w)]
    nn_w = [eye_w + a1 + a2 + a3 for a1, a2, a3 in zip(m1_w, m2_w, m3_w)]
    akv = [_dot(ak16[i], v16[i]) for i in idx]

    rowc = lax.broadcasted_iota(jnp.int32, (TM, n_c * 2 * hd), 0) // CHUNK
    colc = lax.broadcasted_iota(jnp.int32, (TM, n_c * 2 * hd), 1) // (2 * hd)
    same_x = rowc == colc

    def spread16(xr):
        return jnp.where(same_x, jnp.concatenate([xr] * n_c, axis=1), 0.0).astype(BF16)

    def unwide(xw):
        return jnp.concatenate([xw[:, 2 * hd * c:2 * hd * (c + 1)] for c in range(n_c)], axis=0)

    tx = [wdot(td_w[i], spread16(jnp.concatenate([at[i], akv[i]], axis=-1))) for i in idx]
    wu = [unwide(wdot(nn_w[i], spread16(unwide(tx[i])))) for i in idx]
    wa = [x[:, :hd] for x in wu]
    u016 = [x[:, hd:].astype(BF16) for x in wu]
    rb_w = [wide(jnp.where(m_incl, _dot_nt(rt16[i], bt16[i]), 0.0)) for i in idx]
    rbwu = [unwide(wdot(rb_w[i], spread16(wu[i]))) for i in idx]
    yr16 = [(rt[i] + rbwu[i][:, :hd]).astype(BF16) for i in idx]
    y0 = [rbwu[i][:, hd:] + _dot(rk16[i], v16[i]) for i in idx]

    order = range(n_chunks - 1, -1, -1) if reverse else range(n_chunks)
    eye_h = eye[:hd, :hd]
    trans = {}
    for c in order:
        rs = slice(CHUNK * c, CHUNK * (c + 1))
        last = CHUNK * c if reverse else CHUNK * (c + 1) - 1
        for i, (b, h) in enumerate(items):
            decay = e_pos[b][last:last + 1, hd * h:hd * (h + 1)]
            g = (eye_h + _mm3tn(wa[i][rs], bt[i][rs])) * decay
            hh = (_dot_tn(u016[i][rs], bt16[i][rs]) + _dot_tn(v16[i][rs], kt16[i][rs])) * decay
            trans[c, i] = (g, hh)
    for c in order:
        rs = slice(CHUNK * c, CHUNK * (c + 1))
        s0 = [s_ref[b, h] for b, h in items]
        for i, (b, h) in enumerate(items):
            s_ref[b, h] = _mm3(s0[i], trans[c, i][0]) + trans[c, i][1]
        for b in range(batch):
            y_ref[b, rs, :] = jnp.concatenate(
                [_dot_nt(yr16[i][rs], s0[i].astype(BF16)) + y0[i][rs] for i, (bb, _) in enumerate(items) if bb == b],
                axis=-1)


def _mm3tn(a, b):
    ah, al = _split2(a)
    bh, bl = _split2(b)
    return _dot_tn(ah, bh) + (_dot_tn(ah, bl) + _dot_tn(al, bh))


def _rwkv_scan(lw, kd, a, b, r, v, *, d, batch, n_tiles):
    n, gw = r.shape
    s_tot = n // batch
    reverse = d == 1

    def tile(s):
        if reverse:
            return jnp.where(s == 0, 0, n_tiles - s)
        return s

    dspec = pl.BlockSpec((None, batch, TM, gw), lambda s: (d, 0, tile(s), 0))
    spec = pl.BlockSpec((batch, TM, gw), lambda s: (0, tile(s), 0))
    per_dir = [x.reshape(2, batch, s_tot, gw) for x in (lw, kd, a, b)]
    shared = [x.reshape(batch, s_tot, gw) for x in (r, v)]
    y = pl.pallas_call(
        functools.partial(_rwkv_scan_kernel, reverse=reverse),
        grid=(n_tiles,),
        in_specs=[dspec, dspec, dspec, dspec, spec, spec],
        out_specs=spec,
        out_shape=jax.ShapeDtypeStruct((batch, s_tot, gw), F32),
        scratch_shapes=[pltpu.VMEM((batch, GROUP_HEADS, HEAD_DIM, HEAD_DIM), F32)],
        compiler_params=_cparams(("arbitrary",)),
        name="rwkv_scan_rev" if reverse else "rwkv_scan_fwd",
    )(*per_dir, *shared)
    return y.reshape(n, gw)


def _out_mlp_kernel(xc_ref, xl_ref, oa_ref, ob_ref, od_ref, yf_ref, yr_ref, bonus_ref, g_ref,
                    lnw_ref, lnb_ref, ones_ref, mod_ref, gpost_ref, gpre_ref, gpm_ref,
                    wo_ref, w1_ref, w2_ref, xo_ref, *, ctx_first):
    d = D_MODEL
    gw = GROUP_WIDTH
    mod = mod_ref[...]
    ones = ones_ref[...]

    def seg_mean(t):
        sm = jnp.concatenate([_mm_exact_rhs(t[:, :LANES], ones), _mm_exact_rhs(t[:, LANES:], ones)], axis=-1)
        return sm * (1.0 / HEAD_DIM)

    yy = yf_ref[...] + yr_ref[...]
    dlt = yy - seg_mean(yy)
    yn = dlt * lax.rsqrt(seg_mean(dlt * dlt) + RWKV_GN_EPS) * lnw_ref[...] + lnb_ref[...]
    oc = ((yn + bonus_ref[...]) * g_ref[...]).astype(BF16)

    y = _dot(oa_ref[...], wo_ref[0:gw, :])
    y += _dot(ob_ref[...], wo_ref[gw:2 * gw, :])
    y += _dot(oc, wo_ref[2 * gw:3 * gw, :])
    y += _dot(od_ref[...], wo_ref[3 * gw:4 * gw, :])
    x = jnp.where(jnp.logical_and(ctx_first, pl.program_id(1) == 0), xc_ref[...], xl_ref[...])
    x1 = x + mod[:, 2 * d:3 * d] * (_rms(y, NORM_EPS) * gpost_ref[...])
    h = _rms(x1, NORM_EPS) * gpre_ref[...] * (1.0 + mod[:, 4 * d:5 * d]) + mod[:, 3 * d:4 * d]
    u = jnp.maximum(_dot(h.astype(BF16), w1_ref[...]), 0.0)
    zz = _dot((u * u).astype(BF16), w2_ref[...])
    xo_ref[...] = x1 + mod[:, 5 * d:6 * d] * (_rms(zz, NORM_EPS) * gpm_ref[...])


def _out_mlp(xc, xl, xc_map, xl_map, oa, ob, od, yf, yr, bonus, rg, mod3, p, *, batch, n_tiles, skip_ctx):
    d = D_MODEL
    gw = GROUP_WIDTH
    off = 1 if skip_ctx else 0
    nt = n_tiles - off

    def tile(b, j):
        return (b * n_tiles + off + j, 0)

    def mrow(b, j):
        return (jnp.where(off + j == 0, batch, b), 0, 0)

    ospec = pl.BlockSpec((TM, gw), tile)
    row = pl.BlockSpec((1, d), lambda b, j: (0, 0))
    grow = pl.BlockSpec((1, gw), lambda b, j: (0, 0))

    def wspec(shape):
        return pl.BlockSpec(shape, lambda b, j: (0, 0), pipeline_mode=pl.Buffered(1))

    return pl.pallas_call(
        functools.partial(_out_mlp_kernel, ctx_first=not skip_ctx),
        grid=(batch, nt),
        in_specs=[pl.BlockSpec((TM, d), lambda b, j: xc_map(b, off + j)),
                  pl.BlockSpec((TM, d), lambda b, j: xl_map(b, off + j)),
                  ospec, ospec, ospec, ospec, ospec, ospec, ospec,
                  grow, grow, pl.BlockSpec((LANES, LANES), lambda b, j: (0, 0)),
                  pl.BlockSpec((None, 1, 6 * d), mrow),
                  row, row, row,
                  wspec((d, d)), wspec((d, D_FF)), wspec((D_FF, d))],
        out_specs=pl.BlockSpec((TM, d), lambda b, j: (b * nt + j, 0)),
        out_shape=jax.ShapeDtypeStruct((batch * nt * TM, d), F32),
        compiler_params=_cparams(("parallel", "parallel")),
        name="out_mlp",
    )(xc, xl, oa, ob, od, yf, yr, bonus, rg, p['ln_w'], p['ln_b'], p['ones64'], mod3,
      p['g_post_mix'], p['g_pre_mlp'], p['g_post_mlp'], p['w_out'], p['w_mlp1'], p['w_mlp2'])


def _rope_tables(seq, ctx_len):
    t = jnp.arange(seq, dtype=jnp.int32)
    row = (t // GRID_W).astype(F32)
    col = (t % GRID_W).astype(F32)

    def tables(rot_dim, lane_dim):
        n = rot_dim // 4
        inv = ROPE_THETA ** (-jnp.arange(n, dtype=F32) / n)
        ar, ac = row[:, None] * inv, col[:, None] * inv
        cos = jnp.concatenate([jnp.cos(ar), jnp.cos(ar), jnp.cos(ac), jnp.cos(ac)], axis=-1)
        sin = jnp.concatenate([-jnp.sin(ar), jnp.sin(ar), -jnp.sin(ac), jnp.sin(ac)], axis=-1)
        return cos, sin

    def with_ctx(tab, fill):
        return jnp.concatenate([jnp.full((ctx_len, tab.shape[1]), fill, F32), tab], axis=0)

    cos, sin = tables(HEAD_DIM, LANES)
    cs = with_ctx(jnp.tile(cos, (1, LANES // HEAD_DIM)), 1.0)
    sn = with_ctx(jnp.tile(sin, (1, LANES // HEAD_DIM)), 0.0)
    cosm, sinm = tables(MLA_ROPE_DIM, LANES)
    pad_l, pad_r = MLA_NOPE_DIM, LANES - MLA_QK_DIM
    csm = with_ctx(jnp.pad(cosm, ((0, 0), (pad_l, pad_r)), constant_values=1.0), 1.0)
    snm = with_ctx(jnp.pad(sinm, ((0, 0), (pad_l, pad_r))), 0.0)
    return cs, sn, csm, snm


def _layer_params(l, w):
    d = D_MODEL
    gw = GROUP_WIDTH
    w_in = w['w_in'][l]
    a_end, b_end, c_end = A_COLS, A_COLS + B_COLS, A_COLS + B_COLS + C_COLS
    lowrank = MLA_Q_RANK + MLA_KV_RANK
    zeros = lambda c: jnp.zeros((d, c), F32)
    w_in_p = jnp.concatenate([
        w_in[:, :a_end],
        w_in[:, a_end:a_end + lowrank], zeros(MLA_NOPE_DIM), w_in[:, a_end + lowrank:b_end], zeros(LANES - MLA_QK_DIM),
        w_in[:, c_end:],
        w_in[:, b_end:c_end], zeros(C_COLS_PAD - C_COLS)], axis=1).astype(BF16)
    wuq = w['mla_w_uq'][l].reshape(MLA_Q_RANK, GROUP_HEADS, MLA_QK_DIM)
    wuq = jnp.pad(wuq, ((0, 0), (0, 0), (0, LANES - MLA_QK_DIM))).reshape(MLA_Q_RANK, GROUP_HEADS * LANES)
    wukv = w['mla_w_ukv'][l].reshape(MLA_KV_RANK, GROUP_HEADS, MLA_NOPE_DIM + HEAD_DIM)
    wk = jnp.pad(wukv[:, :, :MLA_NOPE_DIM], ((0, 0), (0, 0), (0, LANES - MLA_NOPE_DIM)))
    wukv_p = jnp.concatenate([wk.reshape(MLA_KV_RANK, GROUP_HEADS * LANES),
                              wukv[:, :, MLA_NOPE_DIM:].reshape(MLA_KV_RANK, gw)], axis=1)
    lane = jnp.arange(LANES)
    ones64 = ((lane[:, None] // HEAD_DIM) == (lane[None, :] // HEAD_DIM)).astype(BF16)
    zrows = lambda r: jnp.zeros((2, r, gw), F32)
    return {
        'w_in_p': w_in_p,
        'g_pre_mix': w['g_pre_mix'][l].reshape(1, d),
        'g_post_mix': w['g_post_mix'][l].reshape(1, d),
        'g_pre_mlp': w['g_pre_mlp'][l].reshape(1, d),
        'g_post_mlp': w['g_post_mlp'][l].reshape(1, d),
        'qn': jnp.tile(w['gqa_q_norm'][l], 2).reshape(1, LANES),
        'kn': jnp.tile(w['gqa_k_norm'][l], 2).reshape(1, LANES),
        'mqn': w['mla_q_norm'][l].reshape(1, MLA_Q_RANK),
        'mkvn': w['mla_kv_norm'][l].reshape(1, MLA_KV_RANK),
        'wuq': wuq.astype(BF16),
        'wukv': wukv_p.astype(BF16),
        'ones64': ones64,
        'mu': jnp.pad(w['rwkv_mu'][l], (0, C_COLS_PAD - C_COLS)).reshape(1, C_COLS_PAD),
        'w0': w['rwkv_w0'][l].reshape(2, 1, gw),
        'w2p': jnp.concatenate([w['rwkv_w2'][l], zrows(LANES - RWKV_DECAY_LORA)], axis=1),
        'a0': w['rwkv_a0'][l].reshape(2, 1, gw),
        'a2p': jnp.concatenate([zrows(RWKV_DECAY_LORA), w['rwkv_a2'][l],
                                zrows(LANES - RWKV_DECAY_LORA - RWKV_ICLR_LORA)], axis=1),
        'k_k': w['rwkv_k_k'][l].reshape(2, 1, gw),
        'k_a': w['rwkv_k_a'][l].reshape(2, 1, gw),
        'r_k': w['rwkv_r_k'][l].reshape(1, gw),
        'g2p': jnp.pad(w['rwkv_g2'][l], ((0, gw - RWKV_GATE_LORA), (0, 0))),
        'ln_w': w['rwkv_ln_w'][l].reshape(1, gw),
        'ln_b': w['rwkv_ln_b'][l].reshape(1, gw),
        'sink': w['swa_sink'][l],
        'w_out': w['w_out'][l].astype(BF16),
        'w_mlp1': w['w_mlp1'][l].astype(BF16),
        'w_mlp2': w['w_mlp2'][l].astype(BF16),
    }


def kernel(x, c, ctx, c_ctx, w_mod, b_mod, g_pre_mix, g_post_mix, g_pre_mlp, g_post_mlp, w_in, gqa_q_norm, gqa_k_norm, mla_q_norm, mla_kv_norm, mla_w_uq, mla_w_ukv, rwkv_mu, rwkv_w0, rwkv_w2, rwkv_a0, rwkv_a2, rwkv_k_k, rwkv_k_a, rwkv_r_k, rwkv_g2, rwkv_ln_w, rwkv_ln_b, swa_sink, w_out, w_mlp1, w_mlp2):
    batch, seq, d = x.shape
    ctx_len = ctx.shape[1]
    depth = w_mod.shape[0]
    assert d == D_MODEL and ctx_len == TM and seq % TM == 0 and seq % GRID_W == 0
    assert batch + 1 <= 8
    s_tot = ctx_len + seq
    n_tiles = s_tot // TM
    w = dict(w_in=w_in, g_pre_mix=g_pre_mix, g_post_mix=g_post_mix, g_pre_mlp=g_pre_mlp,
             g_post_mlp=g_post_mlp, gqa_q_norm=gqa_q_norm, gqa_k_norm=gqa_k_norm, mla_q_norm=mla_q_norm,
             mla_kv_norm=mla_kv_norm, mla_w_uq=mla_w_uq, mla_w_ukv=mla_w_ukv, rwkv_mu=rwkv_mu,
             rwkv_w0=rwkv_w0, rwkv_w2=rwkv_w2, rwkv_a0=rwkv_a0, rwkv_a2=rwkv_a2, rwkv_k_k=rwkv_k_k,
             rwkv_k_a=rwkv_k_a, rwkv_r_k=rwkv_r_k, rwkv_g2=rwkv_g2, rwkv_ln_w=rwkv_ln_w,
             rwkv_ln_b=rwkv_ln_b, swa_sink=swa_sink, w_out=w_out, w_mlp1=w_mlp1, w_mlp2=w_mlp2)

    ct = jnp.concatenate([c, c_ctx[None, :], jnp.zeros((8 - batch - 1, d), F32)], axis=0).T
    mods = _modulation(ct, w_mod, b_mod, batch + 1)

    def mod_row(i):
        return jnp.where(i % n_tiles == 0, batch, i // n_tiles)

    tabs = _rope_tables(seq, ctx_len)
    geo = dict(batch=batch, s_tot=s_tot, ctx_len=ctx_len)
    n = batch * s_tot
    n_lat = n_tiles - 1
    xc, xl = ctx.reshape(batch * ctx_len, d), x.reshape(batch * seq, d)
    xc_map = lambda b, j: (b, 0)
    xl_map = lambda b, j: (b * n_lat + jnp.maximum(j - 1, 0), 0)
    for l in range(depth):
        p = _layer_params(l, w)
        mod3 = mods[l].reshape(8, 1, 6 * d)
        flat = lambda m: (lambda i: m(i // n_tiles, i % n_tiles))
        qa, ka, va, qb, kb, vb, qd, kd, vd, zc = _inproj_prep(
            xc, xl, flat(xc_map), flat(xl_map), mod3, tabs, p, mod_row, n, n_tiles)
        oa = _flash(qa, ka, va, shared_kv=True, **geo)
        ob = _flash(qb, kb, vb, shared_kv=False, **geo)
        od = _window_attn(p['sink'], qd, kd, vd, **geo)
        lw, rkd, ra, rb, rr, rv, rg, bonus = _rwkv_prep(zc, p, n_tiles)
        yf = _rwkv_scan(lw, rkd, ra, rb, rr, rv, d=0, batch=batch, n_tiles=n_tiles)
        yr = _rwkv_scan(lw, rkd, ra, rb, rr, rv, d=1, batch=batch, n_tiles=n_tiles)
        xs = _out_mlp(xc, xl, xc_map, xl_map, oa, ob, od, yf, yr, bonus, rg, mod3, p,
                      batch=batch, n_tiles=n_tiles, skip_ctx=l == depth - 1)
        xc = xl = xs
        xc_map = lambda b, j: (b * n_tiles, 0)
        xl_map = lambda b, j: (b * n_tiles + j, 0)
    return xs.reshape(batch, seq, d)
```

```python
import functools
import math

import jax
import jax.numpy as jnp
from jax import lax
from jax.experimental import pallas as pl
from jax.experimental.pallas import tpu as pltpu

F32 = jnp.float32
BF16 = jnp.bfloat16

D_MODEL = 1024
GRID_W = 64
HEAD_DIM = 64
GROUP_WIDTH = 256
GROUP_HEADS = 4
ROPE_THETA = 10000.0
NORM_EPS = 1e-6
NEG_INF = -1e30
MLA_Q_RANK = 256
MLA_KV_RANK = 128
MLA_NOPE_DIM = 64
MLA_ROPE_DIM = 32
MLA_QK_DIM = MLA_NOPE_DIM + MLA_ROPE_DIM
RWKV_DECAY_LORA = 64
RWKV_ICLR_LORA = 64
RWKV_GATE_LORA = 160
RWKV_GN_EPS = 64e-5
WINDOW = 128
D_FF = 4 * D_MODEL

LOG2E = math.log2(math.e)
LANES = 128
TM = 256
ONES_ROWS = 8
LOOKAHEAD = 2
WINDOW_BLOCKS = 11
CHUNK = 64
SUB = 16
GQA_KV_WIDTH = 2 * HEAD_DIM
A_COLS = GROUP_WIDTH + 2 * GQA_KV_WIDTH
B_COLS = MLA_Q_RANK + MLA_KV_RANK + MLA_ROPE_DIM
C_COLS = 3 * GROUP_WIDTH + RWKV_DECAY_LORA + RWKV_ICLR_LORA + RWKV_GATE_LORA
SEC_W = 512
SEC_A = 0
SEC_B = SEC_A + SEC_W
SEC_D = SEC_B + SEC_W
ABD_COLS = SEC_D + SEC_W
C_COLS_PAD = 1152
MOD_BLOCK = 1536
ROPE_SHIFT = HEAD_DIM // 4
ROPE_SHIFT_MLA = MLA_ROPE_DIM // 4
VMEM_LIMIT = 56 * 1024 * 1024


def _cparams(sem):
    return pltpu.CompilerParams(dimension_semantics=sem, vmem_limit_bytes=VMEM_LIMIT)


def _dot(a, b):
    return jnp.dot(a, b, preferred_element_type=F32)


def _dot_nt(a, b):
    return lax.dot_general(a, b, (((1,), (1,)), ((), ())), preferred_element_type=F32)


def _dot_tn(a, b):
    return lax.dot_general(a, b, (((0,), (0,)), ((), ())), preferred_element_type=F32)


def _split2(x):
    hi = x.astype(BF16)
    lo = (x - hi.astype(F32)).astype(BF16)
    return hi, lo


def _mm_exact_rhs(a, b_bf16):
    hi, lo = _split2(a)
    return _dot(hi, b_bf16) + _dot(lo, b_bf16)


def _mm_exact_lhs(a_bf16, b):
    hi, lo = _split2(b)
    return _dot(a_bf16, hi) + _dot(a_bf16, lo)


def _mm3(a, b):
    ah, al = _split2(a)
    bh, bl = _split2(b)
    return _dot(ah, bh) + (_dot(ah, bl) + _dot(al, bh))


def _sigmoid(x):
    return 1.0 / (1.0 + jnp.exp(-x))


def _rms(x, eps):
    return x * lax.rsqrt(jnp.mean(x * x, axis=-1, keepdims=True) + eps)


def _mod_kernel(ct_ref, w_ref, b_ref, o_ref, *, n_rows):
    ct = ct_ref[...]
    st = ct * _sigmoid(ct)
    w = w_ref[...]
    rows = [jnp.sum(st[:, r:r + 1] * w, axis=0, keepdims=True) for r in range(n_rows)]
    rows.append(jnp.zeros((8 - n_rows, w.shape[1]), F32))
    o_ref[...] = jnp.concatenate(rows, axis=0) + b_ref[...]


def _modulation(ct, w_mod, b_mod, n_rows):
    depth, d, n6 = w_mod.shape
    tn = MOD_BLOCK
    return pl.pallas_call(
        functools.partial(_mod_kernel, n_rows=n_rows),
        grid=(depth, n6 // tn),
        in_specs=[pl.BlockSpec((d, 8), lambda l, j: (0, 0)),
                  pl.BlockSpec((None, d, tn), lambda l, j: (l, 0, j)),
                  pl.BlockSpec((None, 1, tn), lambda l, j: (l, 0, j))],
        out_specs=pl.BlockSpec((None, 8, tn), lambda l, j: (l, 0, j)),
        out_shape=jax.ShapeDtypeStruct((depth, 8, n6), F32),
        compiler_params=_cparams(("parallel", "parallel")),
        name="modulation",
    )(ct, w_mod, b_mod.reshape(depth, 1, n6))


def _rope_slab(x, cos, sin_signed, shift):
    left = pltpu.roll(x, LANES - shift, axis=1)
    right = pltpu.roll(x, shift, axis=1)
    lane = lax.broadcasted_iota(jnp.int32, x.shape, 1)
    first = ((lane // shift) % 2) == 0
    return x * cos + jnp.where(first, left, right) * sin_signed


def _head_norm_slab(x, gain, ones):
    ss = _mm_exact_rhs(x * x, ones)
    return x * lax.rsqrt(ss * (1.0 / HEAD_DIM) + NORM_EPS) * gain


def _inproj_prep_kernel(xc_ref, xl_ref, xp_ref, xn_ref, mod_ref, g_ref, w_ref,
                        cs_ref, sn_ref, csm_ref, snm_ref, qn_ref, kn_ref, mqn_ref, mkvn_ref,
                        wuq_ref, wukv_ref, ones_ref,
                        mu_ref, w0_ref, w2_ref, a0_ref, a2_ref, kk_ref, ka_ref, rk_ref, g2_ref,
                        qa_ref, ka_out, va_ref, qb_ref, kb_ref, vb_ref, qd_ref, kd_out, vd_ref,
                        lw_ref, rkd_ref, ra_ref, rb_ref, rr_ref, rv_ref, rg_ref, bonus_ref,
                        z_ref, zc_s, *, n_tiles):
    d = D_MODEL
    mod = mod_ref[...]
    j = pl.program_id(0) % n_tiles
    is_ctx = j == 0

    def modnorm(t):
        return (_rms(t, NORM_EPS) * g_ref[...] * (1.0 + mod[:, d:2 * d]) + mod[:, 0:d]).astype(BF16)

    h16 = modnorm(jnp.where(is_ctx, xc_ref[...], xl_ref[...]))
    h_all = jnp.concatenate([h16, modnorm(jnp.concatenate([xp_ref[...], xn_ref[...]], axis=0))], axis=0)
    zc_s[...] = _dot(h_all, w_ref[:, ABD_COLS:ABD_COLS + C_COLS_PAD])
    cs, sn = cs_ref[...], sn_ref[...]
    csm, snm = csm_ref[...], snm_ref[...]
    ones = ones_ref[...]
    hd = HEAD_DIM
    q_scale = HEAD_DIM ** -0.5

    def project(lo, hi):
        return _dot(h16, w_ref[:, lo:hi])

    def put_heads(ref, base, slab):
        ref[base] = slab[:, :hd].astype(BF16)
        ref[base + 1] = slab[:, hd:].astype(BF16)

    def put_heads_t(ref, base, slab, ones_rows=ONES_ROWS):
        st = slab.T.astype(BF16)
        if ones_rows:
            one = jnp.ones((ones_rows, st.shape[1]), BF16)
            ref[base] = jnp.concatenate([st[:hd], one], axis=0)
            ref[base + 1] = jnp.concatenate([st[hd:], one], axis=0)
        else:
            ref[base] = st[:hd]
            ref[base + 1] = st[hd:]

    z_ref[:, SEC_B:SEC_B + SEC_W] = project(SEC_B, SEC_B + SEC_W)
    z_ref[:, SEC_A:SEC_A + SEC_W] = project(SEC_A, SEC_A + SEC_W)

    cq = _rms(z_ref[:, SEC_B:SEC_B + MLA_Q_RANK], NORM_EPS) * mqn_ref[...]
    q = _dot(cq.astype(BF16), wuq_ref[...])
    ckv = _rms(z_ref[:, SEC_B + MLA_Q_RANK:SEC_B + MLA_Q_RANK + MLA_KV_RANK], NORM_EPS) * mkvn_ref[...]
    kv = _dot(ckv.astype(BF16), wukv_ref[...])

    z_ref[:, SEC_D:SEC_D + SEC_W] = project(SEC_D, SEC_D + SEC_W)

    for s in range(2):
        x = z_ref[:, SEC_A + LANES * s:SEC_A + LANES * (s + 1)]
        x = _rope_slab(_head_norm_slab(x, qn_ref[...], ones), cs, sn, ROPE_SHIFT) * (q_scale * LOG2E)
        put_heads_t(qa_ref, 2 * s, x, 0)
    x = z_ref[:, SEC_A + GROUP_WIDTH:SEC_A + GROUP_WIDTH + GQA_KV_WIDTH]
    put_heads(ka_out, 0, _rope_slab(_head_norm_slab(x, kn_ref[...], ones), cs, sn, ROPE_SHIFT))
    put_heads_t(va_ref, 0, z_ref[:, SEC_A + GROUP_WIDTH + GQA_KV_WIDTH:SEC_A + SEC_W])

    kr = _rope_slab(z_ref[:, SEC_B + MLA_Q_RANK + MLA_KV_RANK:SEC_B + SEC_W], csm, snm, ROPE_SHIFT_MLA)
    b_scale = MLA_QK_DIM ** -0.5 * LOG2E
    for h in range(GROUP_HEADS):
        qh = q[:, LANES * h:LANES * (h + 1)]
        qb_ref[h] = (_rope_slab(qh, csm, snm, ROPE_SHIFT_MLA) * b_scale).T.astype(BF16)
        kb_ref[h] = (kv[:, LANES * h:LANES * (h + 1)] + kr).astype(BF16)
    for s in range(2):
        put_heads_t(vb_ref, 2 * s, kv[:, LANES * (GROUP_HEADS + s):LANES * (GROUP_HEADS + s + 1)])

    for s in range(2):
        x = z_ref[:, SEC_D + LANES * s:SEC_D + LANES * (s + 1)]
        put_heads(qd_ref, 2 * s, _rope_slab(x, cs, sn, ROPE_SHIFT) * q_scale)
    put_heads(kd_out, 0, _rope_slab(z_ref[:, SEC_D + GROUP_WIDTH:SEC_D + GROUP_WIDTH + GQA_KV_WIDTH], cs, sn, ROPE_SHIFT))
    put_heads(vd_ref, 0, z_ref[:, SEC_D + GROUP_WIDTH + GQA_KV_WIDTH:SEC_D + SEC_W])

    z = zc_s[0:TM, :]
    rows = lax.broadcasted_iota(jnp.int32, z.shape, 0)
    prev_row = jnp.where(j <= 1, 0.0, zc_s[TM + 7:TM + 8, :])
    next_row = jnp.where(jnp.logical_or(j == 0, j == n_tiles - 1), 0.0, zc_s[TM + 8:TM + 9, :])
    z_prev = jnp.where(rows == 0, prev_row, pltpu.roll(z, 1, axis=0))
    z_next = jnp.where(rows == TM - 1, next_row, pltpu.roll(z, TM - 1, axis=0))
    zs = z + mu_ref[...] * (0.5 * (z_prev + z_next) - z)
    gw = GROUP_WIDTH
    r, k, v = zs[:, 0:gw], zs[:, gw:2 * gw], zs[:, 2 * gw:3 * gw]
    lora = zs[:, 3 * gw:3 * gw + LANES]
    gl = zs[:, 3 * gw + LANES:3 * gw + LANES + gw]
    wt = jnp.tanh(lora)

    def seg_sum(t):
        return jnp.concatenate([_mm_exact_rhs(t[:, :LANES], ones), _mm_exact_rhs(t[:, LANES:], ones)], axis=-1)

    rr_ref[...] = r
    rv_ref[...] = v
    rg_ref[...] = _mm3(_sigmoid(gl), g2_ref[...])
    kd_sum = None
    for dr in range(2):
        u = w0_ref[dr] + _mm3(wt, w2_ref[dr])
        soft = jnp.maximum(-u, 0.0) + jnp.log(1.0 + jnp.exp(-jnp.abs(u)))
        lw_ref[dr] = -jnp.exp(-soft - 0.5)
        gate = _sigmoid(a0_ref[dr] + _mm3(lora, a2_ref[dr]))
        kk = k * kk_ref[dr]
        kk = kk / jnp.maximum(jnp.sqrt(seg_sum(kk * kk)), 1e-12)
        kd = k * (1.0 + (gate - 1.0) * ka_ref[dr])
        rkd_ref[dr] = kd
        ra_ref[dr] = -kk
        rb_ref[dr] = kk * gate
        kd_sum = kd if kd_sum is None else kd_sum + kd
    bonus_ref[...] = seg_sum(r * kd_sum * rk_ref[...]) * v


def _inproj_prep(xc, xl, xc_map, xl_map, mod3, tabs, p, mod_row, n, n_tiles):
    cs, sn, csm, snm = tabs
    cols = p['w_in_p'].shape[1]

    def tab_spec():
        return pl.BlockSpec((TM, LANES), lambda i: (i % n_tiles, 0))

    def row_spec(w):
        return pl.BlockSpec((1, w), lambda i: (0, 0))

    def heads_out(nh, w):
        return (pl.BlockSpec((nh, TM, w), lambda i: (0, i, 0)),
                jax.ShapeDtypeStruct((nh, n, w), BF16))

    def heads_out_t(nh, rows=HEAD_DIM + ONES_ROWS):
        return (pl.BlockSpec((nh, None, rows, TM), lambda i: (0, i, 0, 0)),
                jax.ShapeDtypeStruct((nh, n // TM, rows, TM), BF16))

    gw = GROUP_WIDTH
    dir_out = (pl.BlockSpec((2, TM, gw), lambda i: (0, i, 0)), jax.ShapeDtypeStruct((2, n, gw), F32))
    one_out = (pl.BlockSpec((TM, gw), lambda i: (i, 0)), jax.ShapeDtypeStruct((n, gw), F32))
    outs = [heads_out_t(4, HEAD_DIM), heads_out(2, 64), heads_out_t(2),
            heads_out_t(4, LANES), heads_out(4, 128), heads_out_t(4),
            heads_out(4, 64), heads_out(2, 64), heads_out(2, 64)] + [dir_out] * 4 + [one_out] * 4
    blk8 = TM // 8
    last8 = xl.shape[0] // 8 - 1

    def full(shape):
        nd = len(shape)
        return pl.BlockSpec(shape, lambda i: (0,) * nd)

    return pl.pallas_call(
        functools.partial(_inproj_prep_kernel, n_tiles=n_tiles),
        grid=(n // TM,),
        in_specs=[pl.BlockSpec((TM, D_MODEL), xc_map), pl.BlockSpec((TM, D_MODEL), xl_map),
                  pl.BlockSpec((8, D_MODEL), lambda i: (jnp.maximum(xl_map(i)[0] * blk8 - 1, 0), 0)),
                  pl.BlockSpec((8, D_MODEL), lambda i: (jnp.minimum((xl_map(i)[0] + 1) * blk8, last8), 0)),
                  pl.BlockSpec((None, 1, 6 * D_MODEL), lambda i: (mod_row(i), 0, 0)),
                  row_spec(D_MODEL),
                  pl.BlockSpec((D_MODEL, cols), lambda i: (0, 0)),
                  tab_spec(), tab_spec(), tab_spec(), tab_spec(),
                  row_spec(LANES), row_spec(LANES), row_spec(MLA_Q_RANK), row_spec(MLA_KV_RANK),
                  pl.BlockSpec((MLA_Q_RANK, GROUP_HEADS * LANES), lambda i: (0, 0)),
                  pl.BlockSpec((MLA_KV_RANK, GROUP_HEADS * LANES + GROUP_WIDTH), lambda i: (0, 0)),
                  pl.BlockSpec((LANES, LANES), lambda i: (0, 0)),
                  full((1, C_COLS_PAD)),
                  full((2, 1, gw)), full((2, LANES, gw)), full((2, 1, gw)), full((2, LANES, gw)),
                  full((2, 1, gw)), full((2, 1, gw)), full((1, gw)), full((gw, gw))],
        out_specs=[o[0] for o in outs],
        out_shape=[o[1] for o in outs],
        scratch_shapes=[pltpu.VMEM((TM, ABD_COLS), F32), pltpu.VMEM((TM + 16, C_COLS_PAD), F32)],
        compiler_params=_cparams(("parallel",)),
        name="inproj_prep",
    )(xc, xl, xl, xl, mod3, p['g_pre_mix'], p['w_in_p'], cs, sn, csm, snm,
      p['qn'], p['kn'], p['mqn'], p['mkvn'], p['wuq'], p['wukv'], p['ones64'],
      p['mu'], p['w0'], p['w2p'], p['a0'], p['a2p'], p['k_k'], p['k_a'], p['r_k'], p['g2p'])


def _flash_kernel(q_ref, k_ref, vt_ref, o_ref, *, shared_kv, tq, tk, ctx_len, s_tot):
    qi = pl.program_id(2)
    dv = vt_ref.shape[-2] - ONES_ROWS
    if shared_kv:
        streams = [(jnp.concatenate([q_ref[0], q_ref[1]], axis=1), 0)]
    else:
        streams = [(q_ref[0], 0), (q_ref[1], 1)]

    def scores(c):
        out = []
        for q, kv in streams:
            k = k_ref[kv, pl.ds(pl.multiple_of(c * tk, tk), tk), :]
            out.append(_dot(k, q))
        return out

    def softmax_pv(c, sts, carries):
        new = []
        for st, (_, kv), (m, acc) in zip(sts, streams, carries):
            m_new = jnp.maximum(m, jnp.max(st, axis=0, keepdims=True))
            pt = jnp.exp2(st - m_new).astype(BF16)
            acc = jnp.exp2(m - m_new) * acc + _dot(vt_ref[kv, c], pt)
            new.append((m_new, acc))
        return tuple(new)

    def run(n_chunks):
        unroll = next(u for u in (33, 11, 3, 2, 1) if n_chunks % u == 0)
        groups = n_chunks // unroll
        init = tuple((jnp.full((1, q.shape[1]), NEG_INF, F32), jnp.zeros((dv + ONES_ROWS, q.shape[1]), F32))
                     for q, _ in streams)

        def body(g, carries):
            queue = [scores(g * unroll + u) for u in range(min(LOOKAHEAD, unroll))]
            for u in range(unroll):
                if u + LOOKAHEAD < unroll:
                    queue.append(scores(g * unroll + u + LOOKAHEAD))
                carries = softmax_pv(g * unroll + u, queue.pop(0), carries)
            return carries

        carries = body(0, init) if groups == 1 else lax.fori_loop(0, groups, body, init)
        outs = [acc[:dv] / acc[dv:dv + 1] for _, acc in carries]
        if shared_kv:
            outs = [outs[0][:, :tq], outs[0][:, tq:]]
        o_ref[...] = jnp.concatenate(outs, axis=0).T.astype(o_ref.dtype)

    is_ctx = qi * tq < ctx_len

    @pl.when(is_ctx)
    def _():
        run(ctx_len // tk)

    @pl.when(jnp.logical_not(is_ctx))
    def _():
        run(s_tot // tk)


def _flash(qt, k, vt, *, shared_kv, batch, s_tot, ctx_len):
    nh, _, dk, _ = qt.shape
    n = k.shape[1]
    dv = vt.shape[-2] - ONES_ROWS
    nkv = 1 if shared_kv else 2
    tq = tk = TM
    nq = s_tot // tq
    kern = functools.partial(_flash_kernel, shared_kv=shared_kv, tq=tq, tk=tk,
                             ctx_len=ctx_len, s_tot=s_tot)
    return pl.pallas_call(
        kern,
        grid=(batch, nh // 2, nq),
        in_specs=[pl.BlockSpec((2, None, dk, tq), lambda b, p, i: (p, b * nq + i, 0, 0)),
                  pl.BlockSpec((nkv, s_tot, dk), lambda b, p, i: (p, b, 0)),
                  pl.BlockSpec((nkv, s_tot // tk, dv + ONES_ROWS, tk), lambda b, p, i: (p, b, 0, 0))],
        out_specs=pl.BlockSpec((tq, 2 * dv), lambda b, p, i: (b * nq + i, p)),
        out_shape=jax.ShapeDtypeStruct((n, nh * dv), BF16),
        compiler_params=_cparams(("parallel", "parallel", "arbitrary")),
        name="flash_shared" if shared_kv else "flash_split",
    )(qt, k, vt)


def _window_kernel(sink_ref, q_ref, k_ref, v_ref, o_ref, *, ctx_len, s_tot, n_blocks):
    w = WINDOW
    g = pl.program_id(1)
    first_lat = ctx_len // w
    last_blk = s_tot // w - 1
    r = lax.broadcasted_iota(jnp.int32, (2 * w, 3 * w), 0) % w
    c = lax.broadcasted_iota(jnp.int32, (2 * w, 3 * w), 1)
    row = lax.broadcasted_iota(jnp.int32, (2 * w, 1), 0)
    sk = jnp.where(row < w, sink_ref[2 * g], sink_ref[2 * g + 1])
    k_ctx = k_ref[0:ctx_len, :]
    v_ctx = v_ref[0:ctx_len, :]

    def rows(ref, blk):
        return ref[pl.ds(pl.multiple_of(blk * w, w), w), :]

    for i in range(n_blocks):
        jb = pl.program_id(2) * n_blocks + i
        is_lat = jb >= first_lat
        pb = jnp.clip(jb - 1, 0, last_blk)
        nb = jnp.clip(jb + 1, 0, last_blk)
        q = q_ref[:, w * i:w * (i + 1), :].reshape(2 * w, HEAD_DIM)
        kw = jnp.concatenate([rows(k_ref, pb), rows(k_ref, jb), rows(k_ref, nb)], axis=0)
        vw = jnp.concatenate([rows(v_ref, pb), rows(v_ref, jb), rows(v_ref, nb)], axis=0)
        s_w = _dot_nt(q, kw)
        s_c = _dot_nt(q, k_ctx)
        lo = jnp.where(is_lat, jnp.where(jb - 1 >= first_lat, 0, w), 3 * w)
        hi = jnp.where(jb + 1 <= last_blk, 3 * w - 1, 2 * w - 1)
        valid = jnp.logical_and(c >= jnp.maximum(r, lo), c <= jnp.minimum(r + 2 * w, hi))
        s_w = jnp.where(valid, s_w, NEG_INF)
        m = jnp.maximum(jnp.maximum(jnp.max(s_w, axis=-1, keepdims=True),
                                    jnp.max(s_c, axis=-1, keepdims=True)), sk)
        p_w = jnp.exp(s_w - m)
        p_c = jnp.exp(s_c - m)
        denom = jnp.exp(sk - m) + jnp.sum(p_w, axis=-1, keepdims=True) + jnp.sum(p_c, axis=-1, keepdims=True)
        o = (_dot(p_w.astype(BF16), vw) + _dot(p_c.astype(BF16), v_ctx)) / denom
        o_ref[w * i:w * (i + 1), :] = jnp.concatenate([o[:w], o[w:]], axis=-1).astype(o_ref.dtype)


def _window_attn(sink, q, k, v, *, batch, s_tot, ctx_len):
    nh, n, dk = q.shape
    nblk = s_tot // WINDOW
    wb = next(u for u in (WINDOW_BLOCKS, 6, 3, 2, 1) if nblk % u == 0)
    rows = WINDOW * wb
    nstep = nblk // wb
    kern = functools.partial(_window_kernel, ctx_len=ctx_len, s_tot=s_tot, n_blocks=wb)
    return pl.pallas_call(
        kern,
        grid=(batch, nh // 2, nstep),
        in_specs=[pl.BlockSpec(memory_space=pltpu.SMEM),
                  pl.BlockSpec((2, rows, dk), lambda b, g, j: (g, b * nstep + j, 0)),
                  pl.BlockSpec((None, s_tot, dk), lambda b, g, j: (g, b, 0)),
                  pl.BlockSpec((None, s_tot, dk), lambda b, g, j: (g, b, 0))],
        out_specs=pl.BlockSpec((rows, 2 * dk), lambda b, g, j: (b * nstep + j, g)),
        out_shape=jax.ShapeDtypeStruct((n, nh * dk), BF16),
        compiler_params=_cparams(("parallel", "parallel", "arbitrary")),
        name="window_attn",
    )(sink, q, k, v)


def _rwkv_scan_kernel(lw_ref, k_ref, a_ref, b_ref, r_ref, v_ref, y_ref, s_ref, *, reverse):
    @pl.when(pl.program_id(0) == 0)
    def _():
        s_ref[...] = jnp.zeros_like(s_ref)

    batch = lw_ref.shape[0]
    n_chunks = TM // CHUNK
    hd = HEAD_DIM
    row = lax.broadcasted_iota(jnp.int32, (TM, TM), 0)
    col = lax.broadcasted_iota(jnp.int32, (TM, TM), 1)
    same = (row // CHUNK) == (col // CHUNK)
    before = (col > row) if reverse else (col < row)
    m_strict = jnp.logical_and(same, before)
    m_incl = jnp.logical_and(same, jnp.logical_or(before, row == col))
    m_incl16 = m_incl.astype(BF16)
    eye = (row == col).astype(F32)
    assert CHUNK // SUB == 4
    same_sub = (row // SUB) == (col // SUB)

    items = [(b, h) for b in range(batch) for h in range(GROUP_HEADS)]
    idx = range(len(items))
    e_pos, at_all, rt_all, bt_all, kt_all, v_all = [], [], [], [], [], []
    for b in range(batch):
        lw = lw_ref[b]
        cum = _mm_exact_lhs(m_incl16, lw)
        e_pos.append(jnp.exp(cum))
        e_neg = jnp.exp(-cum)
        at_all.append(a_ref[b] * jnp.exp(cum - lw))
        rt_all.append(r_ref[b] * e_pos[b])
        bt_all.append(b_ref[b] * e_neg)
        kt_all.append(k_ref[b] * e_neg)
        v_all.append(v_ref[b])

    def head(arrs, b, h):
        return arrs[b][:, hd * h:hd * (h + 1)]

    at = [head(at_all, b, h) for b, h in items]
    rt = [head(rt_all, b, h) for b, h in items]
    bt = [head(bt_all, b, h) for b, h in items]
    at16 = [x.astype(BF16) for x in at]
    rt16 = [x.astype(BF16) for x in rt]
    bt16 = [x.astype(BF16) for x in bt]
    kt16 = [head(kt_all, b, h).astype(BF16) for b, h in items]
    v16 = [head(v_all, b, h).astype(BF16) for b, h in items]
    ab = [jnp.where(m_strict, _dot_nt(at16[i], bt16[i]), 0.0) for i in idx]
    ak16 = [jnp.where(m_strict, _dot_nt(at16[i], kt16[i]), 0.0).astype(BF16) for i in idx]
    rk16 = [jnp.where(m_incl, _dot_nt(rt16[i], kt16[i]), 0.0).astype(BF16) for i in idx]
    n_c = TM // CHUNK

    def wide(m):
        out = m[:CHUNK]
        for c in range(1, n_c):
            out = out + m[CHUNK * c:CHUNK * (c + 1)]
        return out

    def bdiag16(wm):
        return jnp.where(same, jnp.concatenate([wm] * n_c, axis=0), 0.0).astype(BF16)

    def wdot(wl, bd16):
        return _dot(wl.astype(BF16), bd16)

    eye_w = wide(eye)
    ld = [jnp.where(same_sub, x, 0.0) for x in ab]
    lo16 = [jnp.where(same_sub, 0.0, x).astype(BF16) for x in ab]
    pw_w = [wide(x) for x in ld]
    pw_bd = [x.astype(BF16) for x in ld]
    td_w = [eye_w + x for x in pw_w]
    for _ in range(int(math.log2(SUB)) - 1):
        pw_w = [wdot(x, y) for x, y in zip(pw_w, pw_bd)]
        pw_bd = [bdiag16(x) for x in pw_w]
        td_w = [t + wdot(t, y) for t, y in zip(td_w, pw_bd)]
    m1_w = [wdot(t, x) for t, x in zip(td_w, lo16)]
    m1_bd = [bdiag16(x) for x in m1_w]
    m2_w = [wdot(x, y) for x, y in zip(m1_w, m1_bd)]
    m3_w = [wdot(x, bdiag16(y)) for x, y in zip(m1_w, m2_w)]
    nn_w = [eye_w + a1 + a2 + a3 for a1, a2, a3 in zip(m1_w, m2_w, m3_w)]
    akv = [_dot(ak16[i], v16[i]) for i in idx]

    rowc = lax.broadcasted_iota(jnp.int32, (TM, n_c * 2 * hd), 0) // CHUNK
    colc = lax.broadcasted_iota(jnp.int32, (TM, n_c * 2 * hd), 1) // (2 * hd)
    same_x = rowc == colc

    def spread16(xr):
        return jnp.where(same_x, jnp.concatenate([xr] * n_c, axis=1), 0.0).astype(BF16)

    def unwide(xw):
        return jnp.concatenate([xw[:, 2 * hd * c:2 * hd * (c + 1)] for c in range(n_c)], axis=0)

    tx = [wdot(td_w[i], spread16(jnp.concatenate([at[i], akv[i]], axis=-1))) for i in idx]
    wu = [unwide(wdot(nn_w[i], spread16(unwide(tx[i])))) for i in idx]
    wa = [x[:, :hd] for x in wu]
    u016 = [x[:, hd:].astype(BF16) for x in wu]
    rb_w = [wide(jnp.where(m_incl, _dot_nt(rt16[i], bt16[i]), 0.0)) for i in idx]
    rbwu = [unwide(wdot(rb_w[i], spread16(wu[i]))) for i in idx]
    yr16 = [(rt[i] + rbwu[i][:, :hd]).astype(BF16) for i in idx]
    y0 = [rbwu[i][:, hd:] + _dot(rk16[i], v16[i]) for i in idx]

    order = range(n_chunks - 1, -1, -1) if reverse else range(n_chunks)
    eye_h = eye[:hd, :hd]
    trans = {}
    for c in order:
        rs = slice(CHUNK * c, CHUNK * (c + 1))
        last = CHUNK * c if reverse else CHUNK * (c + 1) - 1
        for i, (b, h) in enumerate(items):
            decay = e_pos[b][last:last + 1, hd * h:hd * (h + 1)]
            g = (eye_h + _mm3tn(wa[i][rs], bt[i][rs])) * decay
            hh = (_dot_tn(u016[i][rs], bt16[i][rs]) + _dot_tn(v16[i][rs], kt16[i][rs])) * decay
            trans[c, i] = (g, hh)
    for c in order:
        rs = slice(CHUNK * c, CHUNK * (c + 1))
        s0 = [s_ref[b, h] for b, h in items]
        for i, (b, h) in enumerate(items):
            s_ref[b, h] = _mm3(s0[i], trans[c, i][0]) + trans[c, i][1]
        for b in range(batch):
            y_ref[b, rs, :] = jnp.concatenate(
                [_dot_nt(yr16[i][rs], s0[i].astype(BF16)) + y0[i][rs] for i, (bb, _) in enumerate(items) if bb == b],
                axis=-1)


def _mm3tn(a, b):
    ah, al = _split2(a)
    bh, bl = _split2(b)
    return _dot_tn(ah, bh) + (_dot_tn(ah, bl) + _dot_tn(al, bh))


def _rwkv_scan(lw, kd, a, b, r, v, *, d, batch, n_tiles):
    n, gw = r.shape
    s_tot = n // batch
    reverse = d == 1

    def tile(s):
        if reverse:
            return jnp.where(s == 0, 0, n_tiles - s)
        return s

    dspec = pl.BlockSpec((None, batch, TM, gw), lambda s: (d, 0, tile(s), 0))
    spec = pl.BlockSpec((batch, TM, gw), lambda s: (0, tile(s), 0))
    per_dir = [x.reshape(2, batch, s_tot, gw) for x in (lw, kd, a, b)]
    shared = [x.reshape(batch, s_tot, gw) for x in (r, v)]
    y = pl.pallas_call(
        functools.partial(_rwkv_scan_kernel, reverse=reverse),
        grid=(n_tiles,),
        in_specs=[dspec, dspec, dspec, dspec, spec, spec],
        out_specs=spec,
        out_shape=jax.ShapeDtypeStruct((batch, s_tot, gw), F32),
        scratch_shapes=[pltpu.VMEM((batch, GROUP_HEADS, HEAD_DIM, HEAD_DIM), F32)],
        compiler_params=_cparams(("arbitrary",)),
        name="rwkv_scan_rev" if reverse else "rwkv_scan_fwd",
    )(*per_dir, *shared)
    return y.reshape(n, gw)


def _out_mlp_kernel(xc_ref, xl_ref, oa_ref, ob_ref, od_ref, yf_ref, yr_ref, bonus_ref, g_ref,
                    lnw_ref, lnb_ref, ones_ref, mod_ref, gpost_ref, gpre_ref, gpm_ref,
                    wo_ref, w1_ref, w2_ref, xo_ref, *, ctx_first):
    d = D_MODEL
    gw = GROUP_WIDTH
    mod = mod_ref[...]
    ones = ones_ref[...]

    def seg_mean(t):
        sm = jnp.concatenate([_mm_exact_rhs(t[:, :LANES], ones), _mm_exact_rhs(t[:, LANES:], ones)], axis=-1)
        return sm * (1.0 / HEAD_DIM)

    yy = yf_ref[...] + yr_ref[...]
    dlt = yy - seg_mean(yy)
    yn = dlt * lax.rsqrt(seg_mean(dlt * dlt) + RWKV_GN_EPS) * lnw_ref[...] + lnb_ref[...]
    oc = ((yn + bonus_ref[...]) * g_ref[...]).astype(BF16)

    y = _dot(oa_ref[...], wo_ref[0:gw, :])
    y += _dot(ob_ref[...], wo_ref[gw:2 * gw, :])
    y += _dot(oc, wo_ref[2 * gw:3 * gw, :])
    y += _dot(od_ref[...], wo_ref[3 * gw:4 * gw, :])
    x = jnp.where(jnp.logical_and(ctx_first, pl.program_id(1) == 0), xc_ref[...], xl_ref[...])
    x1 = x + mod[:, 2 * d:3 * d] * (_rms(y, NORM_EPS) * gpost_ref[...])
    h = _rms(x1, NORM_EPS) * gpre_ref[...] * (1.0 + mod[:, 4 * d:5 * d]) + mod[:, 3 * d:4 * d]
    u = jnp.maximum(_dot(h.astype(BF16), w1_ref[...]), 0.0)
    zz = _dot((u * u).astype(BF16), w2_ref[...])
    xo_ref[...] = x1 + mod[:, 5 * d:6 * d] * (_rms(zz, NORM_EPS) * gpm_ref[...])


def _out_mlp(xc, xl, xc_map, xl_map, oa, ob, od, yf, yr, bonus, rg, mod3, p, *, batch, n_tiles, skip_ctx):
    d = D_MODEL
    gw = GROUP_WIDTH
    off = 1 if skip_ctx else 0
    nt = n_tiles - off

    def tile(b, j):
        return (b * n_tiles + off + j, 0)

    def mrow(b, j):
        return (jnp.where(off + j == 0, batch, b), 0, 0)

    ospec = pl.BlockSpec((TM, gw), tile)
    row = pl.BlockSpec((1, d), lambda b, j: (0, 0))
    grow = pl.BlockSpec((1, gw), lambda b, j: (0, 0))

    def wspec(shape):
        return pl.BlockSpec(shape, lambda b, j: (0, 0), pipeline_mode=pl.Buffered(1))

    return pl.pallas_call(
        functools.partial(_out_mlp_kernel, ctx_first=not skip_ctx),
        grid=(batch, nt),
        in_specs=[pl.BlockSpec((TM, d), lambda b, j: xc_map(b, off + j)),
                  pl.BlockSpec((TM, d), lambda b, j: xl_map(b, off + j)),
                  ospec, ospec, ospec, ospec, ospec, ospec, ospec,
                  grow, grow, pl.BlockSpec((LANES, LANES), lambda b, j: (0, 0)),
                  pl.BlockSpec((None, 1, 6 * d), mrow),
                  row, row, row,
                  wspec((d, d)), wspec((d, D_FF)), wspec((D_FF, d))],
        out_specs=pl.BlockSpec((TM, d), lambda b, j: (b * nt + j, 0)),
        out_shape=jax.ShapeDtypeStruct((batch * nt * TM, d), F32),
        compiler_params=_cparams(("parallel", "parallel")),
        name="out_mlp",
    )(xc, xl, oa, ob, od, yf, yr, bonus, rg, p['ln_w'], p['ln_b'], p['ones64'], mod3,
      p['g_post_mix'], p['g_pre_mlp'], p['g_post_mlp'], p['w_out'], p['w_mlp1'], p['w_mlp2'])


def _rope_tables(seq, ctx_len):
    t = jnp.arange(seq, dtype=jnp.int32)
    row = (t // GRID_W).astype(F32)
    col = (t % GRID_W).astype(F32)

    def tables(rot_dim, lane_dim):
        n = rot_dim // 4
        inv = ROPE_THETA ** (-jnp.arange(n, dtype=F32) / n)
        ar, ac = row[:, None] * inv, col[:, None] * inv
        cos = jnp.concatenate([jnp.cos(ar), jnp.cos(ar), jnp.cos(ac), jnp.cos(ac)], axis=-1)
        sin = jnp.concatenate([-jnp.sin(ar), jnp.sin(ar), -jnp.sin(ac), jnp.sin(ac)], axis=-1)
        return cos, sin

    def with_ctx(tab, fill):
        return jnp.concatenate([jnp.full((ctx_len, tab.shape[1]), fill, F32), tab], axis=0)

    cos, sin = tables(HEAD_DIM, LANES)
    cs = with_ctx(jnp.tile(cos, (1, LANES // HEAD_DIM)), 1.0)
    sn = with_ctx(jnp.tile(sin, (1, LANES // HEAD_DIM)), 0.0)
    cosm, sinm = tables(MLA_ROPE_DIM, LANES)
    pad_l, pad_r = MLA_NOPE_DIM, LANES - MLA_QK_DIM
    csm = with_ctx(jnp.pad(cosm, ((0, 0), (pad_l, pad_r)), constant_values=1.0), 1.0)
    snm = with_ctx(jnp.pad(sinm, ((0, 0), (pad_l, pad_r))), 0.0)
    return cs, sn, csm, snm


def _layer_params(l, w):
    d = D_MODEL
    gw = GROUP_WIDTH
    w_in = w['w_in'][l]
    a_end, b_end, c_end = A_COLS, A_COLS + B_COLS, A_COLS + B_COLS + C_COLS
    lowrank = MLA_Q_RANK + MLA_KV_RANK
    zeros = lambda c: jnp.zeros((d, c), F32)
    w_in_p = jnp.concatenate([
        w_in[:, :a_end],
        w_in[:, a_end:a_end + lowrank], zeros(MLA_NOPE_DIM), w_in[:, a_end + lowrank:b_end], zeros(LANES - MLA_QK_DIM),
        w_in[:, c_end:],
        w_in[:, b_end:c_end], zeros(C_COLS_PAD - C_COLS)], axis=1).astype(BF16)
    wuq = w['mla_w_uq'][l].reshape(MLA_Q_RANK, GROUP_HEADS, MLA_QK_DIM)
    wuq = jnp.pad(wuq, ((0, 0), (0, 0), (0, LANES - MLA_QK_DIM))).reshape(MLA_Q_RANK, GROUP_HEADS * LANES)
    wukv = w['mla_w_ukv'][l].reshape(MLA_KV_RANK, GROUP_HEADS, MLA_NOPE_DIM + HEAD_DIM)
    wk = jnp.pad(wukv[:, :, :MLA_NOPE_DIM], ((0, 0), (0, 0), (0, LANES - MLA_NOPE_DIM)))
    wukv_p = jnp.concatenate([wk.reshape(MLA_KV_RANK, GROUP_HEADS * LANES),
                              wukv[:, :, MLA_NOPE_DIM:].reshape(MLA_KV_RANK, gw)], axis=1)
    lane = jnp.arange(LANES)
    ones64 = ((lane[:, None] // HEAD_DIM) == (lane[None, :] // HEAD_DIM)).astype(BF16)
    zrows = lambda r: jnp.zeros((2, r, gw), F32)
    return {
        'w_in_p': w_in_p,
        'g_pre_mix': w['g_pre_mix'][l].reshape(1, d),
        'g_post_mix': w['g_post_mix'][l].reshape(1, d),
        'g_pre_mlp': w['g_pre_mlp'][l].reshape(1, d),
        'g_post_mlp': w['g_post_mlp'][l].reshape(1, d),
        'qn': jnp.tile(w['gqa_q_norm'][l], 2).reshape(1, LANES),
        'kn': jnp.tile(w['gqa_k_norm'][l], 2).reshape(1, LANES),
        'mqn': w['mla_q_norm'][l].reshape(1, MLA_Q_RANK),
        'mkvn': w['mla_kv_norm'][l].reshape(1, MLA_KV_RANK),
        'wuq': wuq.astype(BF16),
        'wukv': wukv_p.astype(BF16),
        'ones64': ones64,
        'mu': jnp.pad(w['rwkv_mu'][l], (0, C_COLS_PAD - C_COLS)).reshape(1, C_COLS_PAD),
        'w0': w['rwkv_w0'][l].reshape(2, 1, gw),
        'w2p': jnp.concatenate([w['rwkv_w2'][l], zrows(LANES - RWKV_DECAY_LORA)], axis=1),
        'a0': w['rwkv_a0'][l].reshape(2, 1, gw),
        'a2p': jnp.concatenate([zrows(RWKV_DECAY_LORA), w['rwkv_a2'][l],
                                zrows(LANES - RWKV_DECAY_LORA - RWKV_ICLR_LORA)], axis=1),
        'k_k': w['rwkv_k_k'][l].reshape(2, 1, gw),
        'k_a': w['rwkv_k_a'][l].reshape(2, 1, gw),
        'r_k': w['rwkv_r_k'][l].reshape(1, gw),
        'g2p': jnp.pad(w['rwkv_g2'][l], ((0, gw - RWKV_GATE_LORA), (0, 0))),
        'ln_w': w['rwkv_ln_w'][l].reshape(1, gw),
        'ln_b': w['rwkv_ln_b'][l].reshape(1, gw),
        'sink': w['swa_sink'][l],
        'w_out': w['w_out'][l].astype(BF16),
        'w_mlp1': w['w_mlp1'][l].astype(BF16),
        'w_mlp2': w['w_mlp2'][l].astype(BF16),
    }


def kernel(x, c, ctx, c_ctx, w_mod, b_mod, g_pre_mix, g_post_mix, g_pre_mlp, g_post_mlp, w_in, gqa_q_norm, gqa_k_norm, mla_q_norm, mla_kv_norm, mla_w_uq, mla_w_ukv, rwkv_mu, rwkv_w0, rwkv_w2, rwkv_a0, rwkv_a2, rwkv_k_k, rwkv_k_a, rwkv_r_k, rwkv_g2, rwkv_ln_w, rwkv_ln_b, swa_sink, w_out, w_mlp1, w_mlp2):
    batch, seq, d = x.shape
    ctx_len = ctx.shape[1]
    depth = w_mod.shape[0]
    assert d == D_MODEL and ctx_len == TM and seq % TM == 0 and seq % GRID_W == 0
    assert batch + 1 <= 8
    s_tot = ctx_len + seq
    n_tiles = s_tot // TM
    w = dict(w_in=w_in, g_pre_mix=g_pre_mix, g_post_mix=g_post_mix, g_pre_mlp=g_pre_mlp,
             g_post_mlp=g_post_mlp, gqa_q_norm=gqa_q_norm, gqa_k_norm=gqa_k_norm, mla_q_norm=mla_q_norm,
             mla_kv_norm=mla_kv_norm, mla_w_uq=mla_w_uq, mla_w_ukv=mla_w_ukv, rwkv_mu=rwkv_mu,
             rwkv_w0=rwkv_w0, rwkv_w2=rwkv_w2, rwkv_a0=rwkv_a0, rwkv_a2=rwkv_a2, rwkv_k_k=rwkv_k_k,
             rwkv_k_a=rwkv_k_a, rwkv_r_k=rwkv_r_k, rwkv_g2=rwkv_g2, rwkv_ln_w=rwkv_ln_w,
             rwkv_ln_b=rwkv_ln_b, swa_sink=swa_sink, w_out=w_out, w_mlp1=w_mlp1, w_mlp2=w_mlp2)

    ct = jnp.concatenate([c, c_ctx[None, :], jnp.zeros((8 - batch - 1, d), F32)], axis=0).T
    mods = _modulation(ct, w_mod, b_mod, batch + 1)

    def mod_row(i):
        return jnp.where(i % n_tiles == 0, batch, i // n_tiles)

    tabs = _rope_tables(seq, ctx_len)
    geo = dict(batch=batch, s_tot=s_tot, ctx_len=ctx_len)
    n = batch * s_tot
    n_lat = n_tiles - 1
    xc, xl = ctx.reshape(batch * ctx_len, d), x.reshape(batch * seq, d)
    xc_map = lambda b, j: (b, 0)
    xl_map = lambda b, j: (b * n_lat + jnp.maximum(j - 1, 0), 0)
    for l in range(depth):
        p = _layer_params(l, w)
        mod3 = mods[l].reshape(8, 1, 6 * d)
        flat = lambda m: (lambda i: m(i // n_tiles, i % n_tiles))
        qa, ka, va, qb, kb, vb, qd, kd, vd, lw, rkd, ra, rb, rr, rv, rg, bonus = _inproj_prep(
            xc, xl, flat(xc_map), flat(xl_map), mod3, tabs, p, mod_row, n, n_tiles)
        oa = _flash(qa, ka, va, shared_kv=True, **geo)
        ob = _flash(qb, kb, vb, shared_kv=False, **geo)
        od = _window_attn(p['sink'], qd, kd, vd, **geo)
        yf = _rwkv_scan(lw, rkd, ra, rb, rr, rv, d=0, batch=batch, n_tiles=n_tiles)
        yr = _rwkv_scan(lw, rkd, ra, rb, rr, rv, d=1, batch=batch, n_tiles=n_tiles)
        xs = _out_mlp(xc, xl, xc_map, xl_map, oa, ob, od, yf, yr, bonus, rg, mod3, p,
                      batch=batch, n_tiles=n_tiles, skip_ctx=l == depth - 1)
        xc = xl = xs
        xc_map = lambda b, j: (b * n_tiles, 0)
        xl_map = lambda b, j: (b * n_tiles + j, 0)
    return xs.reshape(batch, seq, d)
```

```python
import functools
import math

import jax
import jax.numpy as jnp
from jax import lax
from jax.experimental import pallas as pl
from jax.experimental.pallas import tpu as pltpu

F32 = jnp.float32
BF16 = jnp.bfloat16

D_MODEL = 1024
GRID_W = 64
HEAD_DIM = 64
GROUP_WIDTH = 256
GROUP_HEADS = 4
ROPE_THETA = 10000.0
NORM_EPS = 1e-6
NEG_INF = -1e30
MLA_Q_RANK = 256
MLA_KV_RANK = 128
MLA_NOPE_DIM = 64
MLA_ROPE_DIM = 32
MLA_QK_DIM = MLA_NOPE_DIM + MLA_ROPE_DIM
RWKV_DECAY_LORA = 64
RWKV_ICLR_LORA = 64
RWKV_GATE_LORA = 160
RWKV_GN_EPS = 64e-5
WINDOW = 128
D_FF = 4 * D_MODEL

LOG2E = math.log2(math.e)
LANES = 128
TM = 256
ONES_ROWS = 8
LOOKAHEAD = 2
WINDOW_BLOCKS = 11
CHUNK = 64
SUB = 16
GQA_KV_WIDTH = 2 * HEAD_DIM
A_COLS = GROUP_WIDTH + 2 * GQA_KV_WIDTH
B_COLS = MLA_Q_RANK + MLA_KV_RANK + MLA_ROPE_DIM
C_COLS = 3 * GROUP_WIDTH + RWKV_DECAY_LORA + RWKV_ICLR_LORA + RWKV_GATE_LORA
SEC_W = 512
SEC_A = 0
SEC_B = SEC_A + SEC_W
SEC_D = SEC_B + SEC_W
ABD_COLS = SEC_D + SEC_W
C_COLS_PAD = 1152
MOD_BLOCK = 1536
ROPE_SHIFT = HEAD_DIM // 4
ROPE_SHIFT_MLA = MLA_ROPE_DIM // 4
VMEM_LIMIT = 56 * 1024 * 1024


def _cparams(sem):
    return pltpu.CompilerParams(dimension_semantics=sem, vmem_limit_bytes=VMEM_LIMIT)


def _dot(a, b):
    return jnp.dot(a, b, preferred_element_type=F32)


def _dot_nt(a, b):
    return lax.dot_general(a, b, (((1,), (1,)), ((), ())), preferred_element_type=F32)


def _dot_tn(a, b):
    return lax.dot_general(a, b, (((0,), (0,)), ((), ())), preferred_element_type=F32)


def _split2(x):
    hi = x.astype(BF16)
    lo = (x - hi.astype(F32)).astype(BF16)
    return hi, lo


def _mm_exact_rhs(a, b_bf16):
    hi, lo = _split2(a)
    return _dot(hi, b_bf16) + _dot(lo, b_bf16)


def _mm_exact_lhs(a_bf16, b):
    hi, lo = _split2(b)
    return _dot(a_bf16, hi) + _dot(a_bf16, lo)


def _mm3(a, b):
    ah, al = _split2(a)
    bh, bl = _split2(b)
    return _dot(ah, bh) + (_dot(ah, bl) + _dot(al, bh))


def _sigmoid(x):
    return 1.0 / (1.0 + jnp.exp(-x))


def _rms(x, eps):
    return x * lax.rsqrt(jnp.mean(x * x, axis=-1, keepdims=True) + eps)


def _mod_kernel(ct_ref, w_ref, b_ref, o_ref, *, n_rows):
    ct = ct_ref[...]
    st = ct * _sigmoid(ct)
    w = w_ref[...]
    rows = [jnp.sum(st[:, r:r + 1] * w, axis=0, keepdims=True) for r in range(n_rows)]
    rows.append(jnp.zeros((8 - n_rows, w.shape[1]), F32))
    o_ref[...] = jnp.concatenate(rows, axis=0) + b_ref[...]


def _modulation(ct, w_mod, b_mod, n_rows):
    depth, d, n6 = w_mod.shape
    tn = MOD_BLOCK
    return pl.pallas_call(
        functools.partial(_mod_kernel, n_rows=n_rows),
        grid=(depth, n6 // tn),
        in_specs=[pl.BlockSpec((d, 8), lambda l, j: (0, 0)),
                  pl.BlockSpec((None, d, tn), lambda l, j: (l, 0, j)),
                  pl.BlockSpec((None, 1, tn), lambda l, j: (l, 0, j))],
        out_specs=pl.BlockSpec((None, 8, tn), lambda l, j: (l, 0, j)),
        out_shape=jax.ShapeDtypeStruct((depth, 8, n6), F32),
        compiler_params=_cparams(("parallel", "parallel")),
        name="modulation",
    )(ct, w_mod, b_mod.reshape(depth, 1, n6))


def _rope_slab(x, cos, sin_signed, shift):
    left = pltpu.roll(x, LANES - shift, axis=1)
    right = pltpu.roll(x, shift, axis=1)
    lane = lax.broadcasted_iota(jnp.int32, x.shape, 1)
    first = ((lane // shift) % 2) == 0
    return x * cos + jnp.where(first, left, right) * sin_signed


def _head_norm_slab(x, gain, ones):
    ss = _mm_exact_rhs(x * x, ones)
    return x * lax.rsqrt(ss * (1.0 / HEAD_DIM) + NORM_EPS) * gain


def _inproj_prep_kernel(xc_ref, xl_ref, xp_ref, xn_ref, mod_ref, g_ref, w_ref,
                        cs_ref, sn_ref, csm_ref, snm_ref, qn_ref, kn_ref, mqn_ref, mkvn_ref,
                        wuq_ref, wukv_ref, ones_ref,
                        mu_ref, w0_ref, w2_ref, a0_ref, a2_ref, kk_ref, ka_ref, rk_ref, g2_ref,
                        qa_ref, ka_out, va_ref, qb_ref, kb_ref, vb_ref, qd_ref, kd_out, vd_ref,
                        lw_ref, rkd_ref, ra_ref, rb_ref, rr_ref, rv_ref, rg_ref, bonus_ref,
                        z_ref, zc_s, *, n_tiles):
    d = D_MODEL
    mod = mod_ref[...]
    j = pl.program_id(0) % n_tiles
    is_ctx = j == 0

    def modnorm(t):
        return (_rms(t, NORM_EPS) * g_ref[...] * (1.0 + mod[:, d:2 * d]) + mod[:, 0:d]).astype(BF16)

    h16 = modnorm(jnp.where(is_ctx, xc_ref[...], xl_ref[...]))
    h_all = jnp.concatenate([h16, modnorm(jnp.concatenate([xp_ref[...], xn_ref[...]], axis=0))], axis=0)
    zc_s[...] = _dot(h_all, w_ref[:, ABD_COLS:ABD_COLS + C_COLS_PAD])
    cs, sn = cs_ref[...], sn_ref[...]
    csm, snm = csm_ref[...], snm_ref[...]
    ones = ones_ref[...]
    hd = HEAD_DIM
    q_scale = HEAD_DIM ** -0.5

    def project(lo, hi):
        return _dot(h16, w_ref[:, lo:hi])

    def put_heads(ref, base, slab):
        ref[base] = slab[:, :hd].astype(BF16)
        ref[base + 1] = slab[:, hd:].astype(BF16)

    def put_heads_t(ref, base, slab, ones_rows=ONES_ROWS):
        st = slab.T.astype(BF16)
        if ones_rows:
            one = jnp.ones((ones_rows, st.shape[1]), BF16)
            ref[base] = jnp.concatenate([st[:hd], one], axis=0)
            ref[base + 1] = jnp.concatenate([st[hd:], one], axis=0)
        else:
            ref[base] = st[:hd]
            ref[base + 1] = st[hd:]

    z_ref[:, SEC_B:SEC_B + SEC_W] = project(SEC_B, SEC_B + SEC_W)
    z_ref[:, SEC_A:SEC_A + SEC_W] = project(SEC_A, SEC_A + SEC_W)

    cq = _rms(z_ref[:, SEC_B:SEC_B + MLA_Q_RANK], NORM_EPS) * mqn_ref[...]
    q = _dot(cq.astype(BF16), wuq_ref[...])
    ckv = _rms(z_ref[:, SEC_B + MLA_Q_RANK:SEC_B + MLA_Q_RANK + MLA_KV_RANK], NORM_EPS) * mkvn_ref[...]
    kv = _dot(ckv.astype(BF16), wukv_ref[...])

    z_ref[:, SEC_D:SEC_D + SEC_W] = project(SEC_D, SEC_D + SEC_W)

    for s in range(2):
        x = z_ref[:, SEC_A + LANES * s:SEC_A + LANES * (s + 1)]
        x = _rope_slab(_head_norm_slab(x, qn_ref[...], ones), cs, sn, ROPE_SHIFT) * (q_scale * LOG2E)
        put_heads_t(qa_ref, 2 * s, x, 0)
    x = z_ref[:, SEC_A + GROUP_WIDTH:SEC_A + GROUP_WIDTH + GQA_KV_WIDTH]
    put_heads(ka_out, 0, _rope_slab(_head_norm_slab(x, kn_ref[...], ones), cs, sn, ROPE_SHIFT))
    put_heads_t(va_ref, 0, z_ref[:, SEC_A + GROUP_WIDTH + GQA_KV_WIDTH:SEC_A + SEC_W])

    kr = _rope_slab(z_ref[:, SEC_B + MLA_Q_RANK + MLA_KV_RANK:SEC_B + SEC_W], csm, snm, ROPE_SHIFT_MLA)
    b_scale = MLA_QK_DIM ** -0.5 * LOG2E
    for h in range(GROUP_HEADS):
        qh = q[:, LANES * h:LANES * (h + 1)]
        qb_ref[h] = (_rope_slab(qh, csm, snm, ROPE_SHIFT_MLA) * b_scale).T.astype(BF16)
        kb_ref[h] = (kv[:, LANES * h:LANES * (h + 1)] + kr).astype(BF16)
    for s in range(2):
        put_heads_t(vb_ref, 2 * s, kv[:, LANES * (GROUP_HEADS + s):LANES * (GROUP_HEADS + s + 1)])

    def put_chunks_t(ref, base, slab, ones_rows):
        st = slab.T.astype(BF16)
        for hh in range(2):
            for cc in range(TM // WINDOW):
                blk = st[hd * hh:hd * (hh + 1), WINDOW * cc:WINDOW * (cc + 1)]
                if ones_rows:
                    blk = jnp.concatenate([blk, jnp.ones((ones_rows, WINDOW), BF16)], axis=0)
                ref[base + hh, cc] = blk

    for s in range(2):
        x = z_ref[:, SEC_D + LANES * s:SEC_D + LANES * (s + 1)]
        put_chunks_t(qd_ref, 2 * s, _rope_slab(x, cs, sn, ROPE_SHIFT) * q_scale, 0)
    put_heads(kd_out, 0, _rope_slab(z_ref[:, SEC_D + GROUP_WIDTH:SEC_D + GROUP_WIDTH + GQA_KV_WIDTH], cs, sn, ROPE_SHIFT))
    put_chunks_t(vd_ref, 0, z_ref[:, SEC_D + GROUP_WIDTH + GQA_KV_WIDTH:SEC_D + SEC_W], ONES_ROWS)

    z = zc_s[0:TM, :]
    rows = lax.broadcasted_iota(jnp.int32, z.shape, 0)
    prev_row = jnp.where(j <= 1, 0.0, zc_s[TM + 7:TM + 8, :])
    next_row = jnp.where(jnp.logical_or(j == 0, j == n_tiles - 1), 0.0, zc_s[TM + 8:TM + 9, :])
    z_prev = jnp.where(rows == 0, prev_row, pltpu.roll(z, 1, axis=0))
    z_next = jnp.where(rows == TM - 1, next_row, pltpu.roll(z, TM - 1, axis=0))
    zs = z + mu_ref[...] * (0.5 * (z_prev + z_next) - z)
    gw = GROUP_WIDTH
    r, k, v = zs[:, 0:gw], zs[:, gw:2 * gw], zs[:, 2 * gw:3 * gw]
    lora = zs[:, 3 * gw:3 * gw + LANES]
    gl = zs[:, 3 * gw + LANES:3 * gw + LANES + gw]
    wt = jnp.tanh(lora)

    def seg_sum(t):
        return jnp.concatenate([_mm_exact_rhs(t[:, :LANES], ones), _mm_exact_rhs(t[:, LANES:], ones)], axis=-1)

    rr_ref[...] = r
    rv_ref[...] = v
    rg_ref[...] = _mm3(_sigmoid(gl), g2_ref[...])
    kd_sum = None
    for dr in range(2):
        u = w0_ref[dr] + _mm3(wt, w2_ref[dr])
        soft = jnp.maximum(-u, 0.0) + jnp.log(1.0 + jnp.exp(-jnp.abs(u)))
        lw_ref[dr] = -jnp.exp(-soft - 0.5)
        gate = _sigmoid(a0_ref[dr] + _mm3(lora, a2_ref[dr]))
        kk = k * kk_ref[dr]
        kk = kk / jnp.maximum(jnp.sqrt(seg_sum(kk * kk)), 1e-12)
        kd = k * (1.0 + (gate - 1.0) * ka_ref[dr])
        rkd_ref[dr] = kd
        ra_ref[dr] = -kk
        rb_ref[dr] = kk * gate
        kd_sum = kd if kd_sum is None else kd_sum + kd
    bonus_ref[...] = seg_sum(r * kd_sum * rk_ref[...]) * v


def _inproj_prep(xc, xl, xc_map, xl_map, mod3, tabs, p, mod_row, n, n_tiles):
    cs, sn, csm, snm = tabs
    cols = p['w_in_p'].shape[1]

    def tab_spec():
        return pl.BlockSpec((TM, LANES), lambda i: (i % n_tiles, 0))

    def row_spec(w):
        return pl.BlockSpec((1, w), lambda i: (0, 0))

    def heads_out(nh, w):
        return (pl.BlockSpec((nh, TM, w), lambda i: (0, i, 0)),
                jax.ShapeDtypeStruct((nh, n, w), BF16))

    def heads_out_t(nh, rows=HEAD_DIM + ONES_ROWS):
        return (pl.BlockSpec((nh, None, rows, TM), lambda i: (0, i, 0, 0)),
                jax.ShapeDtypeStruct((nh, n // TM, rows, TM), BF16))

    def chunks_out_t(nh, rows):
        cpt = TM // WINDOW
        return (pl.BlockSpec((nh, cpt, rows, WINDOW), lambda i: (0, i, 0, 0)),
                jax.ShapeDtypeStruct((nh, n // WINDOW, rows, WINDOW), BF16))

    gw = GROUP_WIDTH
    dir_out = (pl.BlockSpec((2, TM, gw), lambda i: (0, i, 0)), jax.ShapeDtypeStruct((2, n, gw), F32))
    one_out = (pl.BlockSpec((TM, gw), lambda i: (i, 0)), jax.ShapeDtypeStruct((n, gw), F32))
    outs = [heads_out_t(4, HEAD_DIM), heads_out(2, 64), heads_out_t(2),
            heads_out_t(4, LANES), heads_out(4, 128), heads_out_t(4),
            chunks_out_t(4, HEAD_DIM), heads_out(2, 64), chunks_out_t(2, HEAD_DIM + ONES_ROWS)] + [dir_out] * 4 + [one_out] * 4
    blk8 = TM // 8
    last8 = xl.shape[0] // 8 - 1

    def full(shape):
        nd = len(shape)
        return pl.BlockSpec(shape, lambda i: (0,) * nd)

    return pl.pallas_call(
        functools.partial(_inproj_prep_kernel, n_tiles=n_tiles),
        grid=(n // TM,),
        in_specs=[pl.BlockSpec((TM, D_MODEL), xc_map), pl.BlockSpec((TM, D_MODEL), xl_map),
                  pl.BlockSpec((8, D_MODEL), lambda i: (jnp.maximum(xl_map(i)[0] * blk8 - 1, 0), 0)),
                  pl.BlockSpec((8, D_MODEL), lambda i: (jnp.minimum((xl_map(i)[0] + 1) * blk8, last8), 0)),
                  pl.BlockSpec((None, 1, 6 * D_MODEL), lambda i: (mod_row(i), 0, 0)),
                  row_spec(D_MODEL),
                  pl.BlockSpec((D_MODEL, cols), lambda i: (0, 0)),
                  tab_spec(), tab_spec(), tab_spec(), tab_spec(),
                  row_spec(LANES), row_spec(LANES), row_spec(MLA_Q_RANK), row_spec(MLA_KV_RANK),
                  pl.BlockSpec((MLA_Q_RANK, GROUP_HEADS * LANES), lambda i: (0, 0)),
                  pl.BlockSpec((MLA_KV_RANK, GROUP_HEADS * LANES + GROUP_WIDTH), lambda i: (0, 0)),
                  pl.BlockSpec((LANES, LANES), lambda i: (0, 0)),
                  full((1, C_COLS_PAD)),
                  full((2, 1, gw)), full((2, LANES, gw)), full((2, 1, gw)), full((2, LANES, gw)),
                  full((2, 1, gw)), full((2, 1, gw)), full((1, gw)), full((gw, gw))],
        out_specs=[o[0] for o in outs],
        out_shape=[o[1] for o in outs],
        scratch_shapes=[pltpu.VMEM((TM, ABD_COLS), F32), pltpu.VMEM((TM + 16, C_COLS_PAD), F32)],
        compiler_params=_cparams(("parallel",)),
        name="inproj_prep",
    )(xc, xl, xl, xl, mod3, p['g_pre_mix'], p['w_in_p'], cs, sn, csm, snm,
      p['qn'], p['kn'], p['mqn'], p['mkvn'], p['wuq'], p['wukv'], p['ones64'],
      p['mu'], p['w0'], p['w2p'], p['a0'], p['a2p'], p['k_k'], p['k_a'], p['r_k'], p['g2p'])


def _flash_kernel(q_ref, k_ref, vt_ref, o_ref, *, shared_kv, tq, tk, ctx_len, s_tot):
    qi = pl.program_id(2)
    dv = vt_ref.shape[-2] - ONES_ROWS
    if shared_kv:
        streams = [(jnp.concatenate([q_ref[0], q_ref[1]], axis=1), 0)]
    else:
        streams = [(q_ref[0], 0), (q_ref[1], 1)]

    def scores(c):
        out = []
        for q, kv in streams:
            k = k_ref[kv, pl.ds(pl.multiple_of(c * tk, tk), tk), :]
            out.append(_dot(k, q))
        return out

    def softmax_pv(c, sts, carries):
        new = []
        for st, (_, kv), (m, acc) in zip(sts, streams, carries):
            m_new = jnp.maximum(m, jnp.max(st, axis=0, keepdims=True))
            pt = jnp.exp2(st - m_new).astype(BF16)
            acc = jnp.exp2(m - m_new) * acc + _dot(vt_ref[kv, c], pt)
            new.append((m_new, acc))
        return tuple(new)

    def run(n_chunks):
        unroll = next(u for u in (33, 11, 3, 2, 1) if n_chunks % u == 0)
        groups = n_chunks // unroll
        init = tuple((jnp.full((1, q.shape[1]), NEG_INF, F32), jnp.zeros((dv + ONES_ROWS, q.shape[1]), F32))
                     for q, _ in streams)

        def body(g, carries):
            queue = [scores(g * unroll + u) for u in range(min(LOOKAHEAD, unroll))]
            for u in range(unroll):
                if u + LOOKAHEAD < unroll:
                    queue.append(scores(g * unroll + u + LOOKAHEAD))
                carries = softmax_pv(g * unroll + u, queue.pop(0), carries)
            return carries

        carries = body(0, init) if groups == 1 else lax.fori_loop(0, groups, body, init)
        outs = [acc[:dv] / acc[dv:dv + 1] for _, acc in carries]
        if shared_kv:
            outs = [outs[0][:, :tq], outs[0][:, tq:]]
        o_ref[...] = jnp.concatenate(outs, axis=0).T.astype(o_ref.dtype)

    is_ctx = qi * tq < ctx_len

    @pl.when(is_ctx)
    def _():
        run(ctx_len // tk)

    @pl.when(jnp.logical_not(is_ctx))
    def _():
        run(s_tot // tk)


def _flash(qt, k, vt, *, shared_kv, batch, s_tot, ctx_len):
    nh, _, dk, _ = qt.shape
    n = k.shape[1]
    dv = vt.shape[-2] - ONES_ROWS
    nkv = 1 if shared_kv else 2
    tq = tk = TM
    nq = s_tot // tq
    kern = functools.partial(_flash_kernel, shared_kv=shared_kv, tq=tq, tk=tk,
                             ctx_len=ctx_len, s_tot=s_tot)
    return pl.pallas_call(
        kern,
        grid=(batch, nh // 2, nq),
        in_specs=[pl.BlockSpec((2, None, dk, tq), lambda b, p, i: (p, b * nq + i, 0, 0)),
                  pl.BlockSpec((nkv, s_tot, dk), lambda b, p, i: (p, b, 0)),
                  pl.BlockSpec((nkv, s_tot // tk, dv + ONES_ROWS, tk), lambda b, p, i: (p, b, 0, 0))],
        out_specs=pl.BlockSpec((tq, 2 * dv), lambda b, p, i: (b * nq + i, p)),
        out_shape=jax.ShapeDtypeStruct((n, nh * dv), BF16),
        compiler_params=_cparams(("parallel", "parallel", "arbitrary")),
        name="flash_shared" if shared_kv else "flash_split",
    )(qt, k, vt)


def _window_kernel(sink_ref, bias_ref, q_ref, k_ref, vt_ref, o_ref, *, ctx_len, s_tot, n_blocks):
    w = WINDOW
    g = pl.program_id(1)
    first_lat = ctx_len // w
    last_blk = s_tot // w - 1
    lane = lax.broadcasted_iota(jnp.int32, (1, 2 * w), 1)
    sk = jnp.where(lane < w, sink_ref[2 * g], sink_ref[2 * g + 1])
    k_ctx = k_ref[0:ctx_len, :]
    vt_ctx = jnp.concatenate([vt_ref[c] for c in range(first_lat)], axis=1)

    def rows(blk):
        return k_ref[pl.ds(pl.multiple_of(blk * w, w), w), :]

    def neighbours(i):
        jb = pl.program_id(2) * n_blocks + i
        return jnp.clip(jb - 1, 0, last_blk), jb, jnp.clip(jb + 1, 0, last_blk)

    def scores(i):
        pb, jb, nb = neighbours(i)
        kind = jnp.where(jb < first_lat, 0, jnp.where(jb == first_lat, 1, jnp.where(jb == last_blk, 3, 2)))
        qt = jnp.concatenate([q_ref[0, i], q_ref[1, i]], axis=1)
        kw = jnp.concatenate([rows(pb), rows(jb), rows(nb)], axis=0)
        return _dot(kw, qt) + bias_ref[kind], _dot(k_ctx, qt)

    queue = [scores(i) for i in range(min(LOOKAHEAD, n_blocks))]
    for i in range(n_blocks):
        if i + LOOKAHEAD < n_blocks:
            queue.append(scores(i + LOOKAHEAD))
        s_w, s_c = queue.pop(0)
        pb, jb, nb = neighbours(i)
        m = jnp.maximum(jnp.maximum(jnp.max(s_w, axis=0, keepdims=True),
                                    jnp.max(s_c, axis=0, keepdims=True)), sk)
        p_w = jnp.exp(s_w - m).astype(BF16)
        p_c = jnp.exp(s_c - m).astype(BF16)
        vt_w = jnp.concatenate([vt_ref[pb], vt_ref[jb], vt_ref[nb]], axis=1)
        acc = _dot(vt_w, p_w) + _dot(vt_ctx, p_c)
        o = acc[:HEAD_DIM] / (acc[HEAD_DIM:HEAD_DIM + 1] + jnp.exp(sk - m))
        o_ref[w * i:w * (i + 1), :] = jnp.concatenate([o[:, :w], o[:, w:]], axis=0).T.astype(o_ref.dtype)


def _window_bias():
    w = WINDOW
    c = jnp.arange(3 * w)[:, None]
    r = jnp.arange(2 * w)[None, :] % w
    band = (c >= r) & (c <= r + 2 * w)
    kinds = [jnp.zeros_like(band), band & (c >= w), band, band & (c < 2 * w)]
    return jnp.where(jnp.stack(kinds), 0.0, NEG_INF).astype(F32)


def _window_attn(sink, qt, k, vt, *, batch, s_tot, ctx_len):
    nh, _, dk, _ = qt.shape
    n = k.shape[1]
    nblk = s_tot // WINDOW
    wb = next(u for u in (WINDOW_BLOCKS, 6, 3, 2, 1) if nblk % u == 0)
    nstep = nblk // wb
    kern = functools.partial(_window_kernel, ctx_len=ctx_len, s_tot=s_tot, n_blocks=wb)
    return pl.pallas_call(
        kern,
        grid=(batch, nh // 2, nstep),
        in_specs=[pl.BlockSpec(memory_space=pltpu.SMEM),
                  pl.BlockSpec((4, 3 * WINDOW, 2 * WINDOW), lambda b, g, j: (0, 0, 0)),
                  pl.BlockSpec((2, wb, dk, WINDOW), lambda b, g, j: (g, b * nstep + j, 0, 0)),
                  pl.BlockSpec((None, s_tot, dk), lambda b, g, j: (g, b, 0)),
                  pl.BlockSpec((None, nblk, dk + ONES_ROWS, WINDOW), lambda b, g, j: (g, b, 0, 0))],
        out_specs=pl.BlockSpec((WINDOW * wb, 2 * dk), lambda b, g, j: (b * nstep + j, g)),
        out_shape=jax.ShapeDtypeStruct((n, nh * dk), BF16),
        compiler_params=_cparams(("parallel", "parallel", "arbitrary")),
        name="window_attn",
    )(sink, _window_bias(), qt, k, vt)


def _rwkv_scan_kernel(lw_ref, k_ref, a_ref, b_ref, r_ref, v_ref, y_ref, s_ref, *, reverse):
    @pl.when(pl.program_id(0) == 0)
    def _():
        s_ref[...] = jnp.zeros_like(s_ref)

    batch = lw_ref.shape[0]
    n_chunks = TM // CHUNK
    hd = HEAD_DIM
    row = lax.broadcasted_iota(jnp.int32, (TM, TM), 0)
    col = lax.broadcasted_iota(jnp.int32, (TM, TM), 1)
    same = (row // CHUNK) == (col // CHUNK)
    before = (col > row) if reverse else (col < row)
    m_strict = jnp.logical_and(same, before)
    m_incl = jnp.logical_and(same, jnp.logical_or(before, row == col))
    m_incl16 = m_incl.astype(BF16)
    eye = (row == col).astype(F32)
    assert CHUNK // SUB == 4
    same_sub = (row // SUB) == (col // SUB)

    items = [(b, h) for b in range(batch) for h in range(GROUP_HEADS)]
    idx = range(len(items))
    e_pos, at_all, rt_all, bt_all, kt_all, v_all = [], [], [], [], [], []
    for b in range(batch):
        lw = lw_ref[b]
        cum = _mm_exact_lhs(m_incl16, lw)
        e_pos.append(jnp.exp(cum))
        e_neg = jnp.exp(-cum)
        at_all.append(a_ref[b] * jnp.exp(cum - lw))
        rt_all.append(r_ref[b] * e_pos[b])
        bt_all.append(b_ref[b] * e_neg)
        kt_all.append(k_ref[b] * e_neg)
        v_all.append(v_ref[b])

    def head(arrs, b, h):
        return arrs[b][:, hd * h:hd * (h + 1)]

    at = [head(at_all, b, h) for b, h in items]
    rt = [head(rt_all, b, h) for b, h in items]
    bt = [head(bt_all, b, h) for b, h in items]
    at16 = [x.astype(BF16) for x in at]
    rt16 = [x.astype(BF16) for x in rt]
    bt16 = [x.astype(BF16) for x in bt]
    kt16 = [head(kt_all, b, h).astype(BF16) for b, h in items]
    v16 = [head(v_all, b, h).astype(BF16) for b, h in items]
    ab = [jnp.where(m_strict, _dot_nt(at16[i], bt16[i]), 0.0) for i in idx]
    ak16 = [jnp.where(m_strict, _dot_nt(at16[i], kt16[i]), 0.0).astype(BF16) for i in idx]
    rk16 = [jnp.where(m_incl, _dot_nt(rt16[i], kt16[i]), 0.0).astype(BF16) for i in idx]
    n_c = TM // CHUNK

    def wide(m):
        out = m[:CHUNK]
        for c in range(1, n_c):
            out = out + m[CHUNK * c:CHUNK * (c + 1)]
        return out

    def bdiag16(wm):
        return jnp.where(same, jnp.concatenate([wm] * n_c, axis=0), 0.0).astype(BF16)

    def wdot(wl, bd16):
        return _dot(wl.astype(BF16), bd16)

    eye_w = wide(eye)
    ld = [jnp.where(same_sub, x, 0.0) for x in ab]
    lo16 = [jnp.where(same_sub, 0.0, x).astype(BF16) for x in ab]
    pw_w = [wide(x) for x in ld]
    pw_bd = [x.astype(BF16) for x in ld]
    td_w = [eye_w + x for x in pw_w]
    for _ in range(int(math.log2(SUB)) - 1):
        pw_w = [wdot(x, y) for x, y in zip(pw_w, pw_bd)]
        pw_bd = [bdiag16(x) for x in pw_w]
        td_w = [t + wdot(t, y) for t, y in zip(td_w, pw_bd)]
    m1_w = [wdot(t, x) for t, x in zip(td_w, lo16)]
    m1_bd = [bdiag16(x) for x in m1_w]
    m2_w = [wdot(x, y) for x, y in zip(m1_w, m1_bd)]
    m3_w = [wdot(x, bdiag16(y)) for x, y in zip(m1_w, m2_w)]
    nn_w = [eye_w + a1 + a2 + a3 for a1, a2, a3 in zip(m1_w, m2_w, m3_w)]
    akv = [_dot(ak16[i], v16[i]) for i in idx]

    rowc = lax.broadcasted_iota(jnp.int32, (TM, n_c * 2 * hd), 0) // CHUNK
    colc = lax.broadcasted_iota(jnp.int32, (TM, n_c * 2 * hd), 1) // (2 * hd)
    same_x = rowc == colc

    def spread16(xr):
        return jnp.where(same_x, jnp.concatenate([xr] * n_c, axis=1), 0.0).astype(BF16)

    def unwide(xw):
        return jnp.concatenate([xw[:, 2 * hd * c:2 * hd * (c + 1)] for c in range(n_c)], axis=0)

    tx = [wdot(td_w[i], spread16(jnp.concatenate([at[i], akv[i]], axis=-1))) for i in idx]
    wu = [unwide(wdot(nn_w[i], spread16(unwide(tx[i])))) for i in idx]
    wa = [x[:, :hd] for x in wu]
    u016 = [x[:, hd:].astype(BF16) for x in wu]
    rb_w = [wide(jnp.where(m_incl, _dot_nt(rt16[i], bt16[i]), 0.0)) for i in idx]
    rbwu = [unwide(wdot(rb_w[i], spread16(wu[i]))) for i in idx]
    yr16 = [(rt[i] + rbwu[i][:, :hd]).astype(BF16) for i in idx]
    y0 = [rbwu[i][:, hd:] + _dot(rk16[i], v16[i]) for i in idx]

    order = range(n_chunks - 1, -1, -1) if reverse else range(n_chunks)
    eye_h = eye[:hd, :hd]
    trans = {}
    for c in order:
        rs = slice(CHUNK * c, CHUNK * (c + 1))
        last = CHUNK * c if reverse else CHUNK * (c + 1) - 1
        for i, (b, h) in enumerate(items):
            decay = e_pos[b][last:last + 1, hd * h:hd * (h + 1)]
            g = (eye_h + _mm3tn(wa[i][rs], bt[i][rs])) * decay
            hh = (_dot_tn(u016[i][rs], bt16[i][rs]) + _dot_tn(v16[i][rs], kt16[i][rs])) * decay
            trans[c, i] = (g, hh)
    for c in order:
        rs = slice(CHUNK * c, CHUNK * (c + 1))
        s0 = [s_ref[b, h] for b, h in items]
        for i, (b, h) in enumerate(items):
            s_ref[b, h] = _mm3(s0[i], trans[c, i][0]) + trans[c, i][1]
        for b in range(batch):
            y_ref[b, rs, :] = jnp.concatenate(
                [_dot_nt(yr16[i][rs], s0[i].astype(BF16)) + y0[i][rs] for i, (bb, _) in enumerate(items) if bb == b],
                axis=-1)


def _mm3tn(a, b):
    ah, al = _split2(a)
    bh, bl = _split2(b)
    return _dot_tn(ah, bh) + (_dot_tn(ah, bl) + _dot_tn(al, bh))


def _rwkv_scan(lw, kd, a, b, r, v, *, d, batch, n_tiles):
    n, gw = r.shape
    s_tot = n // batch
    reverse = d == 1

    def tile(s):
        if reverse:
            return jnp.where(s == 0, 0, n_tiles - s)
        return s

    dspec = pl.BlockSpec((None, batch, TM, gw), lambda s: (d, 0, tile(s), 0))
    spec = pl.BlockSpec((batch, TM, gw), lambda s: (0, tile(s), 0))
    per_dir = [x.reshape(2, batch, s_tot, gw) for x in (lw, kd, a, b)]
    shared = [x.reshape(batch, s_tot, gw) for x in (r, v)]
    y = pl.pallas_call(
        functools.partial(_rwkv_scan_kernel, reverse=reverse),
        grid=(n_tiles,),
        in_specs=[dspec, dspec, dspec, dspec, spec, spec],
        out_specs=spec,
        out_shape=jax.ShapeDtypeStruct((batch, s_tot, gw), F32),
        scratch_shapes=[pltpu.VMEM((batch, GROUP_HEADS, HEAD_DIM, HEAD_DIM), F32)],
        compiler_params=_cparams(("arbitrary",)),
        name="rwkv_scan_rev" if reverse else "rwkv_scan_fwd",
    )(*per_dir, *shared)
    return y.reshape(n, gw)


def _out_mlp_kernel(xc_ref, xl_ref, oa_ref, ob_ref, od_ref, yf_ref, yr_ref, bonus_ref, g_ref,
                    lnw_ref, lnb_ref, ones_ref, mod_ref, gpost_ref, gpre_ref, gpm_ref,
                    wo_ref, w1_ref, w2_ref, xo_ref, *, ctx_first):
    d = D_MODEL
    gw = GROUP_WIDTH
    mod = mod_ref[...]
    ones = ones_ref[...]

    def seg_mean(t):
        sm = jnp.concatenate([_mm_exact_rhs(t[:, :LANES], ones), _mm_exact_rhs(t[:, LANES:], ones)], axis=-1)
        return sm * (1.0 / HEAD_DIM)

    yy = yf_ref[...] + yr_ref[...]
    dlt = yy - seg_mean(yy)
    yn = dlt * lax.rsqrt(seg_mean(dlt * dlt) + RWKV_GN_EPS) * lnw_ref[...] + lnb_ref[...]
    oc = ((yn + bonus_ref[...]) * g_ref[...]).astype(BF16)

    y = _dot(oa_ref[...], wo_ref[0:gw, :])
    y += _dot(ob_ref[...], wo_ref[gw:2 * gw, :])
    y += _dot(oc, wo_ref[2 * gw:3 * gw, :])
    y += _dot(od_ref[...], wo_ref[3 * gw:4 * gw, :])
    x = jnp.where(jnp.logical_and(ctx_first, pl.program_id(1) == 0), xc_ref[...], xl_ref[...])
    x1 = x + mod[:, 2 * d:3 * d] * (_rms(y, NORM_EPS) * gpost_ref[...])
    h = _rms(x1, NORM_EPS) * gpre_ref[...] * (1.0 + mod[:, 4 * d:5 * d]) + mod[:, 3 * d:4 * d]
    u = jnp.maximum(_dot(h.astype(BF16), w1_ref[...]), 0.0)
    zz = _dot((u * u).astype(BF16), w2_ref[...])
    xo_ref[...] = x1 + mod[:, 5 * d:6 * d] * (_rms(zz, NORM_EPS) * gpm_ref[...])


def _out_mlp(xc, xl, xc_map, xl_map, oa, ob, od, yf, yr, bonus, rg, mod3, p, *, batch, n_tiles, skip_ctx):
    d = D_MODEL
    gw = GROUP_WIDTH
    off = 1 if skip_ctx else 0
    nt = n_tiles - off

    def tile(b, j):
        return (b * n_tiles + off + j, 0)

    def mrow(b, j):
        return (jnp.where(off + j == 0, batch, b), 0, 0)

    ospec = pl.BlockSpec((TM, gw), tile)
    row = pl.BlockSpec((1, d), lambda b, j: (0, 0))
    grow = pl.BlockSpec((1, gw), lambda b, j: (0, 0))

    def wspec(shape):
        return pl.BlockSpec(shape, lambda b, j: (0, 0), pipeline_mode=pl.Buffered(1))

    return pl.pallas_call(
        functools.partial(_out_mlp_kernel, ctx_first=not skip_ctx),
        grid=(batch, nt),
        in_specs=[pl.BlockSpec((TM, d), lambda b, j: xc_map(b, off + j)),
                  pl.BlockSpec((TM, d), lambda b, j: xl_map(b, off + j)),
                  ospec, ospec, ospec, ospec, ospec, ospec, ospec,
                  grow, grow, pl.BlockSpec((LANES, LANES), lambda b, j: (0, 0)),
                  pl.BlockSpec((None, 1, 6 * d), mrow),
                  row, row, row,
                  wspec((d, d)), wspec((d, D_FF)), wspec((D_FF, d))],
        out_specs=pl.BlockSpec((TM, d), lambda b, j: (b * nt + j, 0)),
        out_shape=jax.ShapeDtypeStruct((batch * nt * TM, d), F32),
        compiler_params=_cparams(("parallel", "parallel")),
        name="out_mlp",
    )(xc, xl, oa, ob, od, yf, yr, bonus, rg, p['ln_w'], p['ln_b'], p['ones64'], mod3,
      p['g_post_mix'], p['g_pre_mlp'], p['g_post_mlp'], p['w_out'], p['w_mlp1'], p['w_mlp2'])


def _rope_tables(seq, ctx_len):
    t = jnp.arange(seq, dtype=jnp.int32)
    row = (t // GRID_W).astype(F32)
    col = (t % GRID_W).astype(F32)

    def tables(rot_dim, lane_dim):
        n = rot_dim // 4
        inv = ROPE_THETA ** (-jnp.arange(n, dtype=F32) / n)
        ar, ac = row[:, None] * inv, col[:, None] * inv
        cos = jnp.concatenate([jnp.cos(ar), jnp.cos(ar), jnp.cos(ac), jnp.cos(ac)], axis=-1)
        sin = jnp.concatenate([-jnp.sin(ar), jnp.sin(ar), -jnp.sin(ac), jnp.sin(ac)], axis=-1)
        return cos, sin

    def with_ctx(tab, fill):
        return jnp.concatenate([jnp.full((ctx_len, tab.shape[1]), fill, F32), tab], axis=0)

    cos, sin = tables(HEAD_DIM, LANES)
    cs = with_ctx(jnp.tile(cos, (1, LANES // HEAD_DIM)), 1.0)
    sn = with_ctx(jnp.tile(sin, (1, LANES // HEAD_DIM)), 0.0)
    cosm, sinm = tables(MLA_ROPE_DIM, LANES)
    pad_l, pad_r = MLA_NOPE_DIM, LANES - MLA_QK_DIM
    csm = with_ctx(jnp.pad(cosm, ((0, 0), (pad_l, pad_r)), constant_values=1.0), 1.0)
    snm = with_ctx(jnp.pad(sinm, ((0, 0), (pad_l, pad_r))), 0.0)
    return cs, sn, csm, snm


def _layer_params(l, w):
    d = D_MODEL
    gw = GROUP_WIDTH
    w_in = w['w_in'][l]
    a_end, b_end, c_end = A_COLS, A_COLS + B_COLS, A_COLS + B_COLS + C_COLS
    lowrank = MLA_Q_RANK + MLA_KV_RANK
    zeros = lambda c: jnp.zeros((d, c), F32)
    w_in_p = jnp.concatenate([
        w_in[:, :a_end],
        w_in[:, a_end:a_end + lowrank], zeros(MLA_NOPE_DIM), w_in[:, a_end + lowrank:b_end], zeros(LANES - MLA_QK_DIM),
        w_in[:, c_end:],
        w_in[:, b_end:c_end], zeros(C_COLS_PAD - C_COLS)], axis=1).astype(BF16)
    wuq = w['mla_w_uq'][l].reshape(MLA_Q_RANK, GROUP_HEADS, MLA_QK_DIM)
    wuq = jnp.pad(wuq, ((0, 0), (0, 0), (0, LANES - MLA_QK_DIM))).reshape(MLA_Q_RANK, GROUP_HEADS * LANES)
    wukv = w['mla_w_ukv'][l].reshape(MLA_KV_RANK, GROUP_HEADS, MLA_NOPE_DIM + HEAD_DIM)
    wk = jnp.pad(wukv[:, :, :MLA_NOPE_DIM], ((0, 0), (0, 0), (0, LANES - MLA_NOPE_DIM)))
    wukv_p = jnp.concatenate([wk.reshape(MLA_KV_RANK, GROUP_HEADS * LANES),
                              wukv[:, :, MLA_NOPE_DIM:].reshape(MLA_KV_RANK, gw)], axis=1)
    lane = jnp.arange(LANES)
    ones64 = ((lane[:, None] // HEAD_DIM) == (lane[None, :] // HEAD_DIM)).astype(BF16)
    zrows = lambda r: jnp.zeros((2, r, gw), F32)
    return {
        'w_in_p': w_in_p,
        'g_pre_mix': w['g_pre_mix'][l].reshape(1, d),
        'g_post_mix': w['g_post_mix'][l].reshape(1, d),
        'g_pre_mlp': w['g_pre_mlp'][l].reshape(1, d),
        'g_post_mlp': w['g_post_mlp'][l].reshape(1, d),
        'qn': jnp.tile(w['gqa_q_norm'][l], 2).reshape(1, LANES),
        'kn': jnp.tile(w['gqa_k_norm'][l], 2).reshape(1, LANES),
        'mqn': w['mla_q_norm'][l].reshape(1, MLA_Q_RANK),
        'mkvn': w['mla_kv_norm'][l].reshape(1, MLA_KV_RANK),
        'wuq': wuq.astype(BF16),
        'wukv': wukv_p.astype(BF16),
        'ones64': ones64,
        'mu': jnp.pad(w['rwkv_mu'][l], (0, C_COLS_PAD - C_COLS)).reshape(1, C_COLS_PAD),
        'w0': w['rwkv_w0'][l].reshape(2, 1, gw),
        'w2p': jnp.concatenate([w['rwkv_w2'][l], zrows(LANES - RWKV_DECAY_LORA)], axis=1),
        'a0': w['rwkv_a0'][l].reshape(2, 1, gw),
        'a2p': jnp.concatenate([zrows(RWKV_DECAY_LORA), w['rwkv_a2'][l],
                                zrows(LANES - RWKV_DECAY_LORA - RWKV_ICLR_LORA)], axis=1),
        'k_k': w['rwkv_k_k'][l].reshape(2, 1, gw),
        'k_a': w['rwkv_k_a'][l].reshape(2, 1, gw),
        'r_k': w['rwkv_r_k'][l].reshape(1, gw),
        'g2p': jnp.pad(w['rwkv_g2'][l], ((0, gw - RWKV_GATE_LORA), (0, 0))),
        'ln_w': w['rwkv_ln_w'][l].reshape(1, gw),
        'ln_b': w['rwkv_ln_b'][l].reshape(1, gw),
        'sink': w['swa_sink'][l],
        'w_out': w['w_out'][l].astype(BF16),
        'w_mlp1': w['w_mlp1'][l].astype(BF16),
        'w_mlp2': w['w_mlp2'][l].astype(BF16),
    }


def kernel(x, c, ctx, c_ctx, w_mod, b_mod, g_pre_mix, g_post_mix, g_pre_mlp, g_post_mlp, w_in, gqa_q_norm, gqa_k_norm, mla_q_norm, mla_kv_norm, mla_w_uq, mla_w_ukv, rwkv_mu, rwkv_w0, rwkv_w2, rwkv_a0, rwkv_a2, rwkv_k_k, rwkv_k_a, rwkv_r_k, rwkv_g2, rwkv_ln_w, rwkv_ln_b, swa_sink, w_out, w_mlp1, w_mlp2):
    batch, seq, d = x.shape
    ctx_len = ctx.shape[1]
    depth = w_mod.shape[0]
    assert d == D_MODEL and ctx_len == TM and seq % TM == 0 and seq % GRID_W == 0
    assert batch + 1 <= 8
    s_tot = ctx_len + seq
    n_tiles = s_tot // TM
    w = dict(w_in=w_in, g_pre_mix=g_pre_mix, g_post_mix=g_post_mix, g_pre_mlp=g_pre_mlp,
             g_post_mlp=g_post_mlp, gqa_q_norm=gqa_q_norm, gqa_k_norm=gqa_k_norm, mla_q_norm=mla_q_norm,
             mla_kv_norm=mla_kv_norm, mla_w_uq=mla_w_uq, mla_w_ukv=mla_w_ukv, rwkv_mu=rwkv_mu,
             rwkv_w0=rwkv_w0, rwkv_w2=rwkv_w2, rwkv_a0=rwkv_a0, rwkv_a2=rwkv_a2, rwkv_k_k=rwkv_k_k,
             rwkv_k_a=rwkv_k_a, rwkv_r_k=rwkv_r_k, rwkv_g2=rwkv_g2, rwkv_ln_w=rwkv_ln_w,
             rwkv_ln_b=rwkv_ln_b, swa_sink=swa_sink, w_out=w_out, w_mlp1=w_mlp1, w_mlp2=w_mlp2)

    ct = jnp.concatenate([c, c_ctx[None, :], jnp.zeros((8 - batch - 1, d), F32)], axis=0).T
    mods = _modulation(ct, w_mod, b_mod, batch + 1)

    def mod_row(i):
        return jnp.where(i % n_tiles == 0, batch, i // n_tiles)

    tabs = _rope_tables(seq, ctx_len)
    geo = dict(batch=batch, s_tot=s_tot, ctx_len=ctx_len)
    n = batch * s_tot
    n_lat = n_tiles - 1
    xc, xl = ctx.reshape(batch * ctx_len, d), x.reshape(batch * seq, d)
    xc_map = lambda b, j: (b, 0)
    xl_map = lambda b, j: (b * n_lat + jnp.maximum(j - 1, 0), 0)
    for l in range(depth):
        p = _layer_params(l, w)
        mod3 = mods[l].reshape(8, 1, 6 * d)
        flat = lambda m: (lambda i: m(i // n_tiles, i % n_tiles))
        qa, ka, va, qb, kb, vb, qd, kd, vd, lw, rkd, ra, rb, rr, rv, rg, bonus = _inproj_prep(
            xc, xl, flat(xc_map), flat(xl_map), mod3, tabs, p, mod_row, n, n_tiles)
        oa = _flash(qa, ka, va, shared_kv=True, **geo)
        ob = _flash(qb, kb, vb, shared_kv=False, **geo)
        od = _window_attn(p['sink'], qd, kd, vd, **geo)
        yf = _rwkv_scan(lw, rkd, ra, rb, rr, rv, d=0, batch=batch, n_tiles=n_tiles)
        yr = _rwkv_scan(lw, rkd, ra, rb, rr, rv, d=1, batch=batch, n_tiles=n_tiles)
        xs = _out_mlp(xc, xl, xc_map, xl_map, oa, ob, od, yf, yr, bonus, rg, mod3, p,
                      batch=batch, n_tiles=n_tiles, skip_ctx=l == depth - 1)
        xc = xl = xs
        xc_map = lambda b, j: (b * n_tiles, 0)
        xl_map = lambda b, j: (b * n_tiles + j, 0)
    return xs.reshape(batch, seq, d)
```

```python
import functools
import math

import jax
import jax.numpy as jnp
from jax import lax
from jax.experimental import pallas as pl
from jax.experimental.pallas import tpu as pltpu

F32 = jnp.float32
BF16 = jnp.bfloat16

D_MODEL = 1024
GRID_W = 64
HEAD_DIM = 64
GROUP_WIDTH = 256
GROUP_HEADS = 4
ROPE_THETA = 10000.0
NORM_EPS = 1e-6
NEG_INF = -1e30
MLA_Q_RANK = 256
MLA_KV_RANK = 128
MLA_NOPE_DIM = 64
MLA_ROPE_DIM = 32
MLA_QK_DIM = MLA_NOPE_DIM + MLA_ROPE_DIM
RWKV_DECAY_LORA = 64
RWKV_ICLR_LORA = 64
RWKV_GATE_LORA = 160
RWKV_GN_EPS = 64e-5
WINDOW = 128
D_FF = 4 * D_MODEL

LOG2E = math.log2(math.e)
LANES = 128
TM = 256
ONES_ROWS = 8
LOOKAHEAD = 2
WINDOW_BLOCKS = 33
CHUNK = 64
SUB = 16
GQA_KV_WIDTH = 2 * HEAD_DIM
A_COLS = GROUP_WIDTH + 2 * GQA_KV_WIDTH
B_COLS = MLA_Q_RANK + MLA_KV_RANK + MLA_ROPE_DIM
C_COLS = 3 * GROUP_WIDTH + RWKV_DECAY_LORA + RWKV_ICLR_LORA + RWKV_GATE_LORA
SEC_W = 512
SEC_A = 0
SEC_B = SEC_A + SEC_W
SEC_D = SEC_B + SEC_W
ABD_COLS = SEC_D + SEC_W
C_COLS_PAD = 1152
MOD_BLOCK = 1536
ROPE_SHIFT = HEAD_DIM // 4
ROPE_SHIFT_MLA = MLA_ROPE_DIM // 4
VMEM_LIMIT = 56 * 1024 * 1024


def _cparams(sem):
    return pltpu.CompilerParams(dimension_semantics=sem, vmem_limit_bytes=VMEM_LIMIT)


def _dot(a, b):
    return jnp.dot(a, b, preferred_element_type=F32)


def _dot_nt(a, b):
    return lax.dot_general(a, b, (((1,), (1,)), ((), ())), preferred_element_type=F32)


def _dot_tn(a, b):
    return lax.dot_general(a, b, (((0,), (0,)), ((), ())), preferred_element_type=F32)


def _split2(x):
    hi = x.astype(BF16)
    lo = (x - hi.astype(F32)).astype(BF16)
    return hi, lo


def _mm_exact_rhs(a, b_bf16):
    hi, lo = _split2(a)
    return _dot(hi, b_bf16) + _dot(lo, b_bf16)


def _mm_exact_lhs(a_bf16, b):
    hi, lo = _split2(b)
    return _dot(a_bf16, hi) + _dot(a_bf16, lo)


def _mm3(a, b):
    ah, al = _split2(a)
    bh, bl = _split2(b)
    return _dot(ah, bh) + (_dot(ah, bl) + _dot(al, bh))


def _sigmoid(x):
    return 1.0 / (1.0 + jnp.exp(-x))


def _rms(x, eps):
    return x * lax.rsqrt(jnp.mean(x * x, axis=-1, keepdims=True) + eps)


def _mod_kernel(ct_ref, w_ref, b_ref, o_ref, *, n_rows):
    ct = ct_ref[...]
    st = ct * _sigmoid(ct)
    w = w_ref[...]
    rows = [jnp.sum(st[:, r:r + 1] * w, axis=0, keepdims=True) for r in range(n_rows)]
    rows.append(jnp.zeros((8 - n_rows, w.shape[1]), F32))
    o_ref[...] = jnp.concatenate(rows, axis=0) + b_ref[...]


def _modulation(ct, w_mod, b_mod, n_rows):
    depth, d, n6 = w_mod.shape
    tn = MOD_BLOCK
    return pl.pallas_call(
        functools.partial(_mod_kernel, n_rows=n_rows),
        grid=(depth, n6 // tn),
        in_specs=[pl.BlockSpec((d, 8), lambda l, j: (0, 0)),
                  pl.BlockSpec((None, d, tn), lambda l, j: (l, 0, j)),
                  pl.BlockSpec((None, 1, tn), lambda l, j: (l, 0, j))],
        out_specs=pl.BlockSpec((None, 8, tn), lambda l, j: (l, 0, j)),
        out_shape=jax.ShapeDtypeStruct((depth, 8, n6), F32),
        compiler_params=_cparams(("parallel", "parallel")),
        name="modulation",
    )(ct, w_mod, b_mod.reshape(depth, 1, n6))


def _rope_slab(x, cos, sin_signed, shift):
    left = pltpu.roll(x, LANES - shift, axis=1)
    right = pltpu.roll(x, shift, axis=1)
    lane = lax.broadcasted_iota(jnp.int32, x.shape, 1)
    first = ((lane // shift) % 2) == 0
    return x * cos + jnp.where(first, left, right) * sin_signed


def _head_norm_slab(x, gain, ones):
    ss = _mm_exact_rhs(x * x, ones)
    return x * lax.rsqrt(ss * (1.0 / HEAD_DIM) + NORM_EPS) * gain


def _inproj_prep_kernel(xc_ref, xl_ref, xp_ref, xn_ref, mod_ref, g_ref, w_ref,
                        cs_ref, sn_ref, csm_ref, snm_ref, qn_ref, kn_ref, mqn_ref, mkvn_ref,
                        wuq_ref, wukv_ref, ones_ref,
                        mu_ref, w0_ref, w2_ref, a0_ref, a2_ref, kk_ref, ka_ref, rk_ref, g2_ref,
                        qa_ref, ka_out, va_ref, qb_ref, kb_ref, vb_ref, qd_ref, kd_out, vd_ref,
                        lw_ref, rkd_ref, ra_ref, rb_ref, rr_ref, rv_ref, rg_ref, bonus_ref,
                        z_ref, zc_s, *, n_tiles):
    d = D_MODEL
    mod = mod_ref[...]
    j = pl.program_id(0) % n_tiles
    is_ctx = j == 0

    def modnorm(t):
        return (_rms(t, NORM_EPS) * g_ref[...] * (1.0 + mod[:, d:2 * d]) + mod[:, 0:d]).astype(BF16)

    h16 = modnorm(jnp.where(is_ctx, xc_ref[...], xl_ref[...]))
    h_all = jnp.concatenate([h16, modnorm(jnp.concatenate([xp_ref[...], xn_ref[...]], axis=0))], axis=0)
    zc_s[...] = _dot(h_all, w_ref[:, ABD_COLS:ABD_COLS + C_COLS_PAD])
    cs, sn = cs_ref[...], sn_ref[...]
    csm, snm = csm_ref[...], snm_ref[...]
    ones = ones_ref[...]
    hd = HEAD_DIM
    q_scale = HEAD_DIM ** -0.5

    def project(lo, hi):
        return _dot(h16, w_ref[:, lo:hi])

    def put_heads(ref, base, slab):
        ref[base] = slab[:, :hd].astype(BF16)
        ref[base + 1] = slab[:, hd:].astype(BF16)

    def put_heads_t(ref, base, slab, ones_rows=ONES_ROWS):
        st = slab.T.astype(BF16)
        if ones_rows:
            one = jnp.ones((ones_rows, st.shape[1]), BF16)
            ref[base] = jnp.concatenate([st[:hd], one], axis=0)
            ref[base + 1] = jnp.concatenate([st[hd:], one], axis=0)
        else:
            ref[base] = st[:hd]
            ref[base + 1] = st[hd:]

    z_ref[:, SEC_B:SEC_B + SEC_W] = project(SEC_B, SEC_B + SEC_W)
    z_ref[:, SEC_A:SEC_A + SEC_W] = project(SEC_A, SEC_A + SEC_W)

    cq = _rms(z_ref[:, SEC_B:SEC_B + MLA_Q_RANK], NORM_EPS) * mqn_ref[...]
    q = _dot(cq.astype(BF16), wuq_ref[...])
    ckv = _rms(z_ref[:, SEC_B + MLA_Q_RANK:SEC_B + MLA_Q_RANK + MLA_KV_RANK], NORM_EPS) * mkvn_ref[...]
    kv = _dot(ckv.astype(BF16), wukv_ref[...])

    z_ref[:, SEC_D:SEC_D + SEC_W] = project(SEC_D, SEC_D + SEC_W)

    for s in range(2):
        x = z_ref[:, SEC_A + LANES * s:SEC_A + LANES * (s + 1)]
        x = _rope_slab(_head_norm_slab(x, qn_ref[...], ones), cs, sn, ROPE_SHIFT) * (q_scale * LOG2E)
        put_heads_t(qa_ref, 2 * s, x, 0)
    x = z_ref[:, SEC_A + GROUP_WIDTH:SEC_A + GROUP_WIDTH + GQA_KV_WIDTH]
    put_heads(ka_out, 0, _rope_slab(_head_norm_slab(x, kn_ref[...], ones), cs, sn, ROPE_SHIFT))
    put_heads_t(va_ref, 0, z_ref[:, SEC_A + GROUP_WIDTH + GQA_KV_WIDTH:SEC_A + SEC_W])

    kr = _rope_slab(z_ref[:, SEC_B + MLA_Q_RANK + MLA_KV_RANK:SEC_B + SEC_W], csm, snm, ROPE_SHIFT_MLA)
    b_scale = MLA_QK_DIM ** -0.5 * LOG2E
    for h in range(GROUP_HEADS):
        qh = q[:, LANES * h:LANES * (h + 1)]
        qb_ref[h] = (_rope_slab(qh, csm, snm, ROPE_SHIFT_MLA) * b_scale).T.astype(BF16)
        kb_ref[h] = (kv[:, LANES * h:LANES * (h + 1)] + kr).astype(BF16)
    for s in range(2):
        put_heads_t(vb_ref, 2 * s, kv[:, LANES * (GROUP_HEADS + s):LANES * (GROUP_HEADS + s + 1)])

    def put_chunks_t(ref, base, slab, ones_rows):
        st = slab.T.astype(BF16)
        for hh in range(2):
            for cc in range(TM // WINDOW):
                blk = st[hd * hh:hd * (hh + 1), WINDOW * cc:WINDOW * (cc + 1)]
                if ones_rows:
                    blk = jnp.concatenate([blk, jnp.ones((ones_rows, WINDOW), BF16)], axis=0)
                ref[base + hh, cc] = blk

    for s in range(2):
        x = z_ref[:, SEC_D + LANES * s:SEC_D + LANES * (s + 1)]
        put_chunks_t(qd_ref, 2 * s, _rope_slab(x, cs, sn, ROPE_SHIFT) * q_scale, 0)
    put_heads(kd_out, 0, _rope_slab(z_ref[:, SEC_D + GROUP_WIDTH:SEC_D + GROUP_WIDTH + GQA_KV_WIDTH], cs, sn, ROPE_SHIFT))
    put_chunks_t(vd_ref, 0, z_ref[:, SEC_D + GROUP_WIDTH + GQA_KV_WIDTH:SEC_D + SEC_W], ONES_ROWS)

    z = zc_s[0:TM, :]
    rows = lax.broadcasted_iota(jnp.int32, z.shape, 0)
    prev_row = jnp.where(j <= 1, 0.0, zc_s[TM + 7:TM + 8, :])
    next_row = jnp.where(jnp.logical_or(j == 0, j == n_tiles - 1), 0.0, zc_s[TM + 8:TM + 9, :])
    z_prev = jnp.where(rows == 0, prev_row, pltpu.roll(z, 1, axis=0))
    z_next = jnp.where(rows == TM - 1, next_row, pltpu.roll(z, TM - 1, axis=0))
    zs = z + mu_ref[...] * (0.5 * (z_prev + z_next) - z)
    gw = GROUP_WIDTH
    r, k, v = zs[:, 0:gw], zs[:, gw:2 * gw], zs[:, 2 * gw:3 * gw]
    lora = zs[:, 3 * gw:3 * gw + LANES]
    gl = zs[:, 3 * gw + LANES:3 * gw + LANES + gw]
    wt = jnp.tanh(lora)

    def seg_sum(t):
        return jnp.concatenate([_mm_exact_rhs(t[:, :LANES], ones), _mm_exact_rhs(t[:, LANES:], ones)], axis=-1)

    rr_ref[...] = r
    rv_ref[...] = v
    rg_ref[...] = _mm3(_sigmoid(gl), g2_ref[...])
    kd_sum = None
    for dr in range(2):
        u = w0_ref[dr] + _mm3(wt, w2_ref[dr])
        soft = jnp.maximum(-u, 0.0) + jnp.log(1.0 + jnp.exp(-jnp.abs(u)))
        lw_ref[dr] = -jnp.exp(-soft - 0.5)
        gate = _sigmoid(a0_ref[dr] + _mm3(lora, a2_ref[dr]))
        kk = k * kk_ref[dr]
        kk = kk / jnp.maximum(jnp.sqrt(seg_sum(kk * kk)), 1e-12)
        kd = k * (1.0 + (gate - 1.0) * ka_ref[dr])
        rkd_ref[dr] = kd
        ra_ref[dr] = -kk
        rb_ref[dr] = kk * gate
        kd_sum = kd if kd_sum is None else kd_sum + kd
    bonus_ref[...] = seg_sum(r * kd_sum * rk_ref[...]) * v


def _inproj_prep(xc, xl, xc_map, xl_map, mod3, tabs, p, mod_row, n, n_tiles):
    cs, sn, csm, snm = tabs
    cols = p['w_in_p'].shape[1]

    def tab_spec():
        return pl.BlockSpec((TM, LANES), lambda i: (i % n_tiles, 0))

    def row_spec(w):
        return pl.BlockSpec((1, w), lambda i: (0, 0))

    def heads_out(nh, w):
        return (pl.BlockSpec((nh, TM, w), lambda i: (0, i, 0)),
                jax.ShapeDtypeStruct((nh, n, w), BF16))

    def heads_out_t(nh, rows=HEAD_DIM + ONES_ROWS):
        return (pl.BlockSpec((nh, None, rows, TM), lambda i: (0, i, 0, 0)),
                jax.ShapeDtypeStruct((nh, n // TM, rows, TM), BF16))

    def chunks_out_t(nh, rows):
        cpt = TM // WINDOW
        return (pl.BlockSpec((nh, cpt, rows, WINDOW), lambda i: (0, i, 0, 0)),
                jax.ShapeDtypeStruct((nh, n // WINDOW, rows, WINDOW), BF16))

    gw = GROUP_WIDTH
    dir_out = (pl.BlockSpec((2, TM, gw), lambda i: (0, i, 0)), jax.ShapeDtypeStruct((2, n, gw), F32))
    one_out = (pl.BlockSpec((TM, gw), lambda i: (i, 0)), jax.ShapeDtypeStruct((n, gw), F32))
    outs = [heads_out_t(4, HEAD_DIM), heads_out(2, 64), heads_out_t(2),
            heads_out_t(4, LANES), heads_out(4, 128), heads_out_t(4),
            chunks_out_t(4, HEAD_DIM), heads_out(2, 64), chunks_out_t(2, HEAD_DIM + ONES_ROWS)] + [dir_out] * 4 + [one_out] * 4
    blk8 = TM // 8
    last8 = xl.shape[0] // 8 - 1

    def full(shape):
        nd = len(shape)
        return pl.BlockSpec(shape, lambda i: (0,) * nd)

    return pl.pallas_call(
        functools.partial(_inproj_prep_kernel, n_tiles=n_tiles),
        grid=(n // TM,),
        in_specs=[pl.BlockSpec((TM, D_MODEL), xc_map), pl.BlockSpec((TM, D_MODEL), xl_map),
                  pl.BlockSpec((8, D_MODEL), lambda i: (jnp.maximum(xl_map(i)[0] * blk8 - 1, 0), 0)),
                  pl.BlockSpec((8, D_MODEL), lambda i: (jnp.minimum((xl_map(i)[0] + 1) * blk8, last8), 0)),
                  pl.BlockSpec((None, 1, 6 * D_MODEL), lambda i: (mod_row(i), 0, 0)),
                  row_spec(D_MODEL),
                  pl.BlockSpec((D_MODEL, cols), lambda i: (0, 0)),
                  tab_spec(), tab_spec(), tab_spec(), tab_spec(),
                  row_spec(LANES), row_spec(LANES), row_spec(MLA_Q_RANK), row_spec(MLA_KV_RANK),
                  pl.BlockSpec((MLA_Q_RANK, GROUP_HEADS * LANES), lambda i: (0, 0)),
                  pl.BlockSpec((MLA_KV_RANK, GROUP_HEADS * LANES + GROUP_WIDTH), lambda i: (0, 0)),
                  pl.BlockSpec((LANES, LANES), lambda i: (0, 0)),
                  full((1, C_COLS_PAD)),
                  full((2, 1, gw)), full((2, LANES, gw)), full((2, 1, gw)), full((2, LANES, gw)),
                  full((2, 1, gw)), full((2, 1, gw)), full((1, gw)), full((gw, gw))],
        out_specs=[o[0] for o in outs],
        out_shape=[o[1] for o in outs],
        scratch_shapes=[pltpu.VMEM((TM, ABD_COLS), F32), pltpu.VMEM((TM + 16, C_COLS_PAD), F32)],
        compiler_params=_cparams(("parallel",)),
        name="inproj_prep",
    )(xc, xl, xl, xl, mod3, p['g_pre_mix'], p['w_in_p'], cs, sn, csm, snm,
      p['qn'], p['kn'], p['mqn'], p['mkvn'], p['wuq'], p['wukv'], p['ones64'],
      p['mu'], p['w0'], p['w2p'], p['a0'], p['a2p'], p['k_k'], p['k_a'], p['r_k'], p['g2p'])


def _flash_kernel(q_ref, k_ref, vt_ref, o_ref, *, shared_kv, tq, tk, ctx_len, s_tot):
    qi = pl.program_id(2)
    dv = vt_ref.shape[-2] - ONES_ROWS
    if shared_kv:
        streams = [(jnp.concatenate([q_ref[0], q_ref[1]], axis=1), 0)]
    else:
        streams = [(q_ref[0], 0), (q_ref[1], 1)]

    def scores(c):
        out = []
        for q, kv in streams:
            k = k_ref[kv, pl.ds(pl.multiple_of(c * tk, tk), tk), :]
            out.append(_dot(k, q))
        return out

    def softmax_pv(c, sts, carries):
        new = []
        for st, (_, kv), (m, acc) in zip(sts, streams, carries):
            m_new = jnp.maximum(m, jnp.max(st, axis=0, keepdims=True))
            pt = jnp.exp2(st - m_new).astype(BF16)
            acc = jnp.exp2(m - m_new) * acc + _dot(vt_ref[kv, c], pt)
            new.append((m_new, acc))
        return tuple(new)

    def run(n_chunks):
        unroll = next(u for u in (33, 11, 3, 2, 1) if n_chunks % u == 0)
        groups = n_chunks // unroll
        init = tuple((jnp.full((1, q.shape[1]), NEG_INF, F32), jnp.zeros((dv + ONES_ROWS, q.shape[1]), F32))
                     for q, _ in streams)

        def body(g, carries):
            queue = [scores(g * unroll + u) for u in range(min(LOOKAHEAD, unroll))]
            for u in range(unroll):
                if u + LOOKAHEAD < unroll:
                    queue.append(scores(g * unroll + u + LOOKAHEAD))
                carries = softmax_pv(g * unroll + u, queue.pop(0), carries)
            return carries

        carries = body(0, init) if groups == 1 else lax.fori_loop(0, groups, body, init)
        outs = [acc[:dv] / acc[dv:dv + 1] for _, acc in carries]
        if shared_kv:
            outs = [outs[0][:, :tq], outs[0][:, tq:]]
        o_ref[...] = jnp.concatenate(outs, axis=0).T.astype(o_ref.dtype)

    is_ctx = qi * tq < ctx_len

    @pl.when(is_ctx)
    def _():
        run(ctx_len // tk)

    @pl.when(jnp.logical_not(is_ctx))
    def _():
        run(s_tot // tk)


def _flash(qt, k, vt, *, shared_kv, batch, s_tot, ctx_len):
    nh, _, dk, _ = qt.shape
    n = k.shape[1]
    dv = vt.shape[-2] - ONES_ROWS
    nkv = 1 if shared_kv else 2
    tq = tk = TM
    nq = s_tot // tq
    kern = functools.partial(_flash_kernel, shared_kv=shared_kv, tq=tq, tk=tk,
                             ctx_len=ctx_len, s_tot=s_tot)
    return pl.pallas_call(
        kern,
        grid=(batch, nh // 2, nq),
        in_specs=[pl.BlockSpec((2, None, dk, tq), lambda b, p, i: (p, b * nq + i, 0, 0)),
                  pl.BlockSpec((nkv, s_tot, dk), lambda b, p, i: (p, b, 0)),
                  pl.BlockSpec((nkv, s_tot // tk, dv + ONES_ROWS, tk), lambda b, p, i: (p, b, 0, 0))],
        out_specs=pl.BlockSpec((tq, 2 * dv), lambda b, p, i: (b * nq + i, p)),
        out_shape=jax.ShapeDtypeStruct((n, nh * dv), BF16),
        compiler_params=_cparams(("parallel", "parallel", "arbitrary")),
        name="flash_shared" if shared_kv else "flash_split",
    )(qt, k, vt)


def _window_kernel(sink_ref, bias_ref, q_ref, k_ref, vt_ref, o_ref, *, ctx_len, s_tot, n_blocks):
    w = WINDOW
    g = pl.program_id(1)
    first_lat = ctx_len // w
    last_blk = s_tot // w - 1
    lane = lax.broadcasted_iota(jnp.int32, (1, 2 * w), 1)
    sk = jnp.where(lane < w, sink_ref[2 * g], sink_ref[2 * g + 1])
    k_ctx = k_ref[0:ctx_len, :]
    vt_ctx = jnp.concatenate([vt_ref[c] for c in range(first_lat)], axis=1)

    def rows(blk):
        return k_ref[pl.ds(pl.multiple_of(blk * w, w), w), :]

    def neighbours(i):
        jb = pl.program_id(2) * n_blocks + i
        return jnp.clip(jb - 1, 0, last_blk), jb, jnp.clip(jb + 1, 0, last_blk)

    def scores(i):
        pb, jb, nb = neighbours(i)
        kind = jnp.where(jb < first_lat, 0, jnp.where(jb == first_lat, 1, jnp.where(jb == last_blk, 3, 2)))
        qt = jnp.concatenate([q_ref[0, i], q_ref[1, i]], axis=1)
        kw = jnp.concatenate([rows(pb), rows(jb), rows(nb)], axis=0)
        return _dot(kw, qt) + bias_ref[kind], _dot(k_ctx, qt)

    queue = [scores(i) for i in range(min(LOOKAHEAD, n_blocks))]
    for i in range(n_blocks):
        if i + LOOKAHEAD < n_blocks:
            queue.append(scores(i + LOOKAHEAD))
        s_w, s_c = queue.pop(0)
        pb, jb, nb = neighbours(i)
        m = jnp.maximum(jnp.maximum(jnp.max(s_w, axis=0, keepdims=True),
                                    jnp.max(s_c, axis=0, keepdims=True)), sk)
        p_w = jnp.exp(s_w - m).astype(BF16)
        p_c = jnp.exp(s_c - m).astype(BF16)
        vt_w = jnp.concatenate([vt_ref[pb], vt_ref[jb], vt_ref[nb]], axis=1)
        acc = _dot(vt_w, p_w) + _dot(vt_ctx, p_c)
        o = acc[:HEAD_DIM] / (acc[HEAD_DIM:HEAD_DIM + 1] + jnp.exp(sk - m))
        o_ref[w * i:w * (i + 1), :] = jnp.concatenate([o[:, :w], o[:, w:]], axis=0).T.astype(o_ref.dtype)


def _window_bias():
    w = WINDOW
    c = jnp.arange(3 * w)[:, None]
    r = jnp.arange(2 * w)[None, :] % w
    band = (c >= r) & (c <= r + 2 * w)
    kinds = [jnp.zeros_like(band), band & (c >= w), band, band & (c < 2 * w)]
    return jnp.where(jnp.stack(kinds), 0.0, NEG_INF).astype(F32)


def _window_attn(sink, qt, k, vt, *, batch, s_tot, ctx_len):
    nh, _, dk, _ = qt.shape
    n = k.shape[1]
    nblk = s_tot // WINDOW
    wb = next(u for u in (WINDOW_BLOCKS, 11, 6, 3, 2, 1) if nblk % u == 0)
    nstep = nblk // wb
    kern = functools.partial(_window_kernel, ctx_len=ctx_len, s_tot=s_tot, n_blocks=wb)
    return pl.pallas_call(
        kern,
        grid=(batch, nh // 2, nstep),
        in_specs=[pl.BlockSpec(memory_space=pltpu.SMEM),
                  pl.BlockSpec((4, 3 * WINDOW, 2 * WINDOW), lambda b, g, j: (0, 0, 0)),
                  pl.BlockSpec((2, wb, dk, WINDOW), lambda b, g, j: (g, b * nstep + j, 0, 0)),
                  pl.BlockSpec((None, s_tot, dk), lambda b, g, j: (g, b, 0)),
                  pl.BlockSpec((None, nblk, dk + ONES_ROWS, WINDOW), lambda b, g, j: (g, b, 0, 0))],
        out_specs=pl.BlockSpec((WINDOW * wb, 2 * dk), lambda b, g, j: (b * nstep + j, g)),
        out_shape=jax.ShapeDtypeStruct((n, nh * dk), BF16),
        compiler_params=_cparams(("parallel", "parallel", "arbitrary")),
        name="window_attn",
    )(sink, _window_bias(), qt, k, vt)


def _rwkv_scan_kernel(lw_ref, k_ref, a_ref, b_ref, r_ref, v_ref, y_ref, s_ref, *, reverse):
    @pl.when(pl.program_id(0) == 0)
    def _():
        s_ref[...] = jnp.zeros_like(s_ref)

    batch = lw_ref.shape[0]
    n_chunks = TM // CHUNK
    hd = HEAD_DIM
    row = lax.broadcasted_iota(jnp.int32, (TM, TM), 0)
    col = lax.broadcasted_iota(jnp.int32, (TM, TM), 1)
    same = (row // CHUNK) == (col // CHUNK)
    before = (col > row) if reverse else (col < row)
    m_strict = jnp.logical_and(same, before)
    m_incl = jnp.logical_and(same, jnp.logical_or(before, row == col))
    m_incl16 = m_incl.astype(BF16)
    eye = (row == col).astype(F32)
    assert CHUNK // SUB == 4
    same_sub = (row // SUB) == (col // SUB)

    items = [(b, h) for b in range(batch) for h in range(GROUP_HEADS)]
    idx = range(len(items))
    e_pos, at_all, rt_all, bt_all, kt_all, v_all = [], [], [], [], [], []
    for b in range(batch):
        lw = lw_ref[b]
        cum = _mm_exact_lhs(m_incl16, lw)
        e_pos.append(jnp.exp(cum))
        e_neg = jnp.exp(-cum)
        at_all.append(a_ref[b] * jnp.exp(cum - lw))
        rt_all.append(r_ref[b] * e_pos[b])
        bt_all.append(b_ref[b] * e_neg)
        kt_all.append(k_ref[b] * e_neg)
        v_all.append(v_ref[b])

    def head(arrs, b, h):
        return arrs[b][:, hd * h:hd * (h + 1)]

    at = [head(at_all, b, h) for b, h in items]
    rt = [head(rt_all, b, h) for b, h in items]
    bt = [head(bt_all, b, h) for b, h in items]
    at16 = [x.astype(BF16) for x in at]
    rt16 = [x.astype(BF16) for x in rt]
    bt16 = [x.astype(BF16) for x in bt]
    kt16 = [head(kt_all, b, h).astype(BF16) for b, h in items]
    v16 = [head(v_all, b, h).astype(BF16) for b, h in items]
    ab = [jnp.where(m_strict, _dot_nt(at16[i], bt16[i]), 0.0) for i in idx]
    ak16 = [jnp.where(m_strict, _dot_nt(at16[i], kt16[i]), 0.0).astype(BF16) for i in idx]
    rk16 = [jnp.where(m_incl, _dot_nt(rt16[i], kt16[i]), 0.0).astype(BF16) for i in idx]
    n_c = TM // CHUNK

    def wide(m):
        out = m[:CHUNK]
        for c in range(1, n_c):
            out = out + m[CHUNK * c:CHUNK * (c + 1)]
        return out

    def bdiag16(wm):
        return jnp.where(same, jnp.concatenate([wm] * n_c, axis=0), 0.0).astype(BF16)

    def wdot(wl, bd16):
        return _dot(wl.astype(BF16), bd16)

    eye_w = wide(eye)
    ld = [jnp.where(same_sub, x, 0.0) for x in ab]
    lo16 = [jnp.where(same_sub, 0.0, x).astype(BF16) for x in ab]
    pw_w = [wide(x) for x in ld]
    pw_bd = [x.astype(BF16) for x in ld]
    td_w = [eye_w + x for x in pw_w]
    for _ in range(int(math.log2(SUB)) - 1):
        pw_w = [wdot(x, y) for x, y in zip(pw_w, pw_bd)]
        pw_bd = [bdiag16(x) for x in pw_w]
        td_w = [t + wdot(t, y) for t, y in zip(td_w, pw_bd)]
    m1_w = [wdot(t, x) for t, x in zip(td_w, lo16)]
    m1_bd = [bdiag16(x) for x in m1_w]
    m2_w = [wdot(x, y) for x, y in zip(m1_w, m1_bd)]
    m3_w = [wdot(x, bdiag16(y)) for x, y in zip(m1_w, m2_w)]
    nn_w = [eye_w + a1 + a2 + a3 for a1, a2, a3 in zip(m1_w, m2_w, m3_w)]
    akv = [_dot(ak16[i], v16[i]) for i in idx]

    rowc = lax.broadcasted_iota(jnp.int32, (TM, n_c * 2 * hd), 0) // CHUNK
    colc = lax.broadcasted_iota(jnp.int32, (TM, n_c * 2 * hd), 1) // (2 * hd)
    same_x = rowc == colc

    def spread16(xr):
        return jnp.where(same_x, jnp.concatenate([xr] * n_c, axis=1), 0.0).astype(BF16)

    def unwide(xw):
        return jnp.concatenate([xw[:, 2 * hd * c:2 * hd * (c + 1)] for c in range(n_c)], axis=0)

    tx = [wdot(td_w[i], spread16(jnp.concatenate([at[i], akv[i]], axis=-1))) for i in idx]
    wu = [unwide(wdot(nn_w[i], spread16(unwide(tx[i])))) for i in idx]
    wa = [x[:, :hd] for x in wu]
    u016 = [x[:, hd:].astype(BF16) for x in wu]
    rb_w = [wide(jnp.where(m_incl, _dot_nt(rt16[i], bt16[i]), 0.0)) for i in idx]
    rbwu = [unwide(wdot(rb_w[i], spread16(wu[i]))) for i in idx]
    yr16 = [(rt[i] + rbwu[i][:, :hd]).astype(BF16) for i in idx]
    y0 = [rbwu[i][:, hd:] + _dot(rk16[i], v16[i]) for i in idx]

    order = range(n_chunks - 1, -1, -1) if reverse else range(n_chunks)
    eye_h = eye[:hd, :hd]
    trans = {}
    for c in order:
        rs = slice(CHUNK * c, CHUNK * (c + 1))
        last = CHUNK * c if reverse else CHUNK * (c + 1) - 1
        for i, (b, h) in enumerate(items):
            decay = e_pos[b][last:last + 1, hd * h:hd * (h + 1)]
            g = (eye_h + _mm3tn(wa[i][rs], bt[i][rs])) * decay
            hh = (_dot_tn(u016[i][rs], bt16[i][rs]) + _dot_tn(v16[i][rs], kt16[i][rs])) * decay
            trans[c, i] = (g, hh)
    for c in order:
        rs = slice(CHUNK * c, CHUNK * (c + 1))
        s0 = [s_ref[b, h] for b, h in items]
        for i, (b, h) in enumerate(items):
            s_ref[b, h] = _mm3(s0[i], trans[c, i][0]) + trans[c, i][1]
        for b in range(batch):
            y_ref[b, rs, :] = jnp.concatenate(
                [_dot_nt(yr16[i][rs], s0[i].astype(BF16)) + y0[i][rs] for i, (bb, _) in enumerate(items) if bb == b],
                axis=-1)


def _mm3tn(a, b):
    ah, al = _split2(a)
    bh, bl = _split2(b)
    return _dot_tn(ah, bh) + (_dot_tn(ah, bl) + _dot_tn(al, bh))


def _rwkv_scan(lw, kd, a, b, r, v, *, d, batch, n_tiles):
    n, gw = r.shape
    s_tot = n // batch
    reverse = d == 1

    def tile(s):
        if reverse:
            return jnp.where(s == 0, 0, n_tiles - s)
        return s

    dspec = pl.BlockSpec((None, batch, TM, gw), lambda s: (d, 0, tile(s), 0))
    spec = pl.BlockSpec((batch, TM, gw), lambda s: (0, tile(s), 0))
    per_dir = [x.reshape(2, batch, s_tot, gw) for x in (lw, kd, a, b)]
    shared = [x.reshape(batch, s_tot, gw) for x in (r, v)]
    y = pl.pallas_call(
        functools.partial(_rwkv_scan_kernel, reverse=reverse),
        grid=(n_tiles,),
        in_specs=[dspec, dspec, dspec, dspec, spec, spec],
        out_specs=spec,
        out_shape=jax.ShapeDtypeStruct((batch, s_tot, gw), F32),
        scratch_shapes=[pltpu.VMEM((batch, GROUP_HEADS, HEAD_DIM, HEAD_DIM), F32)],
        compiler_params=_cparams(("arbitrary",)),
        name="rwkv_scan_rev" if reverse else "rwkv_scan_fwd",
    )(*per_dir, *shared)
    return y.reshape(n, gw)


def _out_mlp_kernel(xc_ref, xl_ref, oa_ref, ob_ref, od_ref, yf_ref, yr_ref, bonus_ref, g_ref,
                    lnw_ref, lnb_ref, ones_ref, mod_ref, gpost_ref, gpre_ref, gpm_ref,
                    wo_ref, w1_ref, w2_ref, xo_ref, *, ctx_first):
    d = D_MODEL
    gw = GROUP_WIDTH
    mod = mod_ref[...]
    ones = ones_ref[...]

    def seg_mean(t):
        sm = jnp.concatenate([_mm_exact_rhs(t[:, :LANES], ones), _mm_exact_rhs(t[:, LANES:], ones)], axis=-1)
        return sm * (1.0 / HEAD_DIM)

    yy = yf_ref[...] + yr_ref[...]
    dlt = yy - seg_mean(yy)
    yn = dlt * lax.rsqrt(seg_mean(dlt * dlt) + RWKV_GN_EPS) * lnw_ref[...] + lnb_ref[...]
    oc = ((yn + bonus_ref[...]) * g_ref[...]).astype(BF16)

    y = _dot(oa_ref[...], wo_ref[0:gw, :])
    y += _dot(ob_ref[...], wo_ref[gw:2 * gw, :])
    y += _dot(oc, wo_ref[2 * gw:3 * gw, :])
    y += _dot(od_ref[...], wo_ref[3 * gw:4 * gw, :])
    x = jnp.where(jnp.logical_and(ctx_first, pl.program_id(1) == 0), xc_ref[...], xl_ref[...])
    x1 = x + mod[:, 2 * d:3 * d] * (_rms(y, NORM_EPS) * gpost_ref[...])
    h = _rms(x1, NORM_EPS) * gpre_ref[...] * (1.0 + mod[:, 4 * d:5 * d]) + mod[:, 3 * d:4 * d]
    u = jnp.maximum(_dot(h.astype(BF16), w1_ref[...]), 0.0)
    zz = _dot((u * u).astype(BF16), w2_ref[...])
    xo_ref[...] = x1 + mod[:, 5 * d:6 * d] * (_rms(zz, NORM_EPS) * gpm_ref[...])


def _out_mlp(xc, xl, xc_map, xl_map, oa, ob, od, yf, yr, bonus, rg, mod3, p, *, batch, n_tiles, skip_ctx):
    d = D_MODEL
    gw = GROUP_WIDTH
    off = 1 if skip_ctx else 0
    nt = n_tiles - off

    def tile(b, j):
        return (b * n_tiles + off + j, 0)

    def mrow(b, j):
        return (jnp.where(off + j == 0, batch, b), 0, 0)

    ospec = pl.BlockSpec((TM, gw), tile)
    row = pl.BlockSpec((1, d), lambda b, j: (0, 0))
    grow = pl.BlockSpec((1, gw), lambda b, j: (0, 0))

    def wspec(shape):
        return pl.BlockSpec(shape, lambda b, j: (0, 0), pipeline_mode=pl.Buffered(1))

    return pl.pallas_call(
        functools.partial(_out_mlp_kernel, ctx_first=not skip_ctx),
        grid=(batch, nt),
        in_specs=[pl.BlockSpec((TM, d), lambda b, j: xc_map(b, off + j)),
                  pl.BlockSpec((TM, d), lambda b, j: xl_map(b, off + j)),
                  ospec, ospec, ospec, ospec, ospec, ospec, ospec,
                  grow, grow, pl.BlockSpec((LANES, LANES), lambda b, j: (0, 0)),
                  pl.BlockSpec((None, 1, 6 * d), mrow),
                  row, row, row,
                  wspec((d, d)), wspec((d, D_FF)), wspec((D_FF, d))],
        out_specs=pl.BlockSpec((TM, d), lambda b, j: (b * nt + j, 0)),
        out_shape=jax.ShapeDtypeStruct((batch * nt * TM, d), F32),
        compiler_params=_cparams(("parallel", "parallel")),
        name="out_mlp",
    )(xc, xl, oa, ob, od, yf, yr, bonus, rg, p['ln_w'], p['ln_b'], p['ones64'], mod3,
      p['g_post_mix'], p['g_pre_mlp'], p['g_post_mlp'], p['w_out'], p['w_mlp1'], p['w_mlp2'])


def _rope_tables(seq, ctx_len):
    t = jnp.arange(seq, dtype=jnp.int32)
    row = (t // GRID_W).astype(F32)
    col = (t % GRID_W).astype(F32)

    def tables(rot_dim, lane_dim):
        n = rot_dim // 4
        inv = ROPE_THETA ** (-jnp.arange(n, dtype=F32) / n)
        ar, ac = row[:, None] * inv, col[:, None] * inv
        cos = jnp.concatenate([jnp.cos(ar), jnp.cos(ar), jnp.cos(ac), jnp.cos(ac)], axis=-1)
        sin = jnp.concatenate([-jnp.sin(ar), jnp.sin(ar), -jnp.sin(ac), jnp.sin(ac)], axis=-1)
        return cos, sin

    def with_ctx(tab, fill):
        return jnp.concatenate([jnp.full((ctx_len, tab.shape[1]), fill, F32), tab], axis=0)

    cos, sin = tables(HEAD_DIM, LANES)
    cs = with_ctx(jnp.tile(cos, (1, LANES // HEAD_DIM)), 1.0)
    sn = with_ctx(jnp.tile(sin, (1, LANES // HEAD_DIM)), 0.0)
    cosm, sinm = tables(MLA_ROPE_DIM, LANES)
    pad_l, pad_r = MLA_NOPE_DIM, LANES - MLA_QK_DIM
    csm = with_ctx(jnp.pad(cosm, ((0, 0), (pad_l, pad_r)), constant_values=1.0), 1.0)
    snm = with_ctx(jnp.pad(sinm, ((0, 0), (pad_l, pad_r))), 0.0)
    return cs, sn, csm, snm


def _layer_params(l, w):
    d = D_MODEL
    gw = GROUP_WIDTH
    w_in = w['w_in'][l].astype(BF16)
    a_end, b_end, c_end = A_COLS, A_COLS + B_COLS, A_COLS + B_COLS + C_COLS
    lowrank = MLA_Q_RANK + MLA_KV_RANK
    zeros = lambda c: jnp.zeros((d, c), BF16)
    w_in_p = jnp.concatenate([
        w_in[:, :a_end],
        w_in[:, a_end:a_end + lowrank], zeros(MLA_NOPE_DIM), w_in[:, a_end + lowrank:b_end], zeros(LANES - MLA_QK_DIM),
        w_in[:, c_end:],
        w_in[:, b_end:c_end], zeros(C_COLS_PAD - C_COLS)], axis=1)
    wuq = w['mla_w_uq'][l].reshape(MLA_Q_RANK, GROUP_HEADS, MLA_QK_DIM)
    wuq = jnp.pad(wuq, ((0, 0), (0, 0), (0, LANES - MLA_QK_DIM))).reshape(MLA_Q_RANK, GROUP_HEADS * LANES)
    wukv = w['mla_w_ukv'][l].reshape(MLA_KV_RANK, GROUP_HEADS, MLA_NOPE_DIM + HEAD_DIM)
    wk = jnp.pad(wukv[:, :, :MLA_NOPE_DIM], ((0, 0), (0, 0), (0, LANES - MLA_NOPE_DIM)))
    wukv_p = jnp.concatenate([wk.reshape(MLA_KV_RANK, GROUP_HEADS * LANES),
                              wukv[:, :, MLA_NOPE_DIM:].reshape(MLA_KV_RANK, gw)], axis=1)
    lane = jnp.arange(LANES)
    ones64 = ((lane[:, None] // HEAD_DIM) == (lane[None, :] // HEAD_DIM)).astype(BF16)
    zrows = lambda r: jnp.zeros((2, r, gw), F32)
    return {
        'w_in_p': w_in_p,
        'g_pre_mix': w['g_pre_mix'][l].reshape(1, d),
        'g_post_mix': w['g_post_mix'][l].reshape(1, d),
        'g_pre_mlp': w['g_pre_mlp'][l].reshape(1, d),
        'g_post_mlp': w['g_post_mlp'][l].reshape(1, d),
        'qn': jnp.tile(w['gqa_q_norm'][l], 2).reshape(1, LANES),
        'kn': jnp.tile(w['gqa_k_norm'][l], 2).reshape(1, LANES),
        'mqn': w['mla_q_norm'][l].reshape(1, MLA_Q_RANK),
        'mkvn': w['mla_kv_norm'][l].reshape(1, MLA_KV_RANK),
        'wuq': wuq.astype(BF16),
        'wukv': wukv_p.astype(BF16),
        'ones64': ones64,
        'mu': jnp.pad(w['rwkv_mu'][l], (0, C_COLS_PAD - C_COLS)).reshape(1, C_COLS_PAD),
        'w0': w['rwkv_w0'][l].reshape(2, 1, gw),
        'w2p': jnp.concatenate([w['rwkv_w2'][l], zrows(LANES - RWKV_DECAY_LORA)], axis=1),
        'a0': w['rwkv_a0'][l].reshape(2, 1, gw),
        'a2p': jnp.concatenate([zrows(RWKV_DECAY_LORA), w['rwkv_a2'][l],
                                zrows(LANES - RWKV_DECAY_LORA - RWKV_ICLR_LORA)], axis=1),
        'k_k': w['rwkv_k_k'][l].reshape(2, 1, gw),
        'k_a': w['rwkv_k_a'][l].reshape(2, 1, gw),
        'r_k': w['rwkv_r_k'][l].reshape(1, gw),
        'g2p': jnp.pad(w['rwkv_g2'][l], ((0, gw - RWKV_GATE_LORA), (0, 0))),
        'ln_w': w['rwkv_ln_w'][l].reshape(1, gw),
        'ln_b': w['rwkv_ln_b'][l].reshape(1, gw),
        'sink': w['swa_sink'][l],
        'w_out': w['w_out'][l].astype(BF16),
        'w_mlp1': w['w_mlp1'][l].astype(BF16),
        'w_mlp2': w['w_mlp2'][l].astype(BF16),
    }


def kernel(x, c, ctx, c_ctx, w_mod, b_mod, g_pre_mix, g_post_mix, g_pre_mlp, g_post_mlp, w_in, gqa_q_norm, gqa_k_norm, mla_q_norm, mla_kv_norm, mla_w_uq, mla_w_ukv, rwkv_mu, rwkv_w0, rwkv_w2, rwkv_a0, rwkv_a2, rwkv_k_k, rwkv_k_a, rwkv_r_k, rwkv_g2, rwkv_ln_w, rwkv_ln_b, swa_sink, w_out, w_mlp1, w_mlp2):
    batch, seq, d = x.shape
    ctx_len = ctx.shape[1]
    depth = w_mod.shape[0]
    assert d == D_MODEL and ctx_len == TM and seq % TM == 0 and seq % GRID_W == 0
    assert batch + 1 <= 8
    s_tot = ctx_len + seq
    n_tiles = s_tot // TM
    w = dict(w_in=w_in, g_pre_mix=g_pre_mix, g_post_mix=g_post_mix, g_pre_mlp=g_pre_mlp,
             g_post_mlp=g_post_mlp, gqa_q_norm=gqa_q_norm, gqa_k_norm=gqa_k_norm, mla_q_norm=mla_q_norm,
             mla_kv_norm=mla_kv_norm, mla_w_uq=mla_w_uq, mla_w_ukv=mla_w_ukv, rwkv_mu=rwkv_mu,
             rwkv_w0=rwkv_w0, rwkv_w2=rwkv_w2, rwkv_a0=rwkv_a0, rwkv_a2=rwkv_a2, rwkv_k_k=rwkv_k_k,
             rwkv_k_a=rwkv_k_a, rwkv_r_k=rwkv_r_k, rwkv_g2=rwkv_g2, rwkv_ln_w=rwkv_ln_w,
             rwkv_ln_b=rwkv_ln_b, swa_sink=swa_sink, w_out=w_out, w_mlp1=w_mlp1, w_mlp2=w_mlp2)

    ct = jnp.concatenate([c, c_ctx[None, :], jnp.zeros((8 - batch - 1, d), F32)], axis=0).T
    mods = _modulation(ct, w_mod, b_mod, batch + 1)

    def mod_row(i):
        return jnp.where(i % n_tiles == 0, batch, i // n_tiles)

    tabs = _rope_tables(seq, ctx_len)
    geo = dict(batch=batch, s_tot=s_tot, ctx_len=ctx_len)
    n = batch * s_tot
    n_lat = n_tiles - 1
    xc, xl = ctx.reshape(batch * ctx_len, d), x.reshape(batch * seq, d)
    xc_map = lambda b, j: (b, 0)
    xl_map = lambda b, j: (b * n_lat + jnp.maximum(j - 1, 0), 0)
    for l in range(depth):
        p = _layer_params(l, w)
        mod3 = mods[l].reshape(8, 1, 6 * d)
        flat = lambda m: (lambda i: m(i // n_tiles, i % n_tiles))
        qa, ka, va, qb, kb, vb, qd, kd, vd, lw, rkd, ra, rb, rr, rv, rg, bonus = _inproj_prep(
            xc, xl, flat(xc_map), flat(xl_map), mod3, tabs, p, mod_row, n, n_tiles)
        oa = _flash(qa, ka, va, shared_kv=True, **geo)
        ob = _flash(qb, kb, vb, shared_kv=False, **geo)
        od = _window_attn(p['sink'], qd, kd, vd, **geo)
        yf = _rwkv_scan(lw, rkd, ra, rb, rr, rv, d=0, batch=batch, n_tiles=n_tiles)
        yr = _rwkv_scan(lw, rkd, ra, rb, rr, rv, d=1, batch=batch, n_tiles=n_tiles)
        xs = _out_mlp(xc, xl, xc_map, xl_map, oa, ob, od, yf, yr, bonus, rg, mod3, p,
                      batch=batch, n_tiles=n_tiles, skip_ctx=l == depth - 1)
        xc = xl = xs
        xc_map = lambda b, j: (b * n_tiles, 0)
        xl_map = lambda b, j: (b * n_tiles + j, 0)
    return xs.reshape(batch, seq, d)
```

```python
import functools
import math

import jax
import jax.numpy as jnp
from jax import lax
from jax.experimental import pallas as pl
from jax.experimental.pallas import tpu as pltpu

F32 = jnp.float32
BF16 = jnp.bfloat16

D_MODEL = 1024
GRID_W = 64
HEAD_DIM = 64
GROUP_WIDTH = 256
GROUP_HEADS = 4
ROPE_THETA = 10000.0
NORM_EPS = 1e-6
NEG_INF = -1e30
MLA_Q_RANK = 256
MLA_KV_RANK = 128
MLA_NOPE_DIM = 64
MLA_ROPE_DIM = 32
MLA_QK_DIM = MLA_NOPE_DIM + MLA_ROPE_DIM
RWKV_DECAY_LORA = 64
RWKV_ICLR_LORA = 64
RWKV_GATE_LORA = 160
RWKV_GN_EPS = 64e-5
WINDOW = 128
D_FF = 4 * D_MODEL

LOG2E = math.log2(math.e)
LANES = 128
TM = 256
ONES_ROWS = 8
LOOKAHEAD = 2
WINDOW_BLOCKS = 33
CHUNK = 64
SUB = 16
GQA_KV_WIDTH = 2 * HEAD_DIM
A_COLS = GROUP_WIDTH + 2 * GQA_KV_WIDTH
B_COLS = MLA_Q_RANK + MLA_KV_RANK + MLA_ROPE_DIM
C_COLS = 3 * GROUP_WIDTH + RWKV_DECAY_LORA + RWKV_ICLR_LORA + RWKV_GATE_LORA
SEC_W = 512
SEC_A = 0
SEC_B = SEC_A + SEC_W
SEC_D = SEC_B + SEC_W
ABD_COLS = SEC_D + SEC_W
C_COLS_PAD = 1152
MOD_BLOCK = 1536
ROPE_SHIFT = HEAD_DIM // 4
ROPE_SHIFT_MLA = MLA_ROPE_DIM // 4
VMEM_LIMIT = 56 * 1024 * 1024


def _cparams(sem):
    return pltpu.CompilerParams(dimension_semantics=sem, vmem_limit_bytes=VMEM_LIMIT)


def _dot(a, b):
    return jnp.dot(a, b, preferred_element_type=F32)


def _dot_nt(a, b):
    return lax.dot_general(a, b, (((1,), (1,)), ((), ())), preferred_element_type=F32)


def _dot_tn(a, b):
    return lax.dot_general(a, b, (((0,), (0,)), ((), ())), preferred_element_type=F32)


def _split2(x):
    hi = x.astype(BF16)
    lo = (x - hi.astype(F32)).astype(BF16)
    return hi, lo


def _mm_exact_rhs(a, b_bf16):
    hi, lo = _split2(a)
    return _dot(hi, b_bf16) + _dot(lo, b_bf16)


def _mm_exact_lhs(a_bf16, b):
    hi, lo = _split2(b)
    return _dot(a_bf16, hi) + _dot(a_bf16, lo)


def _mm3(a, b):
    ah, al = _split2(a)
    bh, bl = _split2(b)
    return _dot(ah, bh) + (_dot(ah, bl) + _dot(al, bh))


def _sigmoid(x):
    return 1.0 / (1.0 + jnp.exp(-x))


def _rms(x, eps):
    return x * lax.rsqrt(jnp.mean(x * x, axis=-1, keepdims=True) + eps)


def _mod_kernel(ct_ref, w_ref, b_ref, o_ref, *, n_rows):
    ct = ct_ref[...]
    st = ct * _sigmoid(ct)
    w = w_ref[...]
    rows = [jnp.sum(st[:, r:r + 1] * w, axis=0, keepdims=True) for r in range(n_rows)]
    rows.append(jnp.zeros((8 - n_rows, w.shape[1]), F32))
    o_ref[...] = jnp.concatenate(rows, axis=0) + b_ref[...]


def _modulation(ct, w_mod, b_mod, n_rows):
    depth, d, n6 = w_mod.shape
    tn = MOD_BLOCK
    return pl.pallas_call(
        functools.partial(_mod_kernel, n_rows=n_rows),
        grid=(depth, n6 // tn),
        in_specs=[pl.BlockSpec((d, 8), lambda l, j: (0, 0)),
                  pl.BlockSpec((None, d, tn), lambda l, j: (l, 0, j)),
                  pl.BlockSpec((None, 1, tn), lambda l, j: (l, 0, j))],
        out_specs=pl.BlockSpec((None, 8, tn), lambda l, j: (l, 0, j)),
        out_shape=jax.ShapeDtypeStruct((depth, 8, n6), F32),
        compiler_params=_cparams(("parallel", "parallel")),
        name="modulation",
    )(ct, w_mod, b_mod.reshape(depth, 1, n6))


def _rope_slab(x, cos, sin_signed, shift):
    left = pltpu.roll(x, LANES - shift, axis=1)
    right = pltpu.roll(x, shift, axis=1)
    lane = lax.broadcasted_iota(jnp.int32, x.shape, 1)
    first = ((lane // shift) % 2) == 0
    return x * cos + jnp.where(first, left, right) * sin_signed


def _head_norm_slab(x, gain, ones):
    ss = _mm_exact_rhs(x * x, ones)
    return x * lax.rsqrt(ss * (1.0 / HEAD_DIM) + NORM_EPS) * gain


def _inproj_prep_kernel(xc_ref, xl_ref, xp_ref, xn_ref, mod_ref, g_ref, w_ref,
                        cs_ref, sn_ref, csm_ref, snm_ref, qn_ref, kn_ref, mqn_ref, mkvn_ref,
                        wuq_ref, wukv_ref, ones_ref,
                        mu_ref, w0_ref, w2_ref, a0_ref, a2_ref, kk_ref, ka_ref, rk_ref, g2_ref,
                        qa_ref, ka_out, va_ref, qb_ref, kb_ref, vb_ref, qd_ref, kd_out, vd_ref,
                        lw_ref, rkd_ref, ra_ref, rb_ref, rr_ref, rv_ref, rg_ref, bonus_ref,
                        z_ref, zc_s, *, n_tiles):
    d = D_MODEL
    mod = mod_ref[...]
    j = pl.program_id(0) % n_tiles
    is_ctx = j == 0

    def modnorm(t):
        return (_rms(t, NORM_EPS) * g_ref[...] * (1.0 + mod[:, d:2 * d]) + mod[:, 0:d]).astype(BF16)

    h16 = modnorm(jnp.where(is_ctx, xc_ref[...], xl_ref[...]))
    h_all = jnp.concatenate([h16, modnorm(jnp.concatenate([xp_ref[...], xn_ref[...]], axis=0))], axis=0)
    zc_s[...] = _dot(h_all, w_ref[:, ABD_COLS:ABD_COLS + C_COLS_PAD])
    cs, sn = cs_ref[...], sn_ref[...]
    csm, snm = csm_ref[...], snm_ref[...]
    ones = ones_ref[...]
    hd = HEAD_DIM
    q_scale = HEAD_DIM ** -0.5

    def project(lo, hi):
        return _dot(h16, w_ref[:, lo:hi])

    def put_heads(ref, base, slab):
        ref[base] = slab[:, :hd].astype(BF16)
        ref[base + 1] = slab[:, hd:].astype(BF16)

    def put_heads_t(ref, base, slab, ones_rows=ONES_ROWS):
        st = slab.T.astype(BF16)
        if ones_rows:
            one = jnp.ones((ones_rows, st.shape[1]), BF16)
            ref[base] = jnp.concatenate([st[:hd], one], axis=0)
            ref[base + 1] = jnp.concatenate([st[hd:], one], axis=0)
        else:
            ref[base] = st[:hd]
            ref[base + 1] = st[hd:]

    z_ref[:, SEC_B:SEC_B + SEC_W] = project(SEC_B, SEC_B + SEC_W)
    z_ref[:, SEC_A:SEC_A + SEC_W] = project(SEC_A, SEC_A + SEC_W)

    cq = _rms(z_ref[:, SEC_B:SEC_B + MLA_Q_RANK], NORM_EPS) * mqn_ref[...]
    q = _dot(cq.astype(BF16), wuq_ref[...])
    ckv = _rms(z_ref[:, SEC_B + MLA_Q_RANK:SEC_B + MLA_Q_RANK + MLA_KV_RANK], NORM_EPS) * mkvn_ref[...]
    kv = _dot(ckv.astype(BF16), wukv_ref[...])

    z_ref[:, SEC_D:SEC_D + SEC_W] = project(SEC_D, SEC_D + SEC_W)

    for s in range(2):
        x = z_ref[:, SEC_A + LANES * s:SEC_A + LANES * (s + 1)]
        x = _rope_slab(_head_norm_slab(x, qn_ref[...], ones), cs, sn, ROPE_SHIFT) * (q_scale * LOG2E)
        put_heads_t(qa_ref, 2 * s, x, 0)
    x = z_ref[:, SEC_A + GROUP_WIDTH:SEC_A + GROUP_WIDTH + GQA_KV_WIDTH]
    put_heads(ka_out, 0, _rope_slab(_head_norm_slab(x, kn_ref[...], ones), cs, sn, ROPE_SHIFT))
    put_heads_t(va_ref, 0, z_ref[:, SEC_A + GROUP_WIDTH + GQA_KV_WIDTH:SEC_A + SEC_W])

    kr = _rope_slab(z_ref[:, SEC_B + MLA_Q_RANK + MLA_KV_RANK:SEC_B + SEC_W], csm, snm, ROPE_SHIFT_MLA)
    b_scale = MLA_QK_DIM ** -0.5 * LOG2E
    for h in range(GROUP_HEADS):
        qh = q[:, LANES * h:LANES * (h + 1)]
        qb_ref[h] = (_rope_slab(qh, csm, snm, ROPE_SHIFT_MLA) * b_scale).T.astype(BF16)
        kb_ref[h] = (kv[:, LANES * h:LANES * (h + 1)] + kr).astype(BF16)
    for s in range(2):
        put_heads_t(vb_ref, 2 * s, kv[:, LANES * (GROUP_HEADS + s):LANES * (GROUP_HEADS + s + 1)])

    def put_chunks_t(ref, base, slab, ones_rows):
        st = slab.T.astype(BF16)
        for hh in range(2):
            for cc in range(TM // WINDOW):
                blk = st[hd * hh:hd * (hh + 1), WINDOW * cc:WINDOW * (cc + 1)]
                if ones_rows:
                    blk = jnp.concatenate([blk, jnp.ones((ones_rows, WINDOW), BF16)], axis=0)
                ref[base + hh, cc] = blk

    for s in range(2):
        x = z_ref[:, SEC_D + LANES * s:SEC_D + LANES * (s + 1)]
        put_chunks_t(qd_ref, 2 * s, _rope_slab(x, cs, sn, ROPE_SHIFT) * q_scale, 0)
    put_heads(kd_out, 0, _rope_slab(z_ref[:, SEC_D + GROUP_WIDTH:SEC_D + GROUP_WIDTH + GQA_KV_WIDTH], cs, sn, ROPE_SHIFT))
    put_chunks_t(vd_ref, 0, z_ref[:, SEC_D + GROUP_WIDTH + GQA_KV_WIDTH:SEC_D + SEC_W], ONES_ROWS)

    z = zc_s[0:TM, :]
    rows = lax.broadcasted_iota(jnp.int32, z.shape, 0)
    prev_row = jnp.where(j <= 1, 0.0, zc_s[TM + 7:TM + 8, :])
    next_row = jnp.where(jnp.logical_or(j == 0, j == n_tiles - 1), 0.0, zc_s[TM + 8:TM + 9, :])
    z_prev = jnp.where(rows == 0, prev_row, pltpu.roll(z, 1, axis=0))
    z_next = jnp.where(rows == TM - 1, next_row, pltpu.roll(z, TM - 1, axis=0))
    zs = z + mu_ref[...] * (0.5 * (z_prev + z_next) - z)
    gw = GROUP_WIDTH
    r, k, v = zs[:, 0:gw], zs[:, gw:2 * gw], zs[:, 2 * gw:3 * gw]
    lora = zs[:, 3 * gw:3 * gw + LANES]
    gl = zs[:, 3 * gw + LANES:3 * gw + LANES + gw]
    wt = jnp.tanh(lora)

    def seg_sum(t):
        return jnp.concatenate([_mm_exact_rhs(t[:, :LANES], ones), _mm_exact_rhs(t[:, LANES:], ones)], axis=-1)

    rr_ref[...] = r
    rv_ref[...] = v
    rg_ref[...] = _mm3(_sigmoid(gl), g2_ref[...])
    kd_sum = None
    for dr in range(2):
        u = w0_ref[dr] + _mm3(wt, w2_ref[dr])
        soft = jnp.maximum(-u, 0.0) + jnp.log(1.0 + jnp.exp(-jnp.abs(u)))
        lw_ref[dr] = -jnp.exp(-soft - 0.5)
        gate = _sigmoid(a0_ref[dr] + _mm3(lora, a2_ref[dr]))
        kk = k * kk_ref[dr]
        kk = kk / jnp.maximum(jnp.sqrt(seg_sum(kk * kk)), 1e-12)
        kd = k * (1.0 + (gate - 1.0) * ka_ref[dr])
        rkd_ref[dr] = kd
        ra_ref[dr] = -kk
        rb_ref[dr] = kk * gate
        kd_sum = kd if kd_sum is None else kd_sum + kd
    bonus_ref[...] = seg_sum(r * kd_sum * rk_ref[...]) * v


def _inproj_prep(xc, xl, xc_map, xl_map, mod3, tabs, p, mod_row, n, n_tiles):
    cs, sn, csm, snm = tabs
    cols = p['w_in_p'].shape[1]

    def tab_spec():
        return pl.BlockSpec((TM, LANES), lambda i: (i % n_tiles, 0))

    def row_spec(w):
        return pl.BlockSpec((1, w), lambda i: (0, 0))

    def heads_out(nh, w):
        return (pl.BlockSpec((nh, TM, w), lambda i: (0, i, 0)),
                jax.ShapeDtypeStruct((nh, n, w), BF16))

    def heads_out_t(nh, rows=HEAD_DIM + ONES_ROWS):
        return (pl.BlockSpec((nh, None, rows, TM), lambda i: (0, i, 0, 0)),
                jax.ShapeDtypeStruct((nh, n // TM, rows, TM), BF16))

    def chunks_out_t(nh, rows):
        cpt = TM // WINDOW
        return (pl.BlockSpec((nh, cpt, rows, WINDOW), lambda i: (0, i, 0, 0)),
                jax.ShapeDtypeStruct((nh, n // WINDOW, rows, WINDOW), BF16))

    gw = GROUP_WIDTH
    dir_out = (pl.BlockSpec((2, TM, gw), lambda i: (0, i, 0)), jax.ShapeDtypeStruct((2, n, gw), F32))
    one_out = (pl.BlockSpec((TM, gw), lambda i: (i, 0)), jax.ShapeDtypeStruct((n, gw), F32))
    outs = [heads_out_t(4, HEAD_DIM), heads_out(2, 64), heads_out_t(2),
            heads_out_t(4, LANES), heads_out(4, 128), heads_out_t(4),
            chunks_out_t(4, HEAD_DIM), heads_out(2, 64), chunks_out_t(2, HEAD_DIM + ONES_ROWS)] + [dir_out] * 4 + [one_out] * 4
    blk8 = TM // 8
    last8 = xl.shape[0] // 8 - 1

    def full(shape):
        nd = len(shape)
        return pl.BlockSpec(shape, lambda i: (0,) * nd)

    return pl.pallas_call(
        functools.partial(_inproj_prep_kernel, n_tiles=n_tiles),
        grid=(n // TM,),
        in_specs=[pl.BlockSpec((TM, D_MODEL), xc_map), pl.BlockSpec((TM, D_MODEL), xl_map),
                  pl.BlockSpec((8, D_MODEL), lambda i: (jnp.maximum(xl_map(i)[0] * blk8 - 1, 0), 0)),
                  pl.BlockSpec((8, D_MODEL), lambda i: (jnp.minimum((xl_map(i)[0] + 1) * blk8, last8), 0)),
                  pl.BlockSpec((None, 1, 6 * D_MODEL), lambda i: (mod_row(i), 0, 0)),
                  row_spec(D_MODEL),
                  pl.BlockSpec((D_MODEL, cols), lambda i: (0, 0)),
                  tab_spec(), tab_spec(), tab_spec(), tab_spec(),
                  row_spec(LANES), row_spec(LANES), row_spec(MLA_Q_RANK), row_spec(MLA_KV_RANK),
                  pl.BlockSpec((MLA_Q_RANK, GROUP_HEADS * LANES), lambda i: (0, 0)),
                  pl.BlockSpec((MLA_KV_RANK, GROUP_HEADS * LANES + GROUP_WIDTH), lambda i: (0, 0)),
                  pl.BlockSpec((LANES, LANES), lambda i: (0, 0)),
                  full((1, C_COLS_PAD)),
                  full((2, 1, gw)), full((2, LANES, gw)), full((2, 1, gw)), full((2, LANES, gw)),
                  full((2, 1, gw)), full((2, 1, gw)), full((1, gw)), full((gw, gw))],
        out_specs=[o[0] for o in outs],
        out_shape=[o[1] for o in outs],
        scratch_shapes=[pltpu.VMEM((TM, ABD_COLS), F32), pltpu.VMEM((TM + 16, C_COLS_PAD), F32)],
        compiler_params=_cparams(("parallel",)),
        name="inproj_prep",
    )(xc, xl, xl, xl, mod3, p['g_pre_mix'], p['w_in_p'], cs, sn, csm, snm,
      p['qn'], p['kn'], p['mqn'], p['mkvn'], p['wuq'], p['wukv'], p['ones64'],
      p['mu'], p['w0'], p['w2p'], p['a0'], p['a2p'], p['k_k'], p['k_a'], p['r_k'], p['g2p'])


def _flash_kernel(q_ref, k_ref, vt_ref, o_ref, *, shared_kv, tq, tk, n_chunks):
    dv = vt_ref.shape[-2] - ONES_ROWS
    if shared_kv:
        streams = [(jnp.concatenate([q_ref[0], q_ref[1]], axis=1), 0)]
    else:
        streams = [(q_ref[0], 0), (q_ref[1], 1)]

    def scores(c):
        out = []
        for q, kv in streams:
            k = k_ref[kv, pl.ds(pl.multiple_of(c * tk, tk), tk), :]
            out.append(_dot(k, q))
        return out

    def softmax_pv(c, sts, carries):
        new = []
        for st, (_, kv), (m, acc) in zip(sts, streams, carries):
            m_new = jnp.maximum(m, jnp.max(st, axis=0, keepdims=True))
            pt = jnp.exp2(st - m_new).astype(BF16)
            acc = jnp.exp2(m - m_new) * acc + _dot(vt_ref[kv, c], pt)
            new.append((m_new, acc))
        return tuple(new)

    def run(n_chunks):
        unroll = next(u for u in (33, 11, 3, 2, 1) if n_chunks % u == 0)
        groups = n_chunks // unroll
        init = tuple((jnp.full((1, q.shape[1]), NEG_INF, F32), jnp.zeros((dv + ONES_ROWS, q.shape[1]), F32))
                     for q, _ in streams)

        def body(g, carries):
            queue = [scores(g * unroll + u) for u in range(min(LOOKAHEAD, unroll))]
            for u in range(unroll):
                if u + LOOKAHEAD < unroll:
                    queue.append(scores(g * unroll + u + LOOKAHEAD))
                carries = softmax_pv(g * unroll + u, queue.pop(0), carries)
            return carries

        carries = body(0, init) if groups == 1 else lax.fori_loop(0, groups, body, init)
        outs = [acc[:dv] / acc[dv:dv + 1] for _, acc in carries]
        if shared_kv:
            outs = [outs[0][:, :tq], outs[0][:, tq:]]
        o_ref[...] = jnp.concatenate(outs, axis=0).T.astype(o_ref.dtype)

    run(n_chunks)


def _flash(qt, k, vt, *, shared_kv, batch, s_tot, ctx_len):
    nh, _, dk, _ = qt.shape
    n = k.shape[1]
    dv = vt.shape[-2] - ONES_ROWS
    nkv = 1 if shared_kv else 2
    tq = tk = TM
    nq = s_tot // tq
    n_ctx = ctx_len // tq

    def outer(q_hbm, k_hbm, vt_hbm, o_hbm):
        for first, count, n_chunks in ((0, n_ctx, ctx_len // tk), (n_ctx, nq - n_ctx, s_tot // tk)):
            kern = functools.partial(_flash_kernel, shared_kv=shared_kv, tq=tq, tk=tk, n_chunks=n_chunks)
            tile = lambda b, i, first=first: b * nq + first + i
            pltpu.emit_pipeline(
                kern,
                grid=(batch, nh // 2, count),
                in_specs=[pl.BlockSpec((2, None, dk, tq), lambda b, p, i, tile=tile: (p, tile(b, i), 0, 0)),
                          pl.BlockSpec((nkv, s_tot, dk), lambda b, p, i: (p, b, 0)),
                          pl.BlockSpec((nkv, s_tot // tk, dv + ONES_ROWS, tk), lambda b, p, i: (p, b, 0, 0))],
                out_specs=[pl.BlockSpec((tq, 2 * dv), lambda b, p, i, tile=tile: (tile(b, i), p))],
            )(q_hbm, k_hbm, vt_hbm, o_hbm)

    any_spec = pl.BlockSpec(memory_space=pl.ANY)
    return pl.pallas_call(
        outer,
        in_specs=[any_spec, any_spec, any_spec],
        out_specs=any_spec,
        out_shape=jax.ShapeDtypeStruct((n, nh * dv), BF16),
        compiler_params=pltpu.CompilerParams(vmem_limit_bytes=VMEM_LIMIT),
        name="flash_shared" if shared_kv else "flash_split",
    )(qt, k, vt)


def _window_kernel(sink_ref, bias_ref, q_ref, k_ref, vt_ref, o_ref, *, ctx_len, s_tot, n_blocks):
    w = WINDOW
    g = pl.program_id(1)
    first_lat = ctx_len // w
    last_blk = s_tot // w - 1
    lane = lax.broadcasted_iota(jnp.int32, (1, 2 * w), 1)
    sk = jnp.where(lane < w, sink_ref[2 * g], sink_ref[2 * g + 1])
    k_ctx = k_ref[0:ctx_len, :]
    vt_ctx = jnp.concatenate([vt_ref[c] for c in range(first_lat)], axis=1)

    def rows(blk):
        return k_ref[pl.ds(pl.multiple_of(blk * w, w), w), :]

    def neighbours(i):
        jb = pl.program_id(2) * n_blocks + i
        return jnp.clip(jb - 1, 0, last_blk), jb, jnp.clip(jb + 1, 0, last_blk)

    def scores(i):
        pb, jb, nb = neighbours(i)
        kind = jnp.where(jb < first_lat, 0, jnp.where(jb == first_lat, 1, jnp.where(jb == last_blk, 3, 2)))
        qt = jnp.concatenate([q_ref[0, i], q_ref[1, i]], axis=1)
        kw = jnp.concatenate([rows(pb), rows(jb), rows(nb)], axis=0)
        return _dot(kw, qt) + bias_ref[kind], _dot(k_ctx, qt)

    queue = [scores(i) for i in range(min(LOOKAHEAD, n_blocks))]
    for i in range(n_blocks):
        if i + LOOKAHEAD < n_blocks:
            queue.append(scores(i + LOOKAHEAD))
        s_w, s_c = queue.pop(0)
        pb, jb, nb = neighbours(i)
        m = jnp.maximum(jnp.maximum(jnp.max(s_w, axis=0, keepdims=True),
                                    jnp.max(s_c, axis=0, keepdims=True)), sk)
        p_w = jnp.exp(s_w - m).astype(BF16)
        p_c = jnp.exp(s_c - m).astype(BF16)
        vt_w = jnp.concatenate([vt_ref[pb], vt_ref[jb], vt_ref[nb]], axis=1)
        acc = _dot(vt_w, p_w) + _dot(vt_ctx, p_c)
        o = acc[:HEAD_DIM] / (acc[HEAD_DIM:HEAD_DIM + 1] + jnp.exp(sk - m))
        o_ref[w * i:w * (i + 1), :] = jnp.concatenate([o[:, :w], o[:, w:]], axis=0).T.astype(o_ref.dtype)


def _window_bias():
    w = WINDOW
    c = jnp.arange(3 * w)[:, None]
    r = jnp.arange(2 * w)[None, :] % w
    band = (c >= r) & (c <= r + 2 * w)
    kinds = [jnp.zeros_like(band), band & (c >= w), band, band & (c < 2 * w)]
    return jnp.where(jnp.stack(kinds), 0.0, NEG_INF).astype(F32)


def _window_attn(sink, qt, k, vt, *, batch, s_tot, ctx_len):
    nh, _, dk, _ = qt.shape
    n = k.shape[1]
    nblk = s_tot // WINDOW
    wb = next(u for u in (WINDOW_BLOCKS, 11, 6, 3, 2, 1) if nblk % u == 0)
    nstep = nblk // wb
    kern = functools.partial(_window_kernel, ctx_len=ctx_len, s_tot=s_tot, n_blocks=wb)
    return pl.pallas_call(
        kern,
        grid=(batch, nh // 2, nstep),
        in_specs=[pl.BlockSpec(memory_space=pltpu.SMEM),
                  pl.BlockSpec((4, 3 * WINDOW, 2 * WINDOW), lambda b, g, j: (0, 0, 0)),
                  pl.BlockSpec((2, wb, dk, WINDOW), lambda b, g, j: (g, b * nstep + j, 0, 0)),
                  pl.BlockSpec((None, s_tot, dk), lambda b, g, j: (g, b, 0)),
                  pl.BlockSpec((None, nblk, dk + ONES_ROWS, WINDOW), lambda b, g, j: (g, b, 0, 0))],
        out_specs=pl.BlockSpec((WINDOW * wb, 2 * dk), lambda b, g, j: (b * nstep + j, g)),
        out_shape=jax.ShapeDtypeStruct((n, nh * dk), BF16),
        compiler_params=_cparams(("parallel", "parallel", "arbitrary")),
        name="window_attn",
    )(sink, _window_bias(), qt, k, vt)


def _rwkv_scan_kernel(lw_ref, k_ref, a_ref, b_ref, r_ref, v_ref, y_ref, s_ref, *, reverse):
    @pl.when(pl.program_id(0) == 0)
    def _():
        s_ref[...] = jnp.zeros_like(s_ref)

    batch = lw_ref.shape[0]
    n_chunks = TM // CHUNK
    hd = HEAD_DIM
    row = lax.broadcasted_iota(jnp.int32, (TM, TM), 0)
    col = lax.broadcasted_iota(jnp.int32, (TM, TM), 1)
    same = (row // CHUNK) == (col // CHUNK)
    before = (col > row) if reverse else (col < row)
    m_strict = jnp.logical_and(same, before)
    m_incl = jnp.logical_and(same, jnp.logical_or(before, row == col))
    m_incl16 = m_incl.astype(BF16)
    eye = (row == col).astype(F32)
    assert CHUNK // SUB == 4
    same_sub = (row // SUB) == (col // SUB)

    items = [(b, h) for b in range(batch) for h in range(GROUP_HEADS)]
    idx = range(len(items))
    e_pos, at_all, rt_all, bt_all, kt_all, v_all = [], [], [], [], [], []
    for b in range(batch):
        lw = lw_ref[b]
        cum = _mm_exact_lhs(m_incl16, lw)
        e_pos.append(jnp.exp(cum))
        e_neg = jnp.exp(-cum)
        at_all.append(a_ref[b] * jnp.exp(cum - lw))
        rt_all.append(r_ref[b] * e_pos[b])
        bt_all.append(b_ref[b] * e_neg)
        kt_all.append(k_ref[b] * e_neg)
        v_all.append(v_ref[b])

    def head(arrs, b, h):
        return arrs[b][:, hd * h:hd * (h + 1)]

    at = [head(at_all, b, h) for b, h in items]
    rt = [head(rt_all, b, h) for b, h in items]
    bt = [head(bt_all, b, h) for b, h in items]
    at16 = [x.astype(BF16) for x in at]
    rt16 = [x.astype(BF16) for x in rt]
    bt16 = [x.astype(BF16) for x in bt]
    kt16 = [head(kt_all, b, h).astype(BF16) for b, h in items]
    v16 = [head(v_all, b, h).astype(BF16) for b, h in items]
    ab = [jnp.where(m_strict, _dot_nt(at16[i], bt16[i]), 0.0) for i in idx]
    ak16 = [jnp.where(m_strict, _dot_nt(at16[i], kt16[i]), 0.0).astype(BF16) for i in idx]
    rk16 = [jnp.where(m_incl, _dot_nt(rt16[i], kt16[i]), 0.0).astype(BF16) for i in idx]
    n_c = TM // CHUNK

    def wide(m):
        out = m[:CHUNK]
        for c in range(1, n_c):
            out = out + m[CHUNK * c:CHUNK * (c + 1)]
        return out

    def bdiag16(wm):
        return jnp.where(same, jnp.concatenate([wm] * n_c, axis=0), 0.0).astype(BF16)

    def wdot(wl, bd16):
        return _dot(wl.astype(BF16), bd16)

    eye_w = wide(eye)
    ld = [jnp.where(same_sub, x, 0.0) for x in ab]
    lo16 = [jnp.where(same_sub, 0.0, x).astype(BF16) for x in ab]
    pw_w = [wide(x) for x in ld]
    pw_bd = [x.astype(BF16) for x in ld]
    td_w = [eye_w + x for x in pw_w]
    for _ in range(int(math.log2(SUB)) - 1):
        pw_w = [wdot(x, y) for x, y in zip(pw_w, pw_bd)]
        pw_bd = [bdiag16(x) for x in pw_w]
        td_w = [t + wdot(t, y) for t, y in zip(td_w, pw_bd)]
    m1_w = [wdot(t, x) for t, x in zip(td_w, lo16)]
    m1_bd = [bdiag16(x) for x in m1_w]
    m2_w = [wdot(x, y) for x, y in zip(m1_w, m1_bd)]
    m3_w = [wdot(x, bdiag16(y)) for x, y in zip(m1_w, m2_w)]
    nn_w = [eye_w + a1 + a2 + a3 for a1, a2, a3 in zip(m1_w, m2_w, m3_w)]
    akv = [_dot(ak16[i], v16[i]) for i in idx]

    rowc = lax.broadcasted_iota(jnp.int32, (TM, n_c * 2 * hd), 0) // CHUNK
    colc = lax.broadcasted_iota(jnp.int32, (TM, n_c * 2 * hd), 1) // (2 * hd)
    same_x = rowc == colc

    def spread16(xr):
        return jnp.where(same_x, jnp.concatenate([xr] * n_c, axis=1), 0.0).astype(BF16)

    def unwide(xw):
        return jnp.concatenate([xw[:, 2 * hd * c:2 * hd * (c + 1)] for c in range(n_c)], axis=0)

    tx = [wdot(td_w[i], spread16(jnp.concatenate([at[i], akv[i]], axis=-1))) for i in idx]
    wu = [unwide(wdot(nn_w[i], spread16(unwide(tx[i])))) for i in idx]
    wa = [x[:, :hd] for x in wu]
    u016 = [x[:, hd:].astype(BF16) for x in wu]
    rb_w = [wide(jnp.where(m_incl, _dot_nt(rt16[i], bt16[i]), 0.0)) for i in idx]
    rbwu = [unwide(wdot(rb_w[i], spread16(wu[i]))) for i in idx]
    yr16 = [(rt[i] + rbwu[i][:, :hd]).astype(BF16) for i in idx]
    y0 = [rbwu[i][:, hd:] + _dot(rk16[i], v16[i]) for i in idx]

    order = range(n_chunks - 1, -1, -1) if reverse else range(n_chunks)
    eye_h = eye[:hd, :hd]
    trans = {}
    for c in order:
        rs = slice(CHUNK * c, CHUNK * (c + 1))
        last = CHUNK * c if reverse else CHUNK * (c + 1) - 1
        for i, (b, h) in enumerate(items):
            decay = e_pos[b][last:last + 1, hd * h:hd * (h + 1)]
            g = (eye_h + _mm3tn(wa[i][rs], bt[i][rs])) * decay
            hh = (_dot_tn(u016[i][rs], bt16[i][rs]) + _dot_tn(v16[i][rs], kt16[i][rs])) * decay
            trans[c, i] = (g, hh)
    for c in order:
        rs = slice(CHUNK * c, CHUNK * (c + 1))
        s0 = [s_ref[b, h] for b, h in items]
        for i, (b, h) in enumerate(items):
            s_ref[b, h] = _mm3(s0[i], trans[c, i][0]) + trans[c, i][1]
        for b in range(batch):
            y_ref[b, rs, :] = jnp.concatenate(
                [_dot_nt(yr16[i][rs], s0[i].astype(BF16)) + y0[i][rs] for i, (bb, _) in enumerate(items) if bb == b],
                axis=-1)


def _mm3tn(a, b):
    ah, al = _split2(a)
    bh, bl = _split2(b)
    return _dot_tn(ah, bh) + (_dot_tn(ah, bl) + _dot_tn(al, bh))


def _rwkv_scan(lw, kd, a, b, r, v, *, d, batch, n_tiles):
    n, gw = r.shape
    s_tot = n // batch
    reverse = d == 1

    def tile(s):
        if reverse:
            return jnp.where(s == 0, 0, n_tiles - s)
        return s

    dspec = pl.BlockSpec((None, batch, TM, gw), lambda s: (d, 0, tile(s), 0))
    spec = pl.BlockSpec((batch, TM, gw), lambda s: (0, tile(s), 0))
    per_dir = [x.reshape(2, batch, s_tot, gw) for x in (lw, kd, a, b)]
    shared = [x.reshape(batch, s_tot, gw) for x in (r, v)]
    y = pl.pallas_call(
        functools.partial(_rwkv_scan_kernel, reverse=reverse),
        grid=(n_tiles,),
        in_specs=[dspec, dspec, dspec, dspec, spec, spec],
        out_specs=spec,
        out_shape=jax.ShapeDtypeStruct((batch, s_tot, gw), F32),
        scratch_shapes=[pltpu.VMEM((batch, GROUP_HEADS, HEAD_DIM, HEAD_DIM), F32)],
        compiler_params=_cparams(("arbitrary",)),
        name="rwkv_scan_rev" if reverse else "rwkv_scan_fwd",
    )(*per_dir, *shared)
    return y.reshape(n, gw)


def _out_mlp_kernel(xc_ref, xl_ref, oa_ref, ob_ref, od_ref, yf_ref, yr_ref, bonus_ref, g_ref,
                    lnw_ref, lnb_ref, ones_ref, mod_ref, gpost_ref, gpre_ref, gpm_ref,
                    wo_ref, w1_ref, w2_ref, xo_ref, *, ctx_first):
    d = D_MODEL
    gw = GROUP_WIDTH
    mod = mod_ref[...]
    ones = ones_ref[...]

    def seg_mean(t):
        sm = jnp.concatenate([_mm_exact_rhs(t[:, :LANES], ones), _mm_exact_rhs(t[:, LANES:], ones)], axis=-1)
        return sm * (1.0 / HEAD_DIM)

    yy = yf_ref[...] + yr_ref[...]
    dlt = yy - seg_mean(yy)
    yn = dlt * lax.rsqrt(seg_mean(dlt * dlt) + RWKV_GN_EPS) * lnw_ref[...] + lnb_ref[...]
    oc = ((yn + bonus_ref[...]) * g_ref[...]).astype(BF16)

    y = _dot(oa_ref[...], wo_ref[0:gw, :])
    y += _dot(ob_ref[...], wo_ref[gw:2 * gw, :])
    y += _dot(oc, wo_ref[2 * gw:3 * gw, :])
    y += _dot(od_ref[...], wo_ref[3 * gw:4 * gw, :])
    x = jnp.where(jnp.logical_and(ctx_first, pl.program_id(1) == 0), xc_ref[...], xl_ref[...])
    x1 = x + mod[:, 2 * d:3 * d] * (_rms(y, NORM_EPS) * gpost_ref[...])
    h = _rms(x1, NORM_EPS) * gpre_ref[...] * (1.0 + mod[:, 4 * d:5 * d]) + mod[:, 3 * d:4 * d]
    u = jnp.maximum(_dot(h.astype(BF16), w1_ref[...]), 0.0)
    zz = _dot((u * u).astype(BF16), w2_ref[...])
    xo_ref[...] = x1 + mod[:, 5 * d:6 * d] * (_rms(zz, NORM_EPS) * gpm_ref[...])


def _out_mlp(xc, xl, xc_map, xl_map, oa, ob, od, yf, yr, bonus, rg, mod3, p, *, batch, n_tiles, skip_ctx):
    d = D_MODEL
    gw = GROUP_WIDTH
    off = 1 if skip_ctx else 0
    nt = n_tiles - off

    def tile(b, j):
        return (b * n_tiles + off + j, 0)

    def mrow(b, j):
        return (jnp.where(off + j == 0, batch, b), 0, 0)

    ospec = pl.BlockSpec((TM, gw), tile)
    row = pl.BlockSpec((1, d), lambda b, j: (0, 0))
    grow = pl.BlockSpec((1, gw), lambda b, j: (0, 0))

    def wspec(shape):
        return pl.BlockSpec(shape, lambda b, j: (0, 0), pipeline_mode=pl.Buffered(1))

    return pl.pallas_call(
        functools.partial(_out_mlp_kernel, ctx_first=not skip_ctx),
        grid=(batch, nt),
        in_specs=[pl.BlockSpec((TM, d), lambda b, j: xc_map(b, off + j)),
                  pl.BlockSpec((TM, d), lambda b, j: xl_map(b, off + j)),
                  ospec, ospec, ospec, ospec, ospec, ospec, ospec,
                  grow, grow, pl.BlockSpec((LANES, LANES), lambda b, j: (0, 0)),
                  pl.BlockSpec((None, 1, 6 * d), mrow),
                  row, row, row,
                  wspec((d, d)), wspec((d, D_FF)), wspec((D_FF, d))],
        out_specs=pl.BlockSpec((TM, d), lambda b, j: (b * nt + j, 0)),
        out_shape=jax.ShapeDtypeStruct((batch * nt * TM, d), F32),
        compiler_params=_cparams(("parallel", "parallel")),
        name="out_mlp",
    )(xc, xl, oa, ob, od, yf, yr, bonus, rg, p['ln_w'], p['ln_b'], p['ones64'], mod3,
      p['g_post_mix'], p['g_pre_mlp'], p['g_post_mlp'], p['w_out'], p['w_mlp1'], p['w_mlp2'])


def _rope_tables(seq, ctx_len):
    t = jnp.arange(seq, dtype=jnp.int32)
    row = (t // GRID_W).astype(F32)
    col = (t % GRID_W).astype(F32)

    def tables(rot_dim, lane_dim):
        n = rot_dim // 4
        inv = ROPE_THETA ** (-jnp.arange(n, dtype=F32) / n)
        ar, ac = row[:, None] * inv, col[:, None] * inv
        cos = jnp.concatenate([jnp.cos(ar), jnp.cos(ar), jnp.cos(ac), jnp.cos(ac)], axis=-1)
        sin = jnp.concatenate([-jnp.sin(ar), jnp.sin(ar), -jnp.sin(ac), jnp.sin(ac)], axis=-1)
        return cos, sin

    def with_ctx(tab, fill):
        return jnp.concatenate([jnp.full((ctx_len, tab.shape[1]), fill, F32), tab], axis=0)

    cos, sin = tables(HEAD_DIM, LANES)
    cs = with_ctx(jnp.tile(cos, (1, LANES // HEAD_DIM)), 1.0)
    sn = with_ctx(jnp.tile(sin, (1, LANES // HEAD_DIM)), 0.0)
    cosm, sinm = tables(MLA_ROPE_DIM, LANES)
    pad_l, pad_r = MLA_NOPE_DIM, LANES - MLA_QK_DIM
    csm = with_ctx(jnp.pad(cosm, ((0, 0), (pad_l, pad_r)), constant_values=1.0), 1.0)
    snm = with_ctx(jnp.pad(sinm, ((0, 0), (pad_l, pad_r))), 0.0)
    return cs, sn, csm, snm


def _layer_params(l, w):
    d = D_MODEL
    gw = GROUP_WIDTH
    w_in = w['w_in'][l].astype(BF16)
    a_end, b_end, c_end = A_COLS, A_COLS + B_COLS, A_COLS + B_COLS + C_COLS
    lowrank = MLA_Q_RANK + MLA_KV_RANK
    zeros = lambda c: jnp.zeros((d, c), BF16)
    w_in_p = jnp.concatenate([
        w_in[:, :a_end],
        w_in[:, a_end:a_end + lowrank], zeros(MLA_NOPE_DIM), w_in[:, a_end + lowrank:b_end], zeros(LANES - MLA_QK_DIM),
        w_in[:, c_end:],
        w_in[:, b_end:c_end], zeros(C_COLS_PAD - C_COLS)], axis=1)
    wuq = w['mla_w_uq'][l].reshape(MLA_Q_RANK, GROUP_HEADS, MLA_QK_DIM)
    wuq = jnp.pad(wuq, ((0, 0), (0, 0), (0, LANES - MLA_QK_DIM))).reshape(MLA_Q_RANK, GROUP_HEADS * LANES)
    wukv = w['mla_w_ukv'][l].reshape(MLA_KV_RANK, GROUP_HEADS, MLA_NOPE_DIM + HEAD_DIM)
    wk = jnp.pad(wukv[:, :, :MLA_NOPE_DIM], ((0, 0), (0, 0), (0, LANES - MLA_NOPE_DIM)))
    wukv_p = jnp.concatenate([wk.reshape(MLA_KV_RANK, GROUP_HEADS * LANES),
                              wukv[:, :, MLA_NOPE_DIM:].reshape(MLA_KV_RANK, gw)], axis=1)
    lane = jnp.arange(LANES)
    ones64 = ((lane[:, None] // HEAD_DIM) == (lane[None, :] // HEAD_DIM)).astype(BF16)
    zrows = lambda r: jnp.zeros((2, r, gw), F32)
    return {
        'w_in_p': w_in_p,
        'g_pre_mix': w['g_pre_mix'][l].reshape(1, d),
        'g_post_mix': w['g_post_mix'][l].reshape(1, d),
        'g_pre_mlp': w['g_pre_mlp'][l].reshape(1, d),
        'g_post_mlp': w['g_post_mlp'][l].reshape(1, d),
        'qn': jnp.tile(w['gqa_q_norm'][l], 2).reshape(1, LANES),
        'kn': jnp.tile(w['gqa_k_norm'][l], 2).reshape(1, LANES),
        'mqn': w['mla_q_norm'][l].reshape(1, MLA_Q_RANK),
        'mkvn': w['mla_kv_norm'][l].reshape(1, MLA_KV_RANK),
        'wuq': wuq.astype(BF16),
        'wukv': wukv_p.astype(BF16),
        'ones64': ones64,
        'mu': jnp.pad(w['rwkv_mu'][l], (0, C_COLS_PAD - C_COLS)).reshape(1, C_COLS_PAD),
        'w0': w['rwkv_w0'][l].reshape(2, 1, gw),
        'w2p': jnp.concatenate([w['rwkv_w2'][l], zrows(LANES - RWKV_DECAY_LORA)], axis=1),
        'a0': w['rwkv_a0'][l].reshape(2, 1, gw),
        'a2p': jnp.concatenate([zrows(RWKV_DECAY_LORA), w['rwkv_a2'][l],
                                zrows(LANES - RWKV_DECAY_LORA - RWKV_ICLR_LORA)], axis=1),
        'k_k': w['rwkv_k_k'][l].reshape(2, 1, gw),
        'k_a': w['rwkv_k_a'][l].reshape(2, 1, gw),
        'r_k': w['rwkv_r_k'][l].reshape(1, gw),
        'g2p': jnp.pad(w['rwkv_g2'][l], ((0, gw - RWKV_GATE_LORA), (0, 0))),
        'ln_w': w['rwkv_ln_w'][l].reshape(1, gw),
        'ln_b': w['rwkv_ln_b'][l].reshape(1, gw),
        'sink': w['swa_sink'][l],
        'w_out': w['w_out'][l].astype(BF16),
        'w_mlp1': w['w_mlp1'][l].astype(BF16),
        'w_mlp2': w['w_mlp2'][l].astype(BF16),
    }


def kernel(x, c, ctx, c_ctx, w_mod, b_mod, g_pre_mix, g_post_mix, g_pre_mlp, g_post_mlp, w_in, gqa_q_norm, gqa_k_norm, mla_q_norm, mla_kv_norm, mla_w_uq, mla_w_ukv, rwkv_mu, rwkv_w0, rwkv_w2, rwkv_a0, rwkv_a2, rwkv_k_k, rwkv_k_a, rwkv_r_k, rwkv_g2, rwkv_ln_w, rwkv_ln_b, swa_sink, w_out, w_mlp1, w_mlp2):
    batch, seq, d = x.shape
    ctx_len = ctx.shape[1]
    depth = w_mod.shape[0]
    assert d == D_MODEL and ctx_len == TM and seq % TM == 0 and seq % GRID_W == 0
    assert batch + 1 <= 8
    s_tot = ctx_len + seq
    n_tiles = s_tot // TM
    w = dict(w_in=w_in, g_pre_mix=g_pre_mix, g_post_mix=g_post_mix, g_pre_mlp=g_pre_mlp,
             g_post_mlp=g_post_mlp, gqa_q_norm=gqa_q_norm, gqa_k_norm=gqa_k_norm, mla_q_norm=mla_q_norm,
             mla_kv_norm=mla_kv_norm, mla_w_uq=mla_w_uq, mla_w_ukv=mla_w_ukv, rwkv_mu=rwkv_mu,
             rwkv_w0=rwkv_w0, rwkv_w2=rwkv_w2, rwkv_a0=rwkv_a0, rwkv_a2=rwkv_a2, rwkv_k_k=rwkv_k_k,
             rwkv_k_a=rwkv_k_a, rwkv_r_k=rwkv_r_k, rwkv_g2=rwkv_g2, rwkv_ln_w=rwkv_ln_w,
             rwkv_ln_b=rwkv_ln_b, swa_sink=swa_sink, w_out=w_out, w_mlp1=w_mlp1, w_mlp2=w_mlp2)

    ct = jnp.concatenate([c, c_ctx[None, :], jnp.zeros((8 - batch - 1, d), F32)], axis=0).T
    mods = _modulation(ct, w_mod, b_mod, batch + 1)

    def mod_row(i):
        return jnp.where(i % n_tiles == 0, batch, i // n_tiles)

    tabs = _rope_tables(seq, ctx_len)
    geo = dict(batch=batch, s_tot=s_tot, ctx_len=ctx_len)
    n = batch * s_tot
    n_lat = n_tiles - 1
    xc, xl = ctx.reshape(batch * ctx_len, d), x.reshape(batch * seq, d)
    xc_map = lambda b, j: (b, 0)
    xl_map = lambda b, j: (b * n_lat + jnp.maximum(j - 1, 0), 0)
    for l in range(depth):
        p = _layer_params(l, w)
        mod3 = mods[l].reshape(8, 1, 6 * d)
        flat = lambda m: (lambda i: m(i // n_tiles, i % n_tiles))
        qa, ka, va, qb, kb, vb, qd, kd, vd, lw, rkd, ra, rb, rr, rv, rg, bonus = _inproj_prep(
            xc, xl, flat(xc_map), flat(xl_map), mod3, tabs, p, mod_row, n, n_tiles)
        oa = _flash(qa, ka, va, shared_kv=True, **geo)
        ob = _flash(qb, kb, vb, shared_kv=False, **geo)
        od = _window_attn(p['sink'], qd, kd, vd, **geo)
        yf = _rwkv_scan(lw, rkd, ra, rb, rr, rv, d=0, batch=batch, n_tiles=n_tiles)
        yr = _rwkv_scan(lw, rkd, ra, rb, rr, rv, d=1, batch=batch, n_tiles=n_tiles)
        xs = _out_mlp(xc, xl, xc_map, xl_map, oa, ob, od, yf, yr, bonus, rg, mod3, p,
                      batch=batch, n_tiles=n_tiles, skip_ctx=l == depth - 1)
        xc = xl = xs
        xc_map = lambda b, j: (b * n_tiles, 0)
        xl_map = lambda b, j: (b * n_tiles + j, 0)
    return xs.reshape(batch, seq, d)
```
